```python
import math
import jax, jax.numpy as jnp
from jax import lax
import numpy as np

D_MODEL = 1024
BATCH = 2
SEQ = 8192
DEPTH = 4

N_A_LAYERS = DEPTH // 2
N_B_LAYERS = DEPTH - N_A_LAYERS
GMLP_HALF = 3 * D_MODEL
GMLP_GROUPS = 8
GMLP_CHUNK = 128
N_HEADS = 16
HEAD_DIM = D_MODEL // N_HEADS
MOBA_BLOCK = 256
MOBA_TOPK = 3
Q_CHUNK = 32
N_BUCKETS = 32
MAX_DISTANCE = 128
N_EXPERTS = 16
N_GROUPS = 4
EXPERTS_PER_GROUP = N_EXPERTS // N_GROUPS
TOP_K = 2
D_EXPERT = D_MODEL // 4
DEEPNORM_ALPHA = (2 * DEPTH) ** 0.25
DEEPNORM_BETA = (8 * DEPTH) ** -0.25
LN_EPS = 1e-5
NEG_INF = -1e30

kernel_name = 'yoco_gmlp_moba_groupmoe_deepnorm'


def layer_norm(x, g, b):
    xf = x.astype(jnp.float32)
    mu = jnp.mean(xf, axis=-1, keepdims=True)
    var = jnp.mean(jnp.square(xf - mu), axis=-1, keepdims=True)
    y = (xf - mu) * lax.rsqrt(var + LN_EPS)
    return (y * g.astype(jnp.float32) + b.astype(jnp.float32)).astype(x.dtype)


def t5_bucket(dist):
    n = jnp.maximum(dist, 0)
    max_exact = N_BUCKETS // 2
    nf = jnp.maximum(n, 1).astype(jnp.float32)
    large = max_exact + (jnp.log(nf / max_exact) / math.log(MAX_DISTANCE / max_exact)
                         * (N_BUCKETS - max_exact)).astype(jnp.int32)
    large = jnp.minimum(large, N_BUCKETS - 1)
    return jnp.where(n < max_exact, n, large)


def gmlp_mixer(x, w_in, v_g, v_b, w_s, b_s, w_out):
    B, S, _ = x.shape
    z = jax.nn.gelu(x @ w_in)
    u, v = jnp.split(z, 2, axis=-1)
    v = layer_norm(v, v_g, v_b)
    v = v.reshape(B, S // GMLP_CHUNK, GMLP_CHUNK, GMLP_GROUPS, GMLP_HALF // GMLP_GROUPS)
    causal = jnp.tril(jnp.ones((GMLP_CHUNK, GMLP_CHUNK), dtype=w_s.dtype))
    sv = jnp.einsum('gts,bnsgc->bntgc', w_s * causal, v) + b_s.T[None, None, :, :, None]
    return (u * sv.reshape(B, S, GMLP_HALF)) @ w_out


def shared_kv(x, w_k, w_v):
    B, S, _ = x.shape
    nb = -(-S // MOBA_BLOCK)
    pad = nb * MOBA_BLOCK - S
    k = (x @ w_k).reshape(B, S, N_HEADS, HEAD_DIM)
    v = (x @ w_v).reshape(B, S, N_HEADS, HEAD_DIM)
    k = jnp.pad(k, ((0, 0), (0, pad), (0, 0), (0, 0)))
    v = jnp.pad(v, ((0, 0), (0, pad), (0, 0), (0, 0)))
    kb = k.reshape(B, nb, MOBA_BLOCK, N_HEADS, HEAD_DIM)
    vb = v.reshape(B, nb, MOBA_BLOCK, N_HEADS, HEAD_DIM)
    k_mean = jnp.mean(kb, axis=2)
    return kb, vb, k_mean


def moba_attention(x, w_q, w_o, kb, vb, k_mean, rel_bias):
    B, S, _ = x.shape
    nb = kb.shape[1]
    k_eff = min(MOBA_TOPK, nb)
    q = (x @ w_q).reshape(B, S, N_HEADS, HEAD_DIM) * (HEAD_DIM ** -0.5)
    kbh = kb.transpose(0, 3, 1, 2, 4)
    vbh = vb.transpose(0, 3, 1, 2, 4)
    b_ix = jnp.arange(B)[:, None, None, None]
    h_ix = jnp.arange(N_HEADS)[None, None, :, None]
    h_ix5 = jnp.arange(N_HEADS)[None, None, :, None, None]
    blk_pos = jnp.arange(MOBA_BLOCK)
    sel_j = jnp.arange(k_eff)
    blk_ids = jnp.arange(nb)

    def chunk(c):
        t0 = c * Q_CHUNK
        qc = lax.dynamic_slice_in_dim(q, t0, Q_CHUNK, axis=1)
        own = t0 // MOBA_BLOCK
        pos = t0 + jnp.arange(Q_CHUNK)
        gate = jnp.einsum('bqhd,bnhd->bqhn', qc, k_mean).astype(jnp.float32)
        gate = jnp.where(blk_ids < own, gate, NEG_INF)
        _, idx = lax.top_k(gate, k_eff)
        valid = sel_j < own
        k_sel = kbh[b_ix, h_ix, idx]
        v_sel = vbh[b_ix, h_ix, idx]
        s_sel = jnp.einsum('bqhd,bqhksd->bqhks', qc, k_sel).astype(jnp.float32)
        key_pos = idx[..., None] * MOBA_BLOCK + blk_pos
        bucket = t5_bucket(pos[None, :, None, None, None] - key_pos)
        s_sel = s_sel + rel_bias[bucket, h_ix5].astype(jnp.float32)
        s_sel = jnp.where(valid[:, None], s_sel, NEG_INF)
        k_own = lax.dynamic_index_in_dim(kb, own, axis=1, keepdims=False)
        v_own = lax.dynamic_index_in_dim(vb, own, axis=1, keepdims=False)
        s_own = jnp.einsum('bqhd,bshd->bqhs', qc, k_own).astype(jnp.float32)
        dist = pos[:, None] - (own * MOBA_BLOCK + blk_pos)[None, :]
        s_own = s_own + rel_bias[t5_bucket(dist)].transpose(0, 2, 1)[None].astype(jnp.float32)
        s_own = jnp.where((dist >= 0)[:, None, :], s_own, NEG_INF)
        scores = jnp.concatenate(
            [s_own, s_sel.reshape(B, Q_CHUNK, N_HEADS, k_eff * MOBA_BLOCK)], axis=-1)
        p = jax.nn.softmax(scores, axis=-1).astype(vb.dtype)
        p_own = p[..., :MOBA_BLOCK]
        p_sel = p[..., MOBA_BLOCK:].reshape(B, Q_CHUNK, N_HEADS, k_eff, MOBA_BLOCK)
        return (jnp.einsum('bqhs,bshd->bqhd', p_own, v_own)
                + jnp.einsum('bqhks,bqhksd->bqhd', p_sel, v_sel))

    out = lax.map(chunk, jnp.arange(S // Q_CHUNK))
    out = out.transpose(1, 0, 2, 3, 4).reshape(B, S, D_MODEL)
    return out @ w_o


def route(xt, w_router, b_router):
    logits = (xt @ w_router).astype(jnp.float32) + b_router.astype(jnp.float32)
    probs = jax.nn.softmax(logits, axis=-1)
    grouped = probs.reshape(-1, N_GROUPS, EXPERTS_PER_GROUP)
    group_score = lax.top_k(grouped, TOP_K)[0].sum(-1)
    gsel = jnp.argmax(group_score, axis=-1)
    in_group = jnp.repeat(jax.nn.one_hot(gsel, N_GROUPS, dtype=jnp.bool_), EXPERTS_PER_GROUP, axis=-1)
    masked = jnp.where(in_group, probs, -1.0)
    top_w, top_i = lax.top_k(masked, TOP_K)
    top_w = top_w / jnp.sum(top_w, axis=-1, keepdims=True)
    return jnp.einsum('tk,tke->te', top_w, jax.nn.one_hot(top_i, N_EXPERTS, dtype=jnp.float32))


def moe(x, w_router, b_router, w1, w3, w2):
    B, S, D = x.shape
    xt = x.reshape(B * S, D)
    gates = route(xt, w_router, b_router).astype(x.dtype)
    h = jax.nn.silu(jnp.einsum('td,edf->tef', xt, w1)) * jnp.einsum('td,edf->tef', xt, w3)
    y = jnp.einsum('tef,efd->td', h * gates[:, :, None], w2)
    return y.reshape(B, S, D)


def setup_inputs(seed: int = 0) -> dict:
    key = jax.random.key(seed)
    ks = jax.random.split(key, 20)

    def nrm(k, shape, scale):
        return jax.random.normal(k, shape, jnp.float32) * scale

    return {
        'x': nrm(ks[0], (BATCH, SEQ, D_MODEL), 1.0),
        'ln_g': 1.0 + nrm(ks[1], (DEPTH, 2, D_MODEL), 0.02),
        'ln_b': nrm(ks[2], (DEPTH, 2, D_MODEL), 0.02),
        'a_w_in': nrm(ks[3], (N_A_LAYERS, D_MODEL, 2 * GMLP_HALF), D_MODEL ** -0.5),
        'a_v_g': 1.0 + nrm(ks[4], (N_A_LAYERS, GMLP_HALF), 0.02),
        'a_v_b': nrm(ks[5], (N_A_LAYERS, GMLP_HALF), 0.02),
        'a_w_s': nrm(ks[6], (N_A_LAYERS, GMLP_GROUPS, GMLP_CHUNK, GMLP_CHUNK), GMLP_CHUNK ** -0.5),
        'a_b_s': 1.0 + nrm(ks[7], (N_A_LAYERS, GMLP_GROUPS, GMLP_CHUNK), 0.02),
        'a_w_out': nrm(ks[8], (N_A_LAYERS, GMLP_HALF, D_MODEL), GMLP_HALF ** -0.5 * DEEPNORM_BETA),
        'w_k': nrm(ks[9], (D_MODEL, D_MODEL), D_MODEL ** -0.5),
        'w_v': nrm(ks[10], (D_MODEL, D_MODEL), D_MODEL ** -0.5 * DEEPNORM_BETA),
        'b_w_q': nrm(ks[11], (N_B_LAYERS, D_MODEL, D_MODEL), D_MODEL ** -0.5),
        'b_w_o': nrm(ks[12], (N_B_LAYERS, D_MODEL, D_MODEL), D_MODEL ** -0.5 * DEEPNORM_BETA),
        'rel_bias': nrm(ks[13], (N_BUCKETS, N_HEADS), 0.5),
        'w_router': nrm(ks[14], (D_MODEL, N_EXPERTS), D_MODEL ** -0.5),
        'b_router': nrm(ks[15], (N_EXPERTS,), 0.01),
        'e_w1': nrm(ks[16], (DEPTH, N_EXPERTS, D_MODEL, D_EXPERT), D_MODEL ** -0.5),
        'e_w3': nrm(ks[17], (DEPTH, N_EXPERTS, D_MODEL, D_EXPERT), D_MODEL ** -0.5),
        'e_w2': nrm(ks[18], (DEPTH, N_EXPERTS, D_EXPERT, D_MODEL), D_EXPERT ** -0.5 * DEEPNORM_BETA),
    }


def reference(x, ln_g, ln_b, a_w_in, a_v_g, a_v_b, a_w_s, a_b_s, a_w_out, w_k, w_v,
              b_w_q, b_w_o, rel_bias, w_router, b_router, e_w1, e_w3, e_w2):
    kb = vb = k_mean = None
    for l in range(DEPTH):
        if l < N_A_LAYERS:
            h = gmlp_mixer(x, a_w_in[l], a_v_g[l], a_v_b[l], a_w_s[l], a_b_s[l], a_w_out[l])
        else:
            if l == N_A_LAYERS:
                kb, vb, k_mean = shared_kv(x, w_k, w_v)
            j = l - N_A_LAYERS
            h = moba_attention(x, b_w_q[j], b_w_o[j], kb, vb, k_mean, rel_bias)
        x = layer_norm(DEEPNORM_ALPHA * x + h, ln_g[l, 0], ln_b[l, 0])
        f = moe(x, w_router, b_router, e_w1[l], e_w3[l], e_w2[l])
        x = layer_norm(DEEPNORM_ALPHA * x + f, ln_g[l, 1], ln_b[l, 1])
    return x
```

```python
import functools
import math

import numpy as np
import jax
import jax.numpy as jnp
from jax import lax
from jax.experimental import pallas as pl
from jax.experimental.pallas import tpu as pltpu

D_MODEL = 1024
DEPTH = 4
N_A_LAYERS = DEPTH // 2
GMLP_HALF = 3 * D_MODEL
GMLP_GROUPS = 8
GMLP_GROUP_WIDTH = GMLP_HALF // GMLP_GROUPS
GMLP_CHUNK = 128
N_HEADS = 16
HEAD_DIM = D_MODEL // N_HEADS
MOBA_BLOCK = 256
MOBA_TOPK = 3
N_BUCKETS = 32
MAX_DISTANCE = 128
N_EXPERTS = 16
N_GROUPS = 4
EXPERTS_PER_GROUP = N_EXPERTS // N_GROUPS
D_EXPERT = D_MODEL // 4
DEEPNORM_ALPHA = (2 * DEPTH) ** 0.25
LN_EPS = 1e-5
NEG_INF = -1e30

V7X_VMEM_LIMIT_BYTES = 56 * 1024 * 1024

BF16 = jnp.bfloat16
F32 = jnp.float32


def _params(*semantics):
    return pltpu.CompilerParams(dimension_semantics=semantics,
                                vmem_limit_bytes=V7X_VMEM_LIMIT_BYTES)


def _const_spec(shape):
    zeros = (0,) * len(shape)
    return pl.BlockSpec(shape, lambda *_: zeros, pipeline_mode=pl.Buffered(1))


def _layer_norm(y, g, b):
    mu = jnp.mean(y, axis=-1, keepdims=True)
    d = y - mu
    var = jnp.mean(d * d, axis=-1, keepdims=True)
    return d * lax.rsqrt(var + LN_EPS) * g + b


GMLP_TM = 256


def _gmlp_kernel(x_ref, w_in_ref, vg_ref, vb_ref, ws_ref, bs_ref, w_out_ref,
                 lng_ref, lnb_ref, o_ref, vn_ref):
    x = x_ref[...]
    xb = x.astype(BF16)
    v = jax.nn.gelu(jnp.dot(xb, w_in_ref[:, GMLP_HALF:], preferred_element_type=F32))
    vn_ref[...] = _layer_norm(v, vg_ref[...], vb_ref[...]).astype(BF16)

    row = lax.broadcasted_iota(jnp.int32, (GMLP_CHUNK, GMLP_CHUNK), 0)
    col = lax.broadcasted_iota(jnp.int32, (GMLP_CHUNK, GMLP_CHUNK), 1)
    causal = row >= col
    acc = jnp.zeros((GMLP_TM, D_MODEL), F32)
    for g in range(GMLP_GROUPS):
        lo, hi = g * GMLP_GROUP_WIDTH, (g + 1) * GMLP_GROUP_WIDTH
        u = jax.nn.gelu(jnp.dot(xb, w_in_ref[:, lo:hi], preferred_element_type=F32))
        w_s = jnp.where(causal, ws_ref[g], 0.0).astype(BF16)
        b_s = bs_ref[:, g:g + 1]
        parts = []
        for c in range(GMLP_TM // GMLP_CHUNK):
            vc = vn_ref[c * GMLP_CHUNK:(c + 1) * GMLP_CHUNK, lo:hi]
            parts.append(jnp.dot(w_s, vc, preferred_element_type=F32) + b_s)
        sv = jnp.concatenate(parts, axis=0)
        gated = (u * sv).astype(BF16)
        acc = acc + jnp.dot(gated, w_out_ref[lo:hi, :], preferred_element_type=F32)
    o_ref[...] = _layer_norm(DEEPNORM_ALPHA * x + acc, lng_ref[...], lnb_ref[...])


def _gmlp_layer(x, w_in, v_g, v_b, w_s, b_s_t, w_out, ln_g, ln_b):
    t = x.shape[0]
    row_spec = pl.BlockSpec((GMLP_TM, D_MODEL), lambda i: (i, 0))
    return pl.pallas_call(
        _gmlp_kernel,
        out_shape=jax.ShapeDtypeStruct((t, D_MODEL), F32),
        grid=(t // GMLP_TM,),
        in_specs=[row_spec,
                  _const_spec(w_in.shape), _const_spec(v_g.shape), _const_spec(v_b.shape),
                  _const_spec(w_s.shape), _const_spec(b_s_t.shape), _const_spec(w_out.shape),
                  _const_spec(ln_g.shape), _const_spec(ln_b.shape)],
        out_specs=row_spec,
        scratch_shapes=[pltpu.VMEM((GMLP_TM, GMLP_HALF), BF16)],
        compiler_params=_params("arbitrary"),
        name="gmlp_layer",
    )(x, w_in, v_g, v_b, w_s, b_s_t, w_out, ln_g, ln_b)


ROUTER_TM = 2048


def _router_kernel(x_ref, wr_ref, br_ref, o_ref):
    xb = x_ref[...].astype(BF16)
    logits = lax.dot_general(wr_ref[...], xb, (((1,), (1,)), ((), ())),
                             preferred_element_type=F32) + br_ref[...]
    m = jnp.max(logits, axis=0, keepdims=True)
    ex = jnp.exp(logits - m)
    probs = ex / jnp.sum(ex, axis=0, keepdims=True)
    p = [probs[e:e + 1, :] for e in range(N_EXPERTS)]

    scores = []
    for g in range(N_GROUPS):
        q = p[g * EXPERTS_PER_GROUP:(g + 1) * EXPERTS_PER_GROUP]
        best = None
        for a in range(EXPERTS_PER_GROUP):
            for b in range(a + 1, EXPERTS_PER_GROUP):
                s = q[a] + q[b]
                best = s if best is None else jnp.maximum(best, s)
        scores.append(best)
    best_score = scores[0]
    gsel = jnp.zeros_like(best_score, dtype=jnp.int32)
    for g in range(1, N_GROUPS):
        better = scores[g] > best_score
        best_score = jnp.where(better, scores[g], best_score)
        gsel = jnp.where(better, g, gsel)

    rows = []
    for e in range(N_EXPERTS):
        g = e // EXPERTS_PER_GROUP
        rank = jnp.zeros_like(gsel)
        for e2 in range(g * EXPERTS_PER_GROUP, (g + 1) * EXPERTS_PER_GROUP):
            if e2 == e:
                continue
            ahead = (p[e2] > p[e]) | ((p[e2] == p[e]) & (e2 < e))
            rank = rank + ahead.astype(jnp.int32)
        chosen = (gsel == g) & (rank < 2)
        rows.append(jnp.where(chosen, p[e] / best_score, 0.0))
    o_ref[...] = jnp.concatenate(rows, axis=0)


def _router(x, wr_t, br):
    t = x.shape[0]
    return pl.pallas_call(
        _router_kernel,
        out_shape=jax.ShapeDtypeStruct((N_EXPERTS, t), F32),
        grid=(t // ROUTER_TM,),
        in_specs=[pl.BlockSpec((ROUTER_TM, D_MODEL), lambda i: (i, 0)),
                  _const_spec(wr_t.shape), _const_spec(br.shape)],
        out_specs=pl.BlockSpec((N_EXPERTS, ROUTER_TM), lambda i: (0, i)),
        compiler_params=_params("arbitrary"),
        name="moe_router",
    )(x, wr_t, br)


MOE_TM = 1024
MOE_EXPERTS_PER_STEP = 2


def _moe_kernel(x_ref, gate_ref, w1_ref, w3_ref, w2_ref, lng_ref, lnb_ref, o_ref,
                xb_ref, acc_ref):
    k = pl.program_id(1)

    @pl.when(k == 0)
    def _():
        xb_ref[...] = x_ref[...].astype(BF16)
        acc_ref[...] = jnp.zeros_like(acc_ref)

    xb = xb_ref[...]
    gates = gate_ref[...]
    y = None
    for e in range(MOE_EXPERTS_PER_STEP):
        h1 = jnp.dot(xb, w1_ref[e], preferred_element_type=F32)
        h3 = jnp.dot(xb, w3_ref[e], preferred_element_type=F32)
        h = (jax.nn.silu(h1) * h3 * gates[:, e:e + 1]).astype(BF16)
        ye = jnp.dot(h, w2_ref[e], preferred_element_type=F32)
        y = ye if y is None else y + ye
    acc_ref[...] += y

    @pl.when(k == pl.num_programs(1) - 1)
    def _():
        o_ref[...] = _layer_norm(DEEPNORM_ALPHA * x_ref[...] + acc_ref[...],
                                 lng_ref[...], lnb_ref[...])


def _moe_layer(x, gates, w1, w3, w2, ln_g, ln_b):
    t = x.shape[0]
    n_steps = N_EXPERTS // MOE_EXPERTS_PER_STEP
    row_spec = pl.BlockSpec((MOE_TM, D_MODEL), lambda i, k: (i, 0))
    return pl.pallas_call(
        _moe_kernel,
        out_shape=jax.ShapeDtypeStruct((t, D_MODEL), F32),
        grid=(t // MOE_TM, n_steps),
        in_specs=[row_spec,
                  pl.BlockSpec((None, MOE_TM, MOE_EXPERTS_PER_STEP), lambda i, k: (k, i, 0)),
                  pl.BlockSpec((MOE_EXPERTS_PER_STEP, D_MODEL, D_EXPERT), lambda i, k: (k, 0, 0)),
                  pl.BlockSpec((MOE_EXPERTS_PER_STEP, D_MODEL, D_EXPERT), lambda i, k: (k, 0, 0)),
                  pl.BlockSpec((MOE_EXPERTS_PER_STEP, D_EXPERT, D_MODEL), lambda i, k: (k, 0, 0)),
                  pl.BlockSpec((1, D_MODEL), lambda i, k: (0, 0)),
                  pl.BlockSpec((1, D_MODEL), lambda i, k: (0, 0))],
        out_specs=row_spec,
        scratch_shapes=[pltpu.VMEM((MOE_TM, D_MODEL), BF16),
                        pltpu.VMEM((MOE_TM, D_MODEL), F32)],
        compiler_params=_params("arbitrary", "arbitrary"),
        name="moe_layer",
    )(x, gates, w1, w3, w2, ln_g, ln_b)


PROJ_TM = 512


def _proj_kernel(x_ref, w_ref, o_ref, *, scale):
    acc = jnp.dot(x_ref[...].astype(BF16), w_ref[...], preferred_element_type=F32)
    o_ref[...] = (acc * scale).astype(o_ref.dtype)


def _project(x, w, scale):
    t, n = x.shape[0], w.shape[1]
    return pl.pallas_call(
        functools.partial(_proj_kernel, scale=scale),
        out_shape=jax.ShapeDtypeStruct((t, n), BF16),
        grid=(t // PROJ_TM,),
        in_specs=[pl.BlockSpec((PROJ_TM, D_MODEL), lambda i: (i, 0)), _const_spec(w.shape)],
        out_specs=pl.BlockSpec((PROJ_TM, n), lambda i: (i, 0)),
        compiler_params=_params("arbitrary"),
        name="project",
    )(x, w)


def _kv_kernel(x_ref, w_ref, kv_ref, kmean_ref):
    acc = jnp.dot(x_ref[...].astype(BF16), w_ref[...], preferred_element_type=F32)
    kv_ref[...] = acc.astype(BF16)
    k = acc[:, :D_MODEL]
    for j in range(PROJ_TM // MOBA_BLOCK):
        kmean_ref[j] = jnp.mean(k[j * MOBA_BLOCK:(j + 1) * MOBA_BLOCK], axis=0, keepdims=True)


def _project_kv(x, w_kv):
    t = x.shape[0]
    blocks_per_tile = PROJ_TM // MOBA_BLOCK
    return pl.pallas_call(
        _kv_kernel,
        out_shape=(jax.ShapeDtypeStruct((t, 2 * D_MODEL), BF16),
                   jax.ShapeDtypeStruct((t // MOBA_BLOCK, 1, D_MODEL), F32)),
        grid=(t // PROJ_TM,),
        in_specs=[pl.BlockSpec((PROJ_TM, D_MODEL), lambda i: (i, 0)), _const_spec(w_kv.shape)],
        out_specs=(pl.BlockSpec((PROJ_TM, 2 * D_MODEL), lambda i: (i, 0)),
                   pl.BlockSpec((blocks_per_tile, 1, D_MODEL), lambda i: (i, 0, 0))),
        compiler_params=_params("arbitrary"),
        name="project_kv",
    )(x, w_kv)


def _t5_bucket_np(dist):
    n = np.maximum(dist, 0)
    max_exact = N_BUCKETS // 2
    nf = np.maximum(n, 1).astype(np.float32)
    large = max_exact + (np.log(nf / np.float32(max_exact)) / np.float32(math.log(MAX_DISTANCE / max_exact))
                         * np.float32(N_BUCKETS - max_exact)).astype(np.int32)
    large = np.minimum(large, N_BUCKETS - 1)
    return np.where(n < max_exact, n, large).astype(np.int32)


def _bucket_tables():
    kpos = np.arange(MOBA_BLOCK)[:, None]
    qpos = np.arange(MOBA_BLOCK)[None, :]
    own = np.where(qpos >= kpos, _t5_bucket_np(qpos - kpos), -1)
    prev = _t5_bucket_np(qpos + MOBA_BLOCK - kpos)
    return np.stack([own, prev]).astype(np.int32)


def _bias_kernel(rel_ref, bucket_ref, o_ref):
    h = pl.program_id(0)
    far = rel_ref[N_BUCKETS - 1, h]
    for t in range(2):
        bucket = bucket_ref[t]
        acc = jnp.zeros(bucket.shape, F32)
        for b in range(N_BUCKETS):
            acc = jnp.where(bucket == b, rel_ref[b, h], acc)
        o_ref[t] = jnp.where(bucket < 0, NEG_INF, acc - far)


def _bias_tables(rel_bias):
    buckets = jnp.asarray(_bucket_tables())
    return pl.pallas_call(
        _bias_kernel,
        out_shape=jax.ShapeDtypeStruct((N_HEADS, 2, MOBA_BLOCK, MOBA_BLOCK), F32),
        grid=(N_HEADS,),
        in_specs=[pl.BlockSpec(memory_space=pltpu.SMEM), _const_spec(buckets.shape)],
        out_specs=pl.BlockSpec((None, 2, MOBA_BLOCK, MOBA_BLOCK), lambda h: (h, 0, 0, 0)),
        compiler_params=_params("arbitrary"),
        name="rel_bias_tables",
    )(rel_bias, buckets)


def _attn_kernel(qt_ref, k_ref, vt_ref, kmean_ref, bias_ref, o_ref, sel_ref):
    n_blocks = kmean_ref.shape[0]
    kmean = kmean_ref[...].astype(BF16)
    blk = lax.broadcasted_iota(jnp.int32, (n_blocks, MOBA_BLOCK), 0)

    def tile(j, qt, bias, sel, carry):
        m, l, acc = carry
        kj = k_ref[pl.ds(pl.multiple_of(j * MOBA_BLOCK, MOBA_BLOCK), MOBA_BLOCK), :]
        vtj = vt_ref[:, pl.ds(pl.multiple_of(j * MOBA_BLOCK, MOBA_BLOCK), MOBA_BLOCK)]
        s = jnp.dot(kj, qt, preferred_element_type=F32)
        if bias is not None:
            s = s + bias
        mj = jnp.max(s, axis=0, keepdims=True)
        if sel is not None:
            mj = jnp.where(sel, mj, NEG_INF)
        m_new = jnp.maximum(m, mj)
        shift = m_new if sel is None else jnp.where(sel, m_new, -NEG_INF)
        p = jnp.exp(s - shift)
        alpha = jnp.exp(m - m_new)
        l = alpha * l + jnp.sum(p, axis=0, keepdims=True)
        acc = alpha * acc + jnp.dot(vtj, p.astype(BF16), preferred_element_type=F32)
        return m_new, l, acc

    def q_block(i, _):
        q0 = pl.multiple_of(i * MOBA_BLOCK, MOBA_BLOCK)
        qt = qt_ref[:, pl.ds(q0, MOBA_BLOCK)]
        gate = jnp.dot(kmean, qt, preferred_element_type=F32)
        gate = jnp.where(blk < i, gate, NEG_INF)
        sel = jnp.zeros(gate.shape, F32)
        for r in range(MOBA_TOPK):
            top = jnp.max(gate, axis=0, keepdims=True)
            first = jnp.min(jnp.where(gate == top, blk, n_blocks), axis=0, keepdims=True)
            hit = blk == first
            sel = jnp.where(hit & (r < i), 1.0, sel)
            gate = jnp.where(hit, -jnp.inf, gate)
        sel_ref[...] = sel

        carry = (jnp.full((1, MOBA_BLOCK), NEG_INF, F32),
                 jnp.zeros((1, MOBA_BLOCK), F32),
                 jnp.zeros((HEAD_DIM, MOBA_BLOCK), F32))
        carry = tile(i, qt, bias_ref[0], None, carry)
        jp = jnp.maximum(i - 1, 0)
        carry = tile(jp, qt, bias_ref[1], sel_ref[pl.ds(jp, 1), :] > 0.5, carry)
        carry = lax.fori_loop(
            0, jnp.maximum(i - 1, 0),
            lambda j, c: tile(j, qt, None, sel_ref[pl.ds(j, 1), :] > 0.5, c), carry)
        _, l, acc = carry
        o_ref[:, pl.ds(q0, MOBA_BLOCK)] = (acc / l).astype(o_ref.dtype)
        return 0

    lax.fori_loop(0, n_blocks, q_block, 0)


def _moba_attention(qt, k, vt, kmean, bias):
    b, h, _, s = qt.shape
    n_blocks = s // MOBA_BLOCK
    t_spec = pl.BlockSpec((None, None, HEAD_DIM, s), lambda bi, hi: (bi, hi, 0, 0))
    return pl.pallas_call(
        _attn_kernel,
        out_shape=jax.ShapeDtypeStruct((b, h, HEAD_DIM, s), BF16),
        grid=(b, h),
        in_specs=[t_spec,
                  pl.BlockSpec((None, None, s, HEAD_DIM), lambda bi, hi: (bi, hi, 0, 0)),
                  t_spec,
                  pl.BlockSpec((None, None, n_blocks, HEAD_DIM), lambda bi, hi: (bi, hi, 0, 0)),
                  pl.BlockSpec((None, 2, MOBA_BLOCK, MOBA_BLOCK), lambda bi, hi: (hi, 0, 0, 0))],
        out_specs=t_spec,
        scratch_shapes=[pltpu.VMEM((n_blocks, MOBA_BLOCK), F32)],
        compiler_params=_params("arbitrary", "arbitrary"),
        name="moba_attention",
    )(qt, k, vt, kmean, bias)


OPROJ_TM = 512


def _oproj_kernel(a_ref, w_ref, x_ref, lng_ref, lnb_ref, o_ref):
    h = jnp.dot(a_ref[...], w_ref[...], preferred_element_type=F32)
    o_ref[...] = _layer_norm(DEEPNORM_ALPHA * x_ref[...] + h, lng_ref[...], lnb_ref[...])


def _out_project(a, w_o, x, ln_g, ln_b):
    t = x.shape[0]
    row_spec = pl.BlockSpec((OPROJ_TM, D_MODEL), lambda i: (i, 0))
    return pl.pallas_call(
        _oproj_kernel,
        out_shape=jax.ShapeDtypeStruct((t, D_MODEL), F32),
        grid=(t // OPROJ_TM,),
        in_specs=[row_spec, _const_spec(w_o.shape), row_spec,
                  _const_spec(ln_g.shape), _const_spec(ln_b.shape)],
        out_specs=row_spec,
        compiler_params=_params("arbitrary"),
        name="out_project",
    )(a, w_o, x, ln_g, ln_b)


def kernel(x, ln_g, ln_b, a_w_in, a_v_g, a_v_b, a_w_s, a_b_s, a_w_out, w_k, w_v, b_w_q, b_w_o,
           rel_bias, w_router, b_router, e_w1, e_w3, e_w2):
    batch, seq, _ = x.shape
    t = batch * seq
    n_blocks = seq // MOBA_BLOCK
    xt = x.reshape(t, D_MODEL)
    wr_t = w_router.T.astype(BF16)
    br = b_router.reshape(N_EXPERTS, 1)
    n_steps = N_EXPERTS // MOE_EXPERTS_PER_STEP

    def ln_rows(l, j):
        return ln_g[l, j].reshape(1, D_MODEL), ln_b[l, j].reshape(1, D_MODEL)

    k_heads = vt_heads = kmean = bias = None
    for l in range(DEPTH):
        g0, b0 = ln_rows(l, 0)
        if l < N_A_LAYERS:
            xt = _gmlp_layer(xt, a_w_in[l].astype(BF16),
                             a_v_g[l].reshape(1, GMLP_HALF), a_v_b[l].reshape(1, GMLP_HALF),
                             a_w_s[l], a_b_s[l].T, a_w_out[l].astype(BF16), g0, b0)
        else:
            if l == N_A_LAYERS:
                w_kv = jnp.concatenate([w_k, w_v], axis=1).astype(BF16)
                kv, kmean = _project_kv(xt, w_kv)
                kv = kv.reshape(batch, seq, 2, N_HEADS, HEAD_DIM)
                k_heads = kv[:, :, 0].transpose(0, 2, 1, 3)
                vt_heads = kv[:, :, 1].transpose(0, 2, 3, 1)
                kmean = kmean.reshape(batch, n_blocks, N_HEADS, HEAD_DIM).transpose(0, 2, 1, 3)
                bias = _bias_tables(rel_bias)
            j = l - N_A_LAYERS
            q = _project(xt, b_w_q[j].astype(BF16), HEAD_DIM ** -0.5)
            qt = q.reshape(batch, seq, N_HEADS, HEAD_DIM).transpose(0, 2, 3, 1)
            ot = _moba_attention(qt, k_heads, vt_heads, kmean, bias)
            a = ot.transpose(0, 3, 1, 2).reshape(t, D_MODEL)
            xt = _out_project(a, b_w_o[j].astype(BF16), xt, g0, b0)
        gates_t = _router(xt, wr_t, br)
        gates = gates_t.reshape(n_steps, MOE_EXPERTS_PER_STEP, t).transpose(0, 2, 1)
        g1, b1 = ln_rows(l, 1)
        xt = _moe_layer(xt, gates, e_w1[l].astype(BF16), e_w3[l].astype(BF16),
                        e_w2[l].astype(BF16), g1, b1)
    return xt.reshape(batch, seq, D_MODEL)
```

```python
import functools
import math

import numpy as np
import jax
import jax.numpy as jnp
from jax import lax
from jax.experimental import pallas as pl
from jax.experimental.pallas import tpu as pltpu

D_MODEL = 1024
DEPTH = 4
N_A_LAYERS = DEPTH // 2
GMLP_HALF = 3 * D_MODEL
GMLP_GROUPS = 8
GMLP_GROUP_WIDTH = GMLP_HALF // GMLP_GROUPS
GMLP_CHUNK = 128
N_HEADS = 16
HEAD_DIM = D_MODEL // N_HEADS
MOBA_BLOCK = 256
MOBA_TOPK = 3
N_BUCKETS = 32
MAX_DISTANCE = 128
N_EXPERTS = 16
N_GROUPS = 4
EXPERTS_PER_GROUP = N_EXPERTS // N_GROUPS
D_EXPERT = D_MODEL // 4
DEEPNORM_ALPHA = (2 * DEPTH) ** 0.25
LN_EPS = 1e-5
NEG_INF = -1e30

V7X_VMEM_LIMIT_BYTES = 56 * 1024 * 1024
V7X_LANES = 128

BF16 = jnp.bfloat16
F32 = jnp.float32

_NT = (((1,), (1,)), ((), ()))
_TN = (((0,), (0,)), ((), ()))


def _params(*semantics):
    return pltpu.CompilerParams(dimension_semantics=semantics,
                                vmem_limit_bytes=V7X_VMEM_LIMIT_BYTES)


def _const_spec(shape):
    zeros = (0,) * len(shape)
    return pl.BlockSpec(shape, lambda *_: zeros, pipeline_mode=pl.Buffered(1))


def _layer_norm(y, g, b):
    mu = jnp.mean(y, axis=-1, keepdims=True)
    d = y - mu
    var = jnp.mean(d * d, axis=-1, keepdims=True)
    return d * lax.rsqrt(var + LN_EPS) * g + b


GMLP_TM = 256


def _gmlp_kernel(x_ref, w_in_ref, vg_ref, vb_ref, ws_ref, bs_ref, w_out_ref,
                 lng_ref, lnb_ref, o_ref, vn_ref):
    x = x_ref[...]
    xb = x.astype(BF16)
    v = jax.nn.gelu(jnp.dot(xb, w_in_ref[:, GMLP_HALF:], preferred_element_type=F32))
    vn_ref[...] = _layer_norm(v, vg_ref[...], vb_ref[...]).astype(BF16)

    row = lax.broadcasted_iota(jnp.int32, (GMLP_CHUNK, GMLP_CHUNK), 0)
    col = lax.broadcasted_iota(jnp.int32, (GMLP_CHUNK, GMLP_CHUNK), 1)
    causal = row >= col
    acc = jnp.zeros((GMLP_TM, D_MODEL), F32)
    for g in range(GMLP_GROUPS):
        lo, hi = g * GMLP_GROUP_WIDTH, (g + 1) * GMLP_GROUP_WIDTH
        u = jax.nn.gelu(jnp.dot(xb, w_in_ref[:, lo:hi], preferred_element_type=F32))
        w_s = jnp.where(causal, ws_ref[g], 0.0).astype(BF16)
        b_s = bs_ref[:, g:g + 1]
        parts = []
        for c in range(GMLP_TM // GMLP_CHUNK):
            vc = vn_ref[c * GMLP_CHUNK:(c + 1) * GMLP_CHUNK, lo:hi]
            parts.append(jnp.dot(w_s, vc, preferred_element_type=F32) + b_s)
        sv = jnp.concatenate(parts, axis=0)
        gated = (u * sv).astype(BF16)
        acc = acc + jnp.dot(gated, w_out_ref[lo:hi, :], preferred_element_type=F32)
    o_ref[...] = _layer_norm(DEEPNORM_ALPHA * x + acc, lng_ref[...], lnb_ref[...])


def _gmlp_layer(x, w_in, v_g, v_b, w_s, b_s_t, w_out, ln_g, ln_b):
    t = x.shape[0]
    row_spec = pl.BlockSpec((GMLP_TM, D_MODEL), lambda i: (i, 0))
    return pl.pallas_call(
        _gmlp_kernel,
        out_shape=jax.ShapeDtypeStruct((t, D_MODEL), F32),
        grid=(t // GMLP_TM,),
        in_specs=[row_spec,
                  _const_spec(w_in.shape), _const_spec(v_g.shape), _const_spec(v_b.shape),
                  _const_spec(w_s.shape), _const_spec(b_s_t.shape), _const_spec(w_out.shape),
                  _const_spec(ln_g.shape), _const_spec(ln_b.shape)],
        out_specs=row_spec,
        scratch_shapes=[pltpu.VMEM((GMLP_TM, GMLP_HALF), BF16)],
        compiler_params=_params("arbitrary"),
        name="gmlp_layer",
    )(x, w_in, v_g, v_b, w_s, b_s_t, w_out, ln_g, ln_b)


ROUTER_TM = 2048


def _router_kernel(x_ref, wr_ref, br_ref, o_ref):
    xb = x_ref[...].astype(BF16)
    logits = lax.dot_general(wr_ref[...], xb, _NT, preferred_element_type=F32) + br_ref[...]
    m = jnp.max(logits, axis=0, keepdims=True)
    ex = jnp.exp(logits - m)
    probs = ex / jnp.sum(ex, axis=0, keepdims=True)
    p = [probs[e:e + 1, :] for e in range(N_EXPERTS)]

    scores = []
    for g in range(N_GROUPS):
        q = p[g * EXPERTS_PER_GROUP:(g + 1) * EXPERTS_PER_GROUP]
        best = None
        for a in range(EXPERTS_PER_GROUP):
            for b in range(a + 1, EXPERTS_PER_GROUP):
                s = q[a] + q[b]
                best = s if best is None else jnp.maximum(best, s)
        scores.append(best)
    best_score = scores[0]
    gsel = jnp.zeros_like(best_score, dtype=jnp.int32)
    for g in range(1, N_GROUPS):
        better = scores[g] > best_score
        best_score = jnp.where(better, scores[g], best_score)
        gsel = jnp.where(better, g, gsel)

    rows = []
    for e in range(N_EXPERTS):
        g = e // EXPERTS_PER_GROUP
        rank = jnp.zeros_like(gsel)
        for e2 in range(g * EXPERTS_PER_GROUP, (g + 1) * EXPERTS_PER_GROUP):
            if e2 == e:
                continue
            ahead = (p[e2] > p[e]) | ((p[e2] == p[e]) & (e2 < e))
            rank = rank + ahead.astype(jnp.int32)
        chosen = (gsel == g) & (rank < 2)
        rows.append(jnp.where(chosen, p[e] / best_score, 0.0))
    o_ref[...] = jnp.concatenate(rows, axis=0)


def _router(x, wr_t, br):
    t = x.shape[0]
    return pl.pallas_call(
        _router_kernel,
        out_shape=jax.ShapeDtypeStruct((N_EXPERTS, t), F32),
        grid=(t // ROUTER_TM,),
        in_specs=[pl.BlockSpec((ROUTER_TM, D_MODEL), lambda i: (i, 0)),
                  _const_spec(wr_t.shape), _const_spec(br.shape)],
        out_specs=pl.BlockSpec((N_EXPERTS, ROUTER_TM), lambda i: (0, i)),
        compiler_params=_params("arbitrary"),
        name="moe_router",
    )(x, wr_t, br)


MOE_TM = 1024
MOE_EXPERTS_PER_STEP = 2


def _moe_kernel(x_ref, gate_ref, w1_ref, w3_ref, w2_ref, lng_ref, lnb_ref, o_ref,
                xb_ref, acc_ref):
    k = pl.program_id(1)

    @pl.when(k == 0)
    def _():
        xb_ref[...] = x_ref[...].astype(BF16)
        acc_ref[...] = jnp.zeros_like(acc_ref)

    xb = xb_ref[...]
    gates = gate_ref[...]
    y = None
    for e in range(MOE_EXPERTS_PER_STEP):
        h1 = jnp.dot(xb, w1_ref[e], preferred_element_type=F32)
        h3 = jnp.dot(xb, w3_ref[e], preferred_element_type=F32)
        h = (jax.nn.silu(h1) * h3 * gates[:, e:e + 1]).astype(BF16)
        ye = jnp.dot(h, w2_ref[e], preferred_element_type=F32)
        y = ye if y is None else y + ye
    acc_ref[...] += y

    @pl.when(k == pl.num_programs(1) - 1)
    def _():
        o_ref[...] = _layer_norm(DEEPNORM_ALPHA * x_ref[...] + acc_ref[...],
                                 lng_ref[...], lnb_ref[...])


def _moe_layer(x, gates, w1, w3, w2, ln_g, ln_b):
    t = x.shape[0]
    n_steps = N_EXPERTS // MOE_EXPERTS_PER_STEP
    row_spec = pl.BlockSpec((MOE_TM, D_MODEL), lambda i, k: (i, 0))
    return pl.pallas_call(
        _moe_kernel,
        out_shape=jax.ShapeDtypeStruct((t, D_MODEL), F32),
        grid=(t // MOE_TM, n_steps),
        in_specs=[row_spec,
                  pl.BlockSpec((None, MOE_TM, MOE_EXPERTS_PER_STEP), lambda i, k: (k, i, 0)),
                  pl.BlockSpec((MOE_EXPERTS_PER_STEP, D_MODEL, D_EXPERT), lambda i, k: (k, 0, 0)),
                  pl.BlockSpec((MOE_EXPERTS_PER_STEP, D_MODEL, D_EXPERT), lambda i, k: (k, 0, 0)),
                  pl.BlockSpec((MOE_EXPERTS_PER_STEP, D_EXPERT, D_MODEL), lambda i, k: (k, 0, 0)),
                  pl.BlockSpec((1, D_MODEL), lambda i, k: (0, 0)),
                  pl.BlockSpec((1, D_MODEL), lambda i, k: (0, 0))],
        out_specs=row_spec,
        scratch_shapes=[pltpu.VMEM((MOE_TM, D_MODEL), BF16),
                        pltpu.VMEM((MOE_TM, D_MODEL), F32)],
        compiler_params=_params("arbitrary", "arbitrary"),
        name="moe_layer",
    )(x, gates, w1, w3, w2, ln_g, ln_b)


PROJ_TM = 512


def _proj_t_kernel(x_ref, wt_ref, o_ref, *, scale):
    acc = lax.dot_general(wt_ref[...], x_ref[...].astype(BF16), _NT, preferred_element_type=F32)
    o_ref[...] = (acc * scale).astype(o_ref.dtype)


def _project_t(x, w_t, scale):
    b, s, _ = x.shape
    n = w_t.shape[0]
    return pl.pallas_call(
        functools.partial(_proj_t_kernel, scale=scale),
        out_shape=jax.ShapeDtypeStruct((b, n, s), BF16),
        grid=(b, s // PROJ_TM),
        in_specs=[pl.BlockSpec((None, PROJ_TM, D_MODEL), lambda bi, i: (bi, i, 0)),
                  _const_spec(w_t.shape)],
        out_specs=pl.BlockSpec((None, n, PROJ_TM), lambda bi, i: (bi, 0, i)),
        compiler_params=_params("arbitrary", "arbitrary"),
        name="project_t",
    )(x, w_t)


def _kv_kernel(x_ref, wk_ref, wvt_ref, k_ref, vt_ref, kmean_ref):
    xb = x_ref[...].astype(BF16)
    k = jnp.dot(xb, wk_ref[...], preferred_element_type=F32)
    k_ref[...] = k.astype(BF16)
    for j in range(PROJ_TM // MOBA_BLOCK):
        kmean_ref[j] = jnp.mean(k[j * MOBA_BLOCK:(j + 1) * MOBA_BLOCK], axis=0, keepdims=True)
    vt_ref[...] = lax.dot_general(wvt_ref[...], xb, _NT, preferred_element_type=F32).astype(BF16)


def _project_kv(x, w_k, w_v_t):
    b, s, _ = x.shape
    blocks_per_tile = PROJ_TM // MOBA_BLOCK
    tiles = s // PROJ_TM
    return pl.pallas_call(
        _kv_kernel,
        out_shape=(jax.ShapeDtypeStruct((b, s, D_MODEL), BF16),
                   jax.ShapeDtypeStruct((b, D_MODEL, s), BF16),
                   jax.ShapeDtypeStruct((b * s // MOBA_BLOCK, 1, D_MODEL), F32)),
        grid=(b, tiles),
        in_specs=[pl.BlockSpec((None, PROJ_TM, D_MODEL), lambda bi, i: (bi, i, 0)),
                  _const_spec(w_k.shape), _const_spec(w_v_t.shape)],
        out_specs=(pl.BlockSpec((None, PROJ_TM, D_MODEL), lambda bi, i: (bi, i, 0)),
                   pl.BlockSpec((None, D_MODEL, PROJ_TM), lambda bi, i: (bi, 0, i)),
                   pl.BlockSpec((blocks_per_tile, 1, D_MODEL), lambda bi, i: (bi * tiles + i, 0, 0))),
        compiler_params=_params("arbitrary", "arbitrary"),
        name="project_kv",
    )(x, w_k, w_v_t)


def _t5_bucket_np(dist):
    n = np.maximum(dist, 0)
    max_exact = N_BUCKETS // 2
    nf = np.maximum(n, 1).astype(np.float32)
    large = max_exact + (np.log(nf / np.float32(max_exact)) / np.float32(math.log(MAX_DISTANCE / max_exact))
                         * np.float32(N_BUCKETS - max_exact)).astype(np.int32)
    large = np.minimum(large, N_BUCKETS - 1)
    return np.where(n < max_exact, n, large).astype(np.int32)


def _bucket_tables():
    kpos = np.arange(MOBA_BLOCK)[:, None]
    qpos = np.arange(MOBA_BLOCK)[None, :]
    own = np.where(qpos >= kpos, _t5_bucket_np(qpos - kpos), -1)
    prev = _t5_bucket_np(qpos + MOBA_BLOCK - kpos)
    return np.stack([own, prev]).astype(np.int32)


def _bias_kernel(rel_ref, bucket_ref, o_ref):
    h = pl.program_id(0)
    far = rel_ref[N_BUCKETS - 1, h]
    for t in range(2):
        bucket = bucket_ref[t]
        acc = jnp.zeros(bucket.shape, F32)
        for b in range(N_BUCKETS):
            acc = jnp.where(bucket == b, rel_ref[b, h], acc)
        o_ref[t] = jnp.where(bucket < 0, NEG_INF, acc - far)


def _bias_tables(rel_bias):
    buckets = jnp.asarray(_bucket_tables())
    return pl.pallas_call(
        _bias_kernel,
        out_shape=jax.ShapeDtypeStruct((N_HEADS, 2, MOBA_BLOCK, MOBA_BLOCK), F32),
        grid=(N_HEADS,),
        in_specs=[pl.BlockSpec(memory_space=pltpu.SMEM), _const_spec(buckets.shape)],
        out_specs=pl.BlockSpec((None, 2, MOBA_BLOCK, MOBA_BLOCK), lambda h: (h, 0, 0, 0)),
        compiler_params=_params("arbitrary"),
        name="rel_bias_tables",
    )(rel_bias, buckets)


ATTN_HEADS = 4
ATTN_WIDTH = ATTN_HEADS * HEAD_DIM
HEADS_PER_LANE_TILE = V7X_LANES // HEAD_DIM
FAR_TILES_PER_STEP = 2
QK_LOOKAHEAD = 2


def _attn_kernel(qt_ref, k_ref, vt_ref, kmean_ref, bias_ref, o_ref, sel_ref, ml_ref, acc_ref):
    n_blocks = kmean_ref.shape[0]
    blk = lax.broadcasted_iota(jnp.int32, (n_blocks, MOBA_BLOCK), 0)
    causal = (lax.broadcasted_iota(jnp.int32, (MOBA_BLOCK, MOBA_BLOCK), 0)
              <= lax.broadcasted_iota(jnp.int32, (MOBA_BLOCK, MOBA_BLOCK), 1))

    def lane_slab(h):
        return slice((h // HEADS_PER_LANE_TILE) * V7X_LANES, (h // HEADS_PER_LANE_TILE + 1) * V7X_LANES)

    def padded_q(h, q0):
        qh = qt_ref[h * HEAD_DIM:(h + 1) * HEAD_DIM, pl.ds(q0, MOBA_BLOCK)]
        pieces = [jnp.zeros_like(qh)] * HEADS_PER_LANE_TILE
        pieces[h % HEADS_PER_LANE_TILE] = qh
        return jnp.concatenate(pieces, axis=0)

    def scores(h, j, qp):
        k0 = pl.multiple_of(j * MOBA_BLOCK, MOBA_BLOCK)
        kj = k_ref[pl.ds(k0, MOBA_BLOCK), lane_slab(h)]
        return jnp.dot(kj, qp, preferred_element_type=F32)

    def weights_and_values(h, j, s, bias, sel):
        if bias is not None:
            keep = causal if sel is None else sel
            s = jnp.where(keep, s, NEG_INF) + bias[...]
        m_loc = jnp.max(s, axis=0, keepdims=True)
        if sel is None:
            shift = m_loc
        else:
            shift = jnp.where(sel, m_loc, -NEG_INF)
            m_loc = jnp.where(sel, m_loc, NEG_INF)
        p = jnp.exp(s - shift)
        l_loc = jnp.sum(p, axis=0, keepdims=True)
        k0 = pl.multiple_of(j * MOBA_BLOCK, MOBA_BLOCK)
        vtj = vt_ref[h * HEAD_DIM:(h + 1) * HEAD_DIM, pl.ds(k0, MOBA_BLOCK)]
        pv = jnp.dot(vtj, p.astype(BF16), preferred_element_type=F32)
        return m_loc, l_loc, pv

    def run_tiles(tiles):
        n = len(tiles)
        s = [None] * n
        out = [None] * n
        for t in range(n + QK_LOOKAHEAD):
            if t < n:
                h, j, qp, _, _ = tiles[t]
                s[t] = scores(h, j, qp)
            d = t - QK_LOOKAHEAD
            if d >= 0:
                h, j, _, bias, sel = tiles[d]
                out[d] = weights_and_values(h, j, s[d], bias, sel)
                s[d] = None
        return out

    def merge(h, parts, first):
        m_new = parts[0][0]
        for m_loc, _, _ in parts[1:]:
            m_new = jnp.maximum(m_new, m_loc)
        if first:
            l_new = jnp.zeros_like(m_new)
            acc_new = jnp.zeros((HEAD_DIM, MOBA_BLOCK), F32)
        else:
            m_old = ml_ref[h, 0:1, :]
            m_new = jnp.maximum(m_new, m_old)
            alpha = jnp.exp(m_old - m_new)
            l_new = alpha * ml_ref[h, 1:2, :]
            acc_new = alpha * acc_ref[h]
        for m_loc, l_loc, pv in parts:
            w = jnp.exp(m_loc - m_new)
            l_new = l_new + w * l_loc
            acc_new = acc_new + w * pv
        ml_ref[h, 0:1, :] = m_new
        ml_ref[h, 1:2, :] = l_new
        acc_ref[h] = acc_new

    def q_block(i, _):
        q0 = pl.multiple_of(i * MOBA_BLOCK, MOBA_BLOCK)
        jp = jnp.maximum(i - 1, 0)
        n_far = jnp.maximum(i - 1, 0)
        qps = [padded_q(h, q0) for h in range(ATTN_HEADS)]

        for h in range(ATTN_HEADS):
            gate = jnp.dot(kmean_ref[:, lane_slab(h)].astype(BF16), qps[h], preferred_element_type=F32)
            gate = jnp.where(blk < i, gate, NEG_INF)
            sel = jnp.zeros(gate.shape, F32)
            for r in range(MOBA_TOPK):
                top = jnp.max(gate, axis=0, keepdims=True)
                first = jnp.min(jnp.where(gate == top, blk, n_blocks), axis=0, keepdims=True)
                hit = blk == first
                sel = jnp.where(hit & (r < i), 1.0, sel)
                gate = jnp.where(hit, -jnp.inf, gate)
            sel_ref[h] = sel

        tiles = []
        for h in range(ATTN_HEADS):
            tiles.append((h, i, qps[h], bias_ref.at[h, 0], None))
            tiles.append((h, jp, qps[h], bias_ref.at[h, 1], sel_ref[h, pl.ds(jp, 1), :] > 0.5))
        parts = run_tiles(tiles)
        for h in range(ATTN_HEADS):
            merge(h, parts[2 * h:2 * h + 2], first=True)

        def far_step(t, _):
            tiles = []
            for h in range(ATTN_HEADS):
                for u in range(FAR_TILES_PER_STEP):
                    j = t * FAR_TILES_PER_STEP + u
                    jc = jnp.minimum(j, n_far - 1)
                    sel = (sel_ref[h, pl.ds(jc, 1), :] > 0.5) & (j < n_far)
                    tiles.append((h, jc, qps[h], None, sel))
            parts = run_tiles(tiles)
            for h in range(ATTN_HEADS):
                merge(h, parts[h * FAR_TILES_PER_STEP:(h + 1) * FAR_TILES_PER_STEP], first=False)
            return 0

        lax.fori_loop(0, (n_far + FAR_TILES_PER_STEP - 1) // FAR_TILES_PER_STEP, far_step, 0)

        for h in range(ATTN_HEADS):
            o_ref[h * HEAD_DIM:(h + 1) * HEAD_DIM, pl.ds(q0, MOBA_BLOCK)] = (
                acc_ref[h] / ml_ref[h, 1:2, :]).astype(o_ref.dtype)
        return 0

    lax.fori_loop(0, n_blocks, q_block, 0)


def _moba_attention(qt, k, vt, kmean, bias):
    b, _, s = qt.shape
    n_blocks = s // MOBA_BLOCK
    t_spec = pl.BlockSpec((None, ATTN_WIDTH, s), lambda bi, hi: (bi, hi, 0))
    return pl.pallas_call(
        _attn_kernel,
        out_shape=jax.ShapeDtypeStruct((b, D_MODEL, s), BF16),
        grid=(b, N_HEADS // ATTN_HEADS),
        in_specs=[t_spec,
                  pl.BlockSpec((None, s, ATTN_WIDTH), lambda bi, hi: (bi, 0, hi)),
                  t_spec,
                  pl.BlockSpec((None, n_blocks, ATTN_WIDTH), lambda bi, hi: (bi, 0, hi)),
                  pl.BlockSpec((ATTN_HEADS, 2, MOBA_BLOCK, MOBA_BLOCK), lambda bi, hi: (hi, 0, 0, 0))],
        out_specs=t_spec,
        scratch_shapes=[pltpu.VMEM((ATTN_HEADS, n_blocks, MOBA_BLOCK), F32),
                        pltpu.VMEM((ATTN_HEADS, 2, MOBA_BLOCK), F32),
                        pltpu.VMEM((ATTN_HEADS, HEAD_DIM, MOBA_BLOCK), F32)],
        compiler_params=_params("arbitrary", "arbitrary"),
        name="moba_attention",
    )(qt, k, vt, kmean, bias)


OPROJ_TM = 512


def _oproj_kernel(at_ref, w_ref, x_ref, lng_ref, lnb_ref, o_ref):
    h = lax.dot_general(at_ref[...], w_ref[...], _TN, preferred_element_type=F32)
    o_ref[...] = _layer_norm(DEEPNORM_ALPHA * x_ref[...] + h, lng_ref[...], lnb_ref[...])


def _out_project(a_t, w_o, x, ln_g, ln_b):
    b, s, _ = x.shape
    row_spec = pl.BlockSpec((None, OPROJ_TM, D_MODEL), lambda bi, i: (bi, i, 0))
    return pl.pallas_call(
        _oproj_kernel,
        out_shape=jax.ShapeDtypeStruct((b, s, D_MODEL), F32),
        grid=(b, s // OPROJ_TM),
        in_specs=[pl.BlockSpec((None, D_MODEL, OPROJ_TM), lambda bi, i: (bi, 0, i)),
                  _const_spec(w_o.shape), row_spec,
                  _const_spec(ln_g.shape), _const_spec(ln_b.shape)],
        out_specs=row_spec,
        compiler_params=_params("arbitrary", "arbitrary"),
        name="out_project",
    )(a_t, w_o, x, ln_g, ln_b)


def kernel(x, ln_g, ln_b, a_w_in, a_v_g, a_v_b, a_w_s, a_b_s, a_w_out, w_k, w_v, b_w_q, b_w_o,
           rel_bias, w_router, b_router, e_w1, e_w3, e_w2):
    batch, seq, _ = x.shape
    t = batch * seq
    n_blocks = seq // MOBA_BLOCK
    xt = x.reshape(t, D_MODEL)
    wr_t = w_router.T.astype(BF16)
    br = b_router.reshape(N_EXPERTS, 1)
    n_steps = N_EXPERTS // MOE_EXPERTS_PER_STEP

    def ln_rows(l, j):
        return ln_g[l, j].reshape(1, D_MODEL), ln_b[l, j].reshape(1, D_MODEL)

    k = vt = kmean = bias = None
    for l in range(DEPTH):
        g0, b0 = ln_rows(l, 0)
        if l < N_A_LAYERS:
            xt = _gmlp_layer(xt, a_w_in[l].astype(BF16),
                             a_v_g[l].reshape(1, GMLP_HALF), a_v_b[l].reshape(1, GMLP_HALF),
                             a_w_s[l], a_b_s[l].T, a_w_out[l].astype(BF16), g0, b0)
        else:
            x3 = xt.reshape(batch, seq, D_MODEL)
            if l == N_A_LAYERS:
                k, vt, kmean = _project_kv(x3, w_k.astype(BF16), w_v.T.astype(BF16))
                kmean = kmean.reshape(batch, n_blocks, D_MODEL)
                bias = _bias_tables(rel_bias)
            j = l - N_A_LAYERS
            qt = _project_t(x3, b_w_q[j].T.astype(BF16), HEAD_DIM ** -0.5)
            ot = _moba_attention(qt, k, vt, kmean, bias)
            xt = _out_project(ot, b_w_o[j].astype(BF16), x3, g0, b0).reshape(t, D_MODEL)
        gates_t = _router(xt, wr_t, br)
        gates = gates_t.reshape(n_steps, MOE_EXPERTS_PER_STEP, t).transpose(0, 2, 1)
        g1, b1 = ln_rows(l, 1)
        xt = _moe_layer(xt, gates, e_w1[l].astype(BF16), e_w3[l].astype(BF16),
                        e_w2[l].astype(BF16), g1, b1)
    return xt.reshape(batch, seq, D_MODEL)
```

```python
import functools
import math

import numpy as np
import jax
import jax.numpy as jnp
from jax import lax
from jax.experimental import pallas as pl
from jax.experimental.pallas import tpu as pltpu

D_MODEL = 1024
DEPTH = 4
N_A_LAYERS = DEPTH // 2
GMLP_HALF = 3 * D_MODEL
GMLP_GROUPS = 8
GMLP_GROUP_WIDTH = GMLP_HALF // GMLP_GROUPS
GMLP_CHUNK = 128
N_HEADS = 16
HEAD_DIM = D_MODEL // N_HEADS
MOBA_BLOCK = 256
MOBA_TOPK = 3
N_BUCKETS = 32
MAX_DISTANCE = 128
N_EXPERTS = 16
N_GROUPS = 4
EXPERTS_PER_GROUP = N_EXPERTS // N_GROUPS
D_EXPERT = D_MODEL // 4
DEEPNORM_ALPHA = (2 * DEPTH) ** 0.25
LN_EPS = 1e-5
NEG_INF = -1e30

V7X_VMEM_LIMIT_BYTES = 56 * 1024 * 1024
V7X_LANES = 128
BF16_SUBLANE_TILE = 16

BF16 = jnp.bfloat16
F32 = jnp.float32

_NT = (((1,), (1,)), ((), ()))
_TN = (((0,), (0,)), ((), ()))


def _params(*semantics):
    return pltpu.CompilerParams(dimension_semantics=semantics,
                                vmem_limit_bytes=V7X_VMEM_LIMIT_BYTES)


def _const_spec(shape):
    zeros = (0,) * len(shape)
    return pl.BlockSpec(shape, lambda *_: zeros, pipeline_mode=pl.Buffered(1))


def _layer_norm(y, g, b):
    mu = jnp.mean(y, axis=-1, keepdims=True)
    d = y - mu
    var = jnp.mean(d * d, axis=-1, keepdims=True)
    return d * lax.rsqrt(var + LN_EPS) * g + b


GMLP_TM = 512
GMLP_GROUPS_PER_STEP = 2


def _gmlp_kernel(x_ref, w_in_ref, vg_ref, vb_ref, ws_ref, bs_ref, w_out_ref,
                 lng_ref, lnb_ref, o_ref, vn_ref, sv_ref):
    x = x_ref[...]
    xb = x.astype(BF16)
    v = jax.nn.gelu(jnp.dot(xb, w_in_ref[:, GMLP_HALF:], preferred_element_type=F32))
    vn_ref[...] = _layer_norm(v, vg_ref[...], vb_ref[...]).astype(BF16)

    row = lax.broadcasted_iota(jnp.int32, (GMLP_CHUNK, GMLP_CHUNK), 0)
    col = lax.broadcasted_iota(jnp.int32, (GMLP_CHUNK, GMLP_CHUNK), 1)
    causal = row >= col
    width = GMLP_GROUPS_PER_STEP * GMLP_GROUP_WIDTH
    acc = jnp.zeros((GMLP_TM, D_MODEL), F32)
    for step in range(GMLP_GROUPS // GMLP_GROUPS_PER_STEP):
        lo = step * width
        u = jax.nn.gelu(jnp.dot(xb, w_in_ref[:, lo:lo + width], preferred_element_type=F32))
        for gi in range(GMLP_GROUPS_PER_STEP):
            g = step * GMLP_GROUPS_PER_STEP + gi
            w_s = jnp.where(causal, ws_ref[g], 0.0).astype(BF16)
            b_s = bs_ref[:, g:g + 1]
            glo = g * GMLP_GROUP_WIDTH
            for c in range(GMLP_TM // GMLP_CHUNK):
                rows = slice(c * GMLP_CHUNK, (c + 1) * GMLP_CHUNK)
                vc = vn_ref[rows, glo:glo + GMLP_GROUP_WIDTH]
                sv_ref[rows, gi * GMLP_GROUP_WIDTH:(gi + 1) * GMLP_GROUP_WIDTH] = (
                    jnp.dot(w_s, vc, preferred_element_type=F32) + b_s)
        gated = (u * sv_ref[...]).astype(BF16)
        acc = acc + jnp.dot(gated, w_out_ref[lo:lo + width, :], preferred_element_type=F32)
    o_ref[...] = _layer_norm(DEEPNORM_ALPHA * x + acc, lng_ref[...], lnb_ref[...])


def _gmlp_layer(x, w_in, v_g, v_b, w_s, b_s_t, w_out, ln_g, ln_b):
    t = x.shape[0]
    row_spec = pl.BlockSpec((GMLP_TM, D_MODEL), lambda i: (i, 0))
    return pl.pallas_call(
        _gmlp_kernel,
        out_shape=jax.ShapeDtypeStruct((t, D_MODEL), F32),
        grid=(t // GMLP_TM,),
        in_specs=[row_spec,
                  _const_spec(w_in.shape), _const_spec(v_g.shape), _const_spec(v_b.shape),
                  _const_spec(w_s.shape), _const_spec(b_s_t.shape), _const_spec(w_out.shape),
                  _const_spec(ln_g.shape), _const_spec(ln_b.shape)],
        out_specs=row_spec,
        scratch_shapes=[pltpu.VMEM((GMLP_TM, GMLP_HALF), BF16),
                        pltpu.VMEM((GMLP_TM, GMLP_GROUPS_PER_STEP * GMLP_GROUP_WIDTH), F32)],
        compiler_params=_params("arbitrary"),
        name="gmlp_layer",
    )(x, w_in, v_g, v_b, w_s, b_s_t, w_out, ln_g, ln_b)


ROUTER_TM = 2048


def _router_kernel(x_ref, wr_ref, br_ref, o_ref):
    xb = x_ref[...].astype(BF16)
    logits = lax.dot_general(wr_ref[...], xb, _NT, preferred_element_type=F32) + br_ref[...]
    m = jnp.max(logits, axis=0, keepdims=True)
    ex = jnp.exp(logits - m)
    probs = ex / jnp.sum(ex, axis=0, keepdims=True)
    p = [probs[e:e + 1, :] for e in range(N_EXPERTS)]

    scores = []
    for g in range(N_GROUPS):
        q = p[g * EXPERTS_PER_GROUP:(g + 1) * EXPERTS_PER_GROUP]
        best = None
        for a in range(EXPERTS_PER_GROUP):
            for b in range(a + 1, EXPERTS_PER_GROUP):
                s = q[a] + q[b]
                best = s if best is None else jnp.maximum(best, s)
        scores.append(best)
    best_score = scores[0]
    gsel = jnp.zeros_like(best_score, dtype=jnp.int32)
    for g in range(1, N_GROUPS):
        better = scores[g] > best_score
        best_score = jnp.where(better, scores[g], best_score)
        gsel = jnp.where(better, g, gsel)

    rows = []
    for e in range(N_EXPERTS):
        g = e // EXPERTS_PER_GROUP
        rank = jnp.zeros_like(gsel)
        for e2 in range(g * EXPERTS_PER_GROUP, (g + 1) * EXPERTS_PER_GROUP):
            if e2 == e:
                continue
            ahead = (p[e2] > p[e]) | ((p[e2] == p[e]) & (e2 < e))
            rank = rank + ahead.astype(jnp.int32)
        chosen = (gsel == g) & (rank < 2)
        rows.append(jnp.where(chosen, p[e] / best_score, 0.0))
    o_ref[...] = jnp.concatenate(rows, axis=0)


def _router(x, wr_t, br):
    t = x.shape[0]
    return pl.pallas_call(
        _router_kernel,
        out_shape=jax.ShapeDtypeStruct((N_EXPERTS, t), F32),
        grid=(t // ROUTER_TM,),
        in_specs=[pl.BlockSpec((ROUTER_TM, D_MODEL), lambda i: (i, 0)),
                  _const_spec(wr_t.shape), _const_spec(br.shape)],
        out_specs=pl.BlockSpec((N_EXPERTS, ROUTER_TM), lambda i: (0, i)),
        compiler_params=_params("arbitrary"),
        name="moe_router",
    )(x, wr_t, br)


MOE_TM = 1024
MOE_EXPERTS_PER_STEP = 2


def _moe_kernel(x_ref, gate_ref, w1_ref, w3_ref, w2_ref, lng_ref, lnb_ref, o_ref,
                xb_ref, acc_ref):
    k = pl.program_id(1)

    @pl.when(k == 0)
    def _():
        xb_ref[...] = x_ref[...].astype(BF16)
        acc_ref[...] = jnp.zeros_like(acc_ref)

    xb = xb_ref[...]
    gates = gate_ref[...]
    y = None
    for e in range(MOE_EXPERTS_PER_STEP):
        h1 = jnp.dot(xb, w1_ref[e].astype(BF16), preferred_element_type=F32)
        h3 = jnp.dot(xb, w3_ref[e].astype(BF16), preferred_element_type=F32)
        h = (jax.nn.silu(h1) * h3 * gates[:, e:e + 1]).astype(BF16)
        ye = jnp.dot(h, w2_ref[e].astype(BF16), preferred_element_type=F32)
        y = ye if y is None else y + ye
    acc_ref[...] += y

    @pl.when(k == pl.num_programs(1) - 1)
    def _():
        o_ref[...] = _layer_norm(DEEPNORM_ALPHA * x_ref[...] + acc_ref[...],
                                 lng_ref[...], lnb_ref[...])


def _moe_layer(x, gates, w1, w3, w2, ln_g, ln_b, layer):
    t = x.shape[0]
    n_steps = N_EXPERTS // MOE_EXPERTS_PER_STEP
    row_spec = pl.BlockSpec((MOE_TM, D_MODEL), lambda i, k: (i, 0))
    return pl.pallas_call(
        _moe_kernel,
        out_shape=jax.ShapeDtypeStruct((t, D_MODEL), F32),
        grid=(t // MOE_TM, n_steps),
        in_specs=[row_spec,
                  pl.BlockSpec((None, MOE_TM, MOE_EXPERTS_PER_STEP), lambda i, k: (k, i, 0)),
                  pl.BlockSpec((None, MOE_EXPERTS_PER_STEP, D_MODEL, D_EXPERT), lambda i, k: (layer, k, 0, 0)),
                  pl.BlockSpec((None, MOE_EXPERTS_PER_STEP, D_MODEL, D_EXPERT), lambda i, k: (layer, k, 0, 0)),
                  pl.BlockSpec((None, MOE_EXPERTS_PER_STEP, D_EXPERT, D_MODEL), lambda i, k: (layer, k, 0, 0)),
                  pl.BlockSpec((1, D_MODEL), lambda i, k: (0, 0)),
                  pl.BlockSpec((1, D_MODEL), lambda i, k: (0, 0))],
        out_specs=row_spec,
        scratch_shapes=[pltpu.VMEM((MOE_TM, D_MODEL), BF16),
                        pltpu.VMEM((MOE_TM, D_MODEL), F32)],
        compiler_params=_params("arbitrary", "arbitrary"),
        name="moe_layer",
    )(x, gates, w1, w3, w2, ln_g, ln_b)


PROJ_TM = 512


def _proj_t_kernel(x_ref, wt_ref, o_ref, *, scale):
    acc = lax.dot_general(wt_ref[...], x_ref[...].astype(BF16), _NT, preferred_element_type=F32)
    o_ref[...] = (acc * scale).astype(o_ref.dtype)


def _project_t(x, w_t, scale):
    b, s, _ = x.shape
    n = w_t.shape[0]
    return pl.pallas_call(
        functools.partial(_proj_t_kernel, scale=scale),
        out_shape=jax.ShapeDtypeStruct((b, n, s), BF16),
        grid=(b, s // PROJ_TM),
        in_specs=[pl.BlockSpec((None, PROJ_TM, D_MODEL), lambda bi, i: (bi, i, 0)),
                  _const_spec(w_t.shape)],
        out_specs=pl.BlockSpec((None, n, PROJ_TM), lambda bi, i: (bi, 0, i)),
        compiler_params=_params("arbitrary", "arbitrary"),
        name="project_t",
    )(x, w_t)


def _kv_kernel(x_ref, wk_ref, wvt_ref, k_ref, vt_ref, kmean_ref):
    xb = x_ref[...].astype(BF16)
    k = jnp.dot(xb, wk_ref[...], preferred_element_type=F32)
    k_ref[...] = k.astype(BF16)
    for j in range(PROJ_TM // MOBA_BLOCK):
        kmean_ref[j] = jnp.mean(k[j * MOBA_BLOCK:(j + 1) * MOBA_BLOCK], axis=0, keepdims=True)
    vt_ref[...] = lax.dot_general(wvt_ref[...], xb, _NT, preferred_element_type=F32).astype(BF16)


def _project_kv(x, w_k, w_v_t):
    b, s, _ = x.shape
    blocks_per_tile = PROJ_TM // MOBA_BLOCK
    tiles = s // PROJ_TM
    return pl.pallas_call(
        _kv_kernel,
        out_shape=(jax.ShapeDtypeStruct((b, s, D_MODEL), BF16),
                   jax.ShapeDtypeStruct((b, D_MODEL, s), BF16),
                   jax.ShapeDtypeStruct((b * s // MOBA_BLOCK, 1, D_MODEL), F32)),
        grid=(b, tiles),
        in_specs=[pl.BlockSpec((None, PROJ_TM, D_MODEL), lambda bi, i: (bi, i, 0)),
                  _const_spec(w_k.shape), _const_spec(w_v_t.shape)],
        out_specs=(pl.BlockSpec((None, PROJ_TM, D_MODEL), lambda bi, i: (bi, i, 0)),
                   pl.BlockSpec((None, D_MODEL, PROJ_TM), lambda bi, i: (bi, 0, i)),
                   pl.BlockSpec((blocks_per_tile, 1, D_MODEL), lambda bi, i: (bi * tiles + i, 0, 0))),
        compiler_params=_params("arbitrary", "arbitrary"),
        name="project_kv",
    )(x, w_k, w_v_t)


def _t5_bucket_np(dist):
    n = np.maximum(dist, 0)
    max_exact = N_BUCKETS // 2
    nf = np.maximum(n, 1).astype(np.float32)
    large = max_exact + (np.log(nf / np.float32(max_exact)) / np.float32(math.log(MAX_DISTANCE / max_exact))
                         * np.float32(N_BUCKETS - max_exact)).astype(np.int32)
    large = np.minimum(large, N_BUCKETS - 1)
    return np.where(n < max_exact, n, large).astype(np.int32)


def _bucket_tables():
    kpos = np.arange(MOBA_BLOCK)[:, None]
    qpos = np.arange(MOBA_BLOCK)[None, :]
    own = np.where(qpos >= kpos, _t5_bucket_np(qpos - kpos), -1)
    prev = _t5_bucket_np(qpos + MOBA_BLOCK - kpos)
    return np.stack([own, prev]).astype(np.int32)


def _bias_kernel(rel_ref, bucket_ref, o_ref):
    h = pl.program_id(0)
    far = rel_ref[N_BUCKETS - 1, h]
    for t in range(2):
        bucket = bucket_ref[t]
        acc = jnp.zeros(bucket.shape, F32)
        for b in range(N_BUCKETS):
            acc = jnp.where(bucket == b, rel_ref[b, h], acc)
        o_ref[t] = jnp.where(bucket < 0, NEG_INF, (acc - far) * LOG2_E)


def _bias_tables(rel_bias):
    buckets = jnp.asarray(_bucket_tables())
    return pl.pallas_call(
        _bias_kernel,
        out_shape=jax.ShapeDtypeStruct((N_HEADS, 2, MOBA_BLOCK, MOBA_BLOCK), F32),
        grid=(N_HEADS,),
        in_specs=[pl.BlockSpec(memory_space=pltpu.SMEM), _const_spec(buckets.shape)],
        out_specs=pl.BlockSpec((None, 2, MOBA_BLOCK, MOBA_BLOCK), lambda h: (h, 0, 0, 0)),
        compiler_params=_params("arbitrary"),
        name="rel_bias_tables",
    )(rel_bias, buckets)


ATTN_HEADS = 8
ATTN_WIDTH = ATTN_HEADS * HEAD_DIM
HEADS_PER_LANE_TILE = V7X_LANES // HEAD_DIM
FAR_TILES_PER_STEP = 2
QK_LOOKAHEAD = 6
LOG2_E = math.log2(math.e)


def _attn_kernel(qt_ref, k_ref, vt_ref, kmean_ref, bias_ref, o_ref,
                 sel_ref, ml_ref, acc_ref, qp_ref, sc_ref):
    n_blocks = kmean_ref.shape[0]
    blk = lax.broadcasted_iota(jnp.int32, (n_blocks, MOBA_BLOCK), 0)
    causal = (lax.broadcasted_iota(jnp.int32, (MOBA_BLOCK, MOBA_BLOCK), 0)
              <= lax.broadcasted_iota(jnp.int32, (MOBA_BLOCK, MOBA_BLOCK), 1))
    ones_rows = jnp.ones((BF16_SUBLANE_TILE, MOBA_BLOCK), BF16)

    def lane_slab(h):
        return slice((h // HEADS_PER_LANE_TILE) * V7X_LANES, (h // HEADS_PER_LANE_TILE + 1) * V7X_LANES)

    def padded_q(h, q0):
        qh = qt_ref[h * HEAD_DIM:(h + 1) * HEAD_DIM, pl.ds(q0, MOBA_BLOCK)]
        pieces = [jnp.zeros_like(qh)] * HEADS_PER_LANE_TILE
        pieces[h % HEADS_PER_LANE_TILE] = qh
        return jnp.concatenate(pieces, axis=0)

    def scores(h, j):
        k0 = pl.multiple_of(j * MOBA_BLOCK, MOBA_BLOCK)
        kj = k_ref[pl.ds(k0, MOBA_BLOCK), lane_slab(h)]
        return jnp.dot(kj, qp_ref[h], preferred_element_type=F32)

    def weights_and_values(h, j, s, bias, sel):
        if bias is not None:
            keep = causal if sel is None else sel
            s = jnp.where(keep, s, NEG_INF) + bias[...]
        m_loc = jnp.max(s, axis=0, keepdims=True)
        if sel is None:
            shift = m_loc
        else:
            shift = jnp.where(sel, m_loc, -NEG_INF)
            m_loc = jnp.where(sel, m_loc, NEG_INF)
        p = jnp.exp2(s - shift).astype(BF16)
        k0 = pl.multiple_of(j * MOBA_BLOCK, MOBA_BLOCK)
        vtj = vt_ref[h * HEAD_DIM:(h + 1) * HEAD_DIM, pl.ds(k0, MOBA_BLOCK)]
        pv = jnp.dot(jnp.concatenate([vtj, ones_rows], axis=0), p, preferred_element_type=F32)
        return m_loc, pv[HEAD_DIM:HEAD_DIM + 1], pv[:HEAD_DIM]

    def run_tiles(tiles, carried_in, prefetch):
        n = len(tiles)
        s = [None] * n
        out = [None] * n
        for t in range(QK_LOOKAHEAD if carried_in else 0, n + QK_LOOKAHEAD):
            if t < n:
                h, j, _, _ = tiles[t]
                s[t] = scores(h, j)
            else:
                h, j = prefetch[t - n]
                sc_ref[t - n] = scores(h, j)
            d = t - QK_LOOKAHEAD
            if d >= 0:
                h, j, bias, sel = tiles[d]
                sd = sc_ref[d] if carried_in and d < QK_LOOKAHEAD else s[d]
                out[d] = weights_and_values(h, j, sd, bias, sel)
                s[d] = None
        return out

    def far_tile_ids(t):
        return [(h, t * FAR_TILES_PER_STEP + u) for h in range(ATTN_HEADS) for u in range(FAR_TILES_PER_STEP)]

    def merge(h, parts, first):
        m_new = parts[0][0]
        for m_loc, _, _ in parts[1:]:
            m_new = jnp.maximum(m_new, m_loc)
        if first:
            l_new = jnp.zeros_like(m_new)
            acc_new = jnp.zeros((HEAD_DIM, MOBA_BLOCK), F32)
        else:
            m_old = ml_ref[h, 0:1, :]
            m_new = jnp.maximum(m_new, m_old)
            alpha = jnp.exp2(m_old - m_new)
            l_new = alpha * ml_ref[h, 1:2, :]
            acc_new = alpha * acc_ref[h]
        for m_loc, l_loc, pv in parts:
            w = jnp.exp2(m_loc - m_new)
            l_new = l_new + w * l_loc
            acc_new = acc_new + w * pv
        ml_ref[h, 0:1, :] = m_new
        ml_ref[h, 1:2, :] = l_new
        acc_ref[h] = acc_new

    def q_block(i, _):
        q0 = pl.multiple_of(i * MOBA_BLOCK, MOBA_BLOCK)
        jp = jnp.maximum(i - 1, 0)
        n_far = jnp.maximum(i - 1, 0)
        for h in range(ATTN_HEADS):
            qp_ref[h] = padded_q(h, q0)

        for h in range(ATTN_HEADS):
            gate = jnp.dot(kmean_ref[:, lane_slab(h)].astype(BF16), qp_ref[h], preferred_element_type=F32)
            gate = jnp.where(blk < i, gate, NEG_INF)
            sel = jnp.zeros(gate.shape, F32)
            for r in range(MOBA_TOPK):
                top = jnp.max(gate, axis=0, keepdims=True)
                first = jnp.min(jnp.where(gate == top, blk, n_blocks), axis=0, keepdims=True)
                hit = blk == first
                sel = jnp.where(hit & (r < i), 1.0, sel)
                gate = jnp.where(hit, -jnp.inf, gate)
            sel_ref[h] = sel

        tiles = []
        for h in range(ATTN_HEADS):
            tiles.append((h, i, bias_ref.at[h, 0], None))
            tiles.append((h, jp, bias_ref.at[h, 1], sel_ref[h, pl.ds(jp, 1), :] > 0.5))
        last_far = jnp.maximum(n_far - 1, 0)

        def prefetch_ids(t):
            return [(h, jnp.minimum(j, last_far)) for h, j in far_tile_ids(t)[:QK_LOOKAHEAD]]

        parts = run_tiles(tiles, carried_in=False, prefetch=prefetch_ids(0))
        for h in range(ATTN_HEADS):
            merge(h, parts[2 * h:2 * h + 2], first=True)

        def far_step(t, _):
            tiles = []
            for h, j in far_tile_ids(t):
                jc = jnp.minimum(j, last_far)
                sel = (sel_ref[h, pl.ds(jc, 1), :] > 0.5) & (j < n_far)
                tiles.append((h, jc, None, sel))
            parts = run_tiles(tiles, carried_in=True, prefetch=prefetch_ids(t + 1))
            for h in range(ATTN_HEADS):
                merge(h, parts[h * FAR_TILES_PER_STEP:(h + 1) * FAR_TILES_PER_STEP], first=False)
            return 0

        lax.fori_loop(0, (n_far + FAR_TILES_PER_STEP - 1) // FAR_TILES_PER_STEP, far_step, 0)

        for h in range(ATTN_HEADS):
            o_ref[h * HEAD_DIM:(h + 1) * HEAD_DIM, pl.ds(q0, MOBA_BLOCK)] = (
                acc_ref[h] / ml_ref[h, 1:2, :]).astype(o_ref.dtype)
        return 0

    lax.fori_loop(0, n_blocks, q_block, 0)


def _moba_attention(qt, k, vt, kmean, bias):
    b, _, s = qt.shape
    n_blocks = s // MOBA_BLOCK
    single = dict(pipeline_mode=pl.Buffered(1))
    t_spec = pl.BlockSpec((None, ATTN_WIDTH, s), lambda bi, hi: (bi, hi, 0), **single)
    return pl.pallas_call(
        _attn_kernel,
        out_shape=jax.ShapeDtypeStruct((b, D_MODEL, s), BF16),
        grid=(b, N_HEADS // ATTN_HEADS),
        in_specs=[t_spec,
                  pl.BlockSpec((None, s, ATTN_WIDTH), lambda bi, hi: (bi, 0, hi), **single),
                  t_spec,
                  pl.BlockSpec((None, n_blocks, ATTN_WIDTH), lambda bi, hi: (bi, 0, hi)),
                  pl.BlockSpec((ATTN_HEADS, 2, MOBA_BLOCK, MOBA_BLOCK), lambda bi, hi: (hi, 0, 0, 0), **single)],
        out_specs=pl.BlockSpec((None, ATTN_WIDTH, s), lambda bi, hi: (bi, hi, 0)),
        scratch_shapes=[pltpu.VMEM((ATTN_HEADS, n_blocks, MOBA_BLOCK), F32),
                        pltpu.VMEM((ATTN_HEADS, 2, MOBA_BLOCK), F32),
                        pltpu.VMEM((ATTN_HEADS, HEAD_DIM, MOBA_BLOCK), F32),
                        pltpu.VMEM((ATTN_HEADS, V7X_LANES, MOBA_BLOCK), BF16),
                        pltpu.VMEM((QK_LOOKAHEAD, MOBA_BLOCK, MOBA_BLOCK), F32)],
        compiler_params=_params("arbitrary", "arbitrary"),
        name="moba_attention",
    )(qt, k, vt, kmean, bias)


OPROJ_TM = 512


def _oproj_kernel(at_ref, w_ref, x_ref, lng_ref, lnb_ref, o_ref):
    h = lax.dot_general(at_ref[...], w_ref[...], _TN, preferred_element_type=F32)
    o_ref[...] = _layer_norm(DEEPNORM_ALPHA * x_ref[...] + h, lng_ref[...], lnb_ref[...])


def _out_project(a_t, w_o, x, ln_g, ln_b):
    b, s, _ = x.shape
    row_spec = pl.BlockSpec((None, OPROJ_TM, D_MODEL), lambda bi, i: (bi, i, 0))
    return pl.pallas_call(
        _oproj_kernel,
        out_shape=jax.ShapeDtypeStruct((b, s, D_MODEL), F32),
        grid=(b, s // OPROJ_TM),
        in_specs=[pl.BlockSpec((None, D_MODEL, OPROJ_TM), lambda bi, i: (bi, 0, i)),
                  _const_spec(w_o.shape), row_spec,
                  _const_spec(ln_g.shape), _const_spec(ln_b.shape)],
        out_specs=row_spec,
        compiler_params=_params("arbitrary", "arbitrary"),
        name="out_project",
    )(a_t, w_o, x, ln_g, ln_b)


def kernel(x, ln_g, ln_b, a_w_in, a_v_g, a_v_b, a_w_s, a_b_s, a_w_out, w_k, w_v, b_w_q, b_w_o,
           rel_bias, w_router, b_router, e_w1, e_w3, e_w2):
    batch, seq, _ = x.shape
    t = batch * seq
    n_blocks = seq // MOBA_BLOCK
    xt = x.reshape(t, D_MODEL)
    wr_t = w_router.T.astype(BF16)
    br = b_router.reshape(N_EXPERTS, 1)
    n_steps = N_EXPERTS // MOE_EXPERTS_PER_STEP

    def ln_rows(l, j):
        return ln_g[l, j].reshape(1, D_MODEL), ln_b[l, j].reshape(1, D_MODEL)

    k = vt = kmean = bias = None
    for l in range(DEPTH):
        g0, b0 = ln_rows(l, 0)
        if l < N_A_LAYERS:
            xt = _gmlp_layer(xt, a_w_in[l].astype(BF16),
                             a_v_g[l].reshape(1, GMLP_HALF), a_v_b[l].reshape(1, GMLP_HALF),
                             a_w_s[l], a_b_s[l].T, a_w_out[l].astype(BF16), g0, b0)
        else:
            x3 = xt.reshape(batch, seq, D_MODEL)
            if l == N_A_LAYERS:
                k, vt, kmean = _project_kv(x3, w_k.astype(BF16), w_v.T.astype(BF16))
                kmean = kmean.reshape(batch, n_blocks, D_MODEL)
                bias = _bias_tables(rel_bias)
            j = l - N_A_LAYERS
            qt = _project_t(x3, b_w_q[j].T.astype(BF16), HEAD_DIM ** -0.5 * LOG2_E)
            ot = _moba_attention(qt, k, vt, kmean, bias)
            xt = _out_project(ot, b_w_o[j].astype(BF16), x3, g0, b0).reshape(t, D_MODEL)
        gates_t = _router(xt, wr_t, br)
        gates = gates_t.reshape(n_steps, MOE_EXPERTS_PER_STEP, t).transpose(0, 2, 1)
        g1, b1 = ln_rows(l, 1)
        xt = _moe_layer(xt, gates, e_w1, e_w3, e_w2, g1, b1, layer=l)
    return xt.reshape(batch, seq, D_MODEL)
```

```python
import functools
import math

import numpy as np
import jax
import jax.numpy as jnp
from jax import lax
from jax.experimental import pallas as pl
from jax.experimental.pallas import tpu as pltpu

D_MODEL = 1024
DEPTH = 4
N_A_LAYERS = DEPTH // 2
GMLP_HALF = 3 * D_MODEL
GMLP_GROUPS = 8
GMLP_GROUP_WIDTH = GMLP_HALF // GMLP_GROUPS
GMLP_CHUNK = 128
N_HEADS = 16
HEAD_DIM = D_MODEL // N_HEADS
MOBA_BLOCK = 256
MOBA_TOPK = 3
N_BUCKETS = 32
MAX_DISTANCE = 128
N_EXPERTS = 16
N_GROUPS = 4
EXPERTS_PER_GROUP = N_EXPERTS // N_GROUPS
TOP_K = 2
MOE_PAIRS = EXPERTS_PER_GROUP * (EXPERTS_PER_GROUP - 1) // 2
MOE_CLASSES = N_GROUPS * MOE_PAIRS
ROUTER_ROWS = 8
D_EXPERT = D_MODEL // 4
DEEPNORM_ALPHA = (2 * DEPTH) ** 0.25
LN_EPS = 1e-5
NEG_INF = -1e30

V7X_VMEM_LIMIT_BYTES = 56 * 1024 * 1024
V7X_LANES = 128
BF16_SUBLANE_TILE = 16

BF16 = jnp.bfloat16
F32 = jnp.float32

_NT = (((1,), (1,)), ((), ()))
_TN = (((0,), (0,)), ((), ()))


def _params(*semantics):
    return pltpu.CompilerParams(dimension_semantics=semantics,
                                vmem_limit_bytes=V7X_VMEM_LIMIT_BYTES)


def _const_spec(shape):
    zeros = (0,) * len(shape)
    return pl.BlockSpec(shape, lambda *_: zeros, pipeline_mode=pl.Buffered(1))


def _layer_norm(y, g, b):
    mu = jnp.mean(y, axis=-1, keepdims=True)
    d = y - mu
    var = jnp.mean(d * d, axis=-1, keepdims=True)
    return d * lax.rsqrt(var + LN_EPS) * g + b


GMLP_TM = 512
GMLP_GROUPS_PER_STEP = 2


def _gmlp_kernel(x_ref, w_in_ref, vg_ref, vb_ref, ws_ref, bs_ref, w_out_ref,
                 lng_ref, lnb_ref, o_ref, vn_ref, sv_ref):
    x = x_ref[...]
    xb = x.astype(BF16)
    v = jax.nn.gelu(jnp.dot(xb, w_in_ref[:, GMLP_HALF:], preferred_element_type=F32))
    vn_ref[...] = _layer_norm(v, vg_ref[...], vb_ref[...]).astype(BF16)

    row = lax.broadcasted_iota(jnp.int32, (GMLP_CHUNK, GMLP_CHUNK), 0)
    col = lax.broadcasted_iota(jnp.int32, (GMLP_CHUNK, GMLP_CHUNK), 1)
    causal = row >= col
    width = GMLP_GROUPS_PER_STEP * GMLP_GROUP_WIDTH
    acc = jnp.zeros((GMLP_TM, D_MODEL), F32)
    for step in range(GMLP_GROUPS // GMLP_GROUPS_PER_STEP):
        lo = step * width
        u = jax.nn.gelu(jnp.dot(xb, w_in_ref[:, lo:lo + width], preferred_element_type=F32))
        for gi in range(GMLP_GROUPS_PER_STEP):
            g = step * GMLP_GROUPS_PER_STEP + gi
            w_s = jnp.where(causal, ws_ref[g], 0.0).astype(BF16)
            b_s = bs_ref[:, g:g + 1]
            glo = g * GMLP_GROUP_WIDTH
            for c in range(GMLP_TM // GMLP_CHUNK):
                rows = slice(c * GMLP_CHUNK, (c + 1) * GMLP_CHUNK)
                vc = vn_ref[rows, glo:glo + GMLP_GROUP_WIDTH]
                sv_ref[rows, gi * GMLP_GROUP_WIDTH:(gi + 1) * GMLP_GROUP_WIDTH] = (
                    jnp.dot(w_s, vc, preferred_element_type=F32) + b_s)
        gated = (u * sv_ref[...]).astype(BF16)
        acc = acc + jnp.dot(gated, w_out_ref[lo:lo + width, :], preferred_element_type=F32)
    o_ref[...] = _layer_norm(DEEPNORM_ALPHA * x + acc, lng_ref[...], lnb_ref[...])


def _gmlp_layer(x, w_in, v_g, v_b, w_s, b_s_t, w_out, ln_g, ln_b):
    t = x.shape[0]
    row_spec = pl.BlockSpec((GMLP_TM, D_MODEL), lambda i: (i, 0))
    return pl.pallas_call(
        _gmlp_kernel,
        out_shape=jax.ShapeDtypeStruct((t, D_MODEL), F32),
        grid=(t // GMLP_TM,),
        in_specs=[row_spec,
                  _const_spec(w_in.shape), _const_spec(v_g.shape), _const_spec(v_b.shape),
                  _const_spec(w_s.shape), _const_spec(b_s_t.shape), _const_spec(w_out.shape),
                  _const_spec(ln_g.shape), _const_spec(ln_b.shape)],
        out_specs=row_spec,
        scratch_shapes=[pltpu.VMEM((GMLP_TM, GMLP_HALF), BF16),
                        pltpu.VMEM((GMLP_TM, GMLP_GROUPS_PER_STEP * GMLP_GROUP_WIDTH), F32)],
        compiler_params=_params("arbitrary"),
        name="gmlp_layer",
    )(x, w_in, v_g, v_b, w_s, b_s_t, w_out, ln_g, ln_b)


ROUTER_TM = 2048


def _router_kernel(x_ref, wr_ref, br_ref, o_ref):
    xb = x_ref[...].astype(BF16)
    logits = lax.dot_general(wr_ref[...], xb, _NT, preferred_element_type=F32) + br_ref[...]
    m = jnp.max(logits, axis=0, keepdims=True)
    ex = jnp.exp(logits - m)
    probs = ex / jnp.sum(ex, axis=0, keepdims=True)
    p = [probs[e:e + 1, :] for e in range(N_EXPERTS)]

    scores = []
    for g in range(N_GROUPS):
        q = p[g * EXPERTS_PER_GROUP:(g + 1) * EXPERTS_PER_GROUP]
        best = None
        for a in range(EXPERTS_PER_GROUP):
            for b in range(a + 1, EXPERTS_PER_GROUP):
                s = q[a] + q[b]
                best = s if best is None else jnp.maximum(best, s)
        scores.append(best)
    best_score = scores[0]
    gsel = jnp.zeros_like(best_score, dtype=jnp.int32)
    for g in range(1, N_GROUPS):
        better = scores[g] > best_score
        best_score = jnp.where(better, scores[g], best_score)
        gsel = jnp.where(better, g, gsel)

    slot_chosen = [None] * EXPERTS_PER_GROUP
    slot_gate = [None] * EXPERTS_PER_GROUP
    for e in range(N_EXPERTS):
        g, k = divmod(e, EXPERTS_PER_GROUP)
        rank = jnp.zeros_like(gsel)
        for e2 in range(g * EXPERTS_PER_GROUP, (g + 1) * EXPERTS_PER_GROUP):
            if e2 == e:
                continue
            ahead = (p[e2] > p[e]) | ((p[e2] == p[e]) & (e2 < e))
            rank = rank + ahead.astype(jnp.int32)
        chosen = (gsel == g) & (rank < TOP_K)
        gate = jnp.where(chosen, p[e] / best_score, 0.0)
        slot_chosen[k] = chosen if slot_chosen[k] is None else slot_chosen[k] | chosen
        slot_gate[k] = gate if slot_gate[k] is None else slot_gate[k] + gate
    c0, c1, c2, c3 = slot_chosen
    lo = jnp.where(c0, 0, jnp.where(c1, 1, 2))
    hi = jnp.where(c3, 3, jnp.where(c2, 2, 1))
    w_lo = jnp.where(c0, slot_gate[0], jnp.where(c1, slot_gate[1], slot_gate[2]))
    w_hi = jnp.where(c3, slot_gate[3], jnp.where(c2, slot_gate[2], slot_gate[1]))
    pair = jnp.right_shift(lo * (7 - lo), 1) + hi - lo - 1
    cls = gsel * MOE_PAIRS + pair
    zero = jnp.zeros_like(w_lo)
    o_ref[...] = jnp.concatenate([cls.astype(F32), w_lo, w_hi] + [zero] * (ROUTER_ROWS - 3), axis=0)


def _router(x, wr_t, br):
    t = x.shape[0]
    return pl.pallas_call(
        _router_kernel,
        out_shape=jax.ShapeDtypeStruct((ROUTER_ROWS, t), F32),
        grid=(t // ROUTER_TM,),
        in_specs=[pl.BlockSpec((ROUTER_TM, D_MODEL), lambda i: (i, 0)),
                  _const_spec(wr_t.shape), _const_spec(br.shape)],
        out_specs=pl.BlockSpec((ROUTER_ROWS, ROUTER_TM), lambda i: (0, i)),
        compiler_params=_params("arbitrary"),
        name="moe_router",
    )(x, wr_t, br)


MOE_TM = 256
FEATURE_ROWS = D_MODEL // V7X_LANES
DISPATCH_TD = 256
DMA_UNROLL = 8


def _token_rows(ref, n_tokens, rows_per_token, first_row):
    return jnp.concatenate(
        [ref[pl.ds(first_row + c, n_tokens, stride=rows_per_token), :] for c in range(FEATURE_ROWS)], axis=1)


def _store_token_rows(ref, value, n_tokens, rows_per_token, first_row):
    for c in range(FEATURE_ROWS):
        ref[pl.ds(first_row + c, n_tokens, stride=rows_per_token), :] = value[:, c * V7X_LANES:(c + 1) * V7X_LANES]


def _dispatch_kernel(pos_ref, x_ref, buf_in_ref, buf_ref, stage_ref, sems):
    del buf_in_ref
    i = pl.program_id(0)
    n = pl.num_programs(0)
    slot = i % 2
    stage = stage_ref.at[slot]
    rows = DISPATCH_TD * FEATURE_ROWS

    def wait_slot(s):
        pltpu.make_async_copy(stage_ref.at[s], buf_ref.at[pl.ds(0, rows)], sems.at[s]).wait()

    @pl.when(i >= 2)
    def _():
        wait_slot(slot)

    _store_token_rows(stage, x_ref[...], DISPATCH_TD, FEATURE_ROWS, 0)

    def issue(r, _):
        dst = pl.multiple_of(pos_ref[i * DISPATCH_TD + r] * FEATURE_ROWS, FEATURE_ROWS)
        src = pl.multiple_of(r * FEATURE_ROWS, FEATURE_ROWS)
        pltpu.make_async_copy(stage.at[pl.ds(src, FEATURE_ROWS)], buf_ref.at[pl.ds(dst, FEATURE_ROWS)],
                              sems.at[slot]).start()
        return 0

    lax.fori_loop(0, DISPATCH_TD, issue, 0, unroll=DMA_UNROLL)

    @pl.when(i == n - 1)
    def _():
        wait_slot(slot)

        @pl.when(n > 1)
        def _():
            wait_slot(1 - slot)


def _dispatch(pos, x, buf):
    t = x.shape[0]
    return pl.pallas_call(
        _dispatch_kernel,
        out_shape=jax.ShapeDtypeStruct(buf.shape, buf.dtype),
        grid_spec=pltpu.PrefetchScalarGridSpec(
            num_scalar_prefetch=1,
            grid=(t // DISPATCH_TD,),
            in_specs=[pl.BlockSpec((DISPATCH_TD, D_MODEL), lambda i, pos: (i, 0)),
                      pl.BlockSpec(memory_space=pl.ANY)],
            out_specs=pl.BlockSpec(memory_space=pl.ANY),
            scratch_shapes=[pltpu.VMEM((2, DISPATCH_TD * FEATURE_ROWS, V7X_LANES), F32),
                            pltpu.SemaphoreType.DMA((2,))]),
        input_output_aliases={2: 0},
        compiler_params=_params("arbitrary"),
        name="moe_dispatch",
    )(pos, x, buf)


def _expert_kernel(ea_ref, eb_ref, fresh_ref, used_ref, xs_ref,
                   w1a_ref, w1b_ref, w3a_ref, w3b_ref, w2a_ref, w2b_ref, ys_ref,
                   w1_s, w3_s, w2_s):
    del ea_ref, eb_ref
    i = pl.program_id(0)

    @pl.when(fresh_ref[i] == 1)
    def _():
        for e, (w1, w3, w2) in enumerate(((w1a_ref, w3a_ref, w2a_ref), (w1b_ref, w3b_ref, w2b_ref))):
            w1_s[e] = w1[...].astype(BF16)
            w3_s[e] = w3[...].astype(BF16)
            w2_s[e] = w2[...].astype(BF16)

    @pl.when(used_ref[i] == 1)
    def _():
        xb = _token_rows(xs_ref, MOE_TM, FEATURE_ROWS, 0).astype(BF16)
        for e in range(TOP_K):
            h1 = jnp.dot(xb, w1_s[e], preferred_element_type=F32)
            h3 = jnp.dot(xb, w3_s[e], preferred_element_type=F32)
            h = (jax.nn.silu(h1) * h3).astype(BF16)
            y = jnp.dot(h, w2_s[e], preferred_element_type=F32)
            _store_token_rows(ys_ref, y, MOE_TM, TOP_K * FEATURE_ROWS, e * FEATURE_ROWS)

    @pl.when(used_ref[i] == 0)
    def _():
        ys_ref[...] = jnp.zeros_like(ys_ref)


def _expert_ffn(ea, eb, fresh, used, xs, w1, w3, w2, layer):
    n_tiles = ea.shape[0]

    def w_spec(shape, which):
        return pl.BlockSpec((None, None) + shape,
                            lambda i, ea, eb, fresh, used: (layer, (ea, eb)[which][i], 0, 0))

    up, down = (D_MODEL, D_EXPERT), (D_EXPERT, D_MODEL)
    return pl.pallas_call(
        _expert_kernel,
        out_shape=jax.ShapeDtypeStruct((n_tiles * MOE_TM * TOP_K * FEATURE_ROWS, V7X_LANES), F32),
        grid_spec=pltpu.PrefetchScalarGridSpec(
            num_scalar_prefetch=4,
            grid=(n_tiles,),
            in_specs=[pl.BlockSpec((MOE_TM * FEATURE_ROWS, V7X_LANES), lambda i, *_: (i, 0)),
                      w_spec(up, 0), w_spec(up, 1), w_spec(up, 0), w_spec(up, 1),
                      w_spec(down, 0), w_spec(down, 1)],
            out_specs=pl.BlockSpec((MOE_TM * TOP_K * FEATURE_ROWS, V7X_LANES), lambda i, *_: (i, 0)),
            scratch_shapes=[pltpu.VMEM((TOP_K,) + up, BF16), pltpu.VMEM((TOP_K,) + up, BF16),
                            pltpu.VMEM((TOP_K,) + down, BF16)]),
        compiler_params=_params("arbitrary"),
        name="moe_experts",
    )(ea, eb, fresh, used, xs, w1, w1, w3, w3, w2, w2)


def _combine_kernel(pos_ref, ys_ref, x_ref, gate_ref, lng_ref, lnb_ref, o_ref, stage_ref, sems):
    i = pl.program_id(0)
    n = pl.num_programs(0)
    slot = i % 2
    rows_per_token = TOP_K * FEATURE_ROWS
    rows = DISPATCH_TD * rows_per_token

    def issue_step(step, s):
        def issue(r, _):
            src = pl.multiple_of(pos_ref[step * DISPATCH_TD + r] * rows_per_token, rows_per_token)
            dst = pl.multiple_of(r * rows_per_token, rows_per_token)
            pltpu.make_async_copy(ys_ref.at[pl.ds(src, rows_per_token)],
                                  stage_ref.at[s, pl.ds(dst, rows_per_token)], sems.at[s]).start()
            return 0
        lax.fori_loop(0, DISPATCH_TD, issue, 0, unroll=DMA_UNROLL)

    @pl.when(i == 0)
    def _():
        issue_step(0, 0)

    @pl.when(i + 1 < n)
    def _():
        issue_step(i + 1, 1 - slot)

    pltpu.make_async_copy(ys_ref.at[pl.ds(0, rows)], stage_ref.at[slot], sems.at[slot]).wait()

    stage = stage_ref.at[slot]
    gates = gate_ref[...]
    f = (gates[:, 0:1] * _token_rows(stage, DISPATCH_TD, rows_per_token, 0)
         + gates[:, 1:2] * _token_rows(stage, DISPATCH_TD, rows_per_token, FEATURE_ROWS))
    o_ref[...] = _layer_norm(DEEPNORM_ALPHA * x_ref[...] + f, lng_ref[...], lnb_ref[...])


def _combine(pos, ys, x, gates, ln_g, ln_b):
    t = x.shape[0]
    row_spec = pl.BlockSpec((DISPATCH_TD, D_MODEL), lambda i, pos: (i, 0))
    return pl.pallas_call(
        _combine_kernel,
        out_shape=jax.ShapeDtypeStruct((t, D_MODEL), F32),
        grid_spec=pltpu.PrefetchScalarGridSpec(
            num_scalar_prefetch=1,
            grid=(t // DISPATCH_TD,),
            in_specs=[pl.BlockSpec(memory_space=pl.ANY), row_spec,
                      pl.BlockSpec((DISPATCH_TD, TOP_K), lambda i, pos: (i, 0)),
                      pl.BlockSpec((1, D_MODEL), lambda i, pos: (0, 0)),
                      pl.BlockSpec((1, D_MODEL), lambda i, pos: (0, 0))],
            out_specs=row_spec,
            scratch_shapes=[pltpu.VMEM((2, DISPATCH_TD * TOP_K * FEATURE_ROWS, V7X_LANES), F32),
                            pltpu.SemaphoreType.DMA((2,))]),
        compiler_params=_params("arbitrary"),
        name="moe_combine",
    )(pos, ys, x, gates, ln_g, ln_b)


def _pair_table():
    lo, hi = [], []
    for g in range(N_GROUPS):
        for a in range(EXPERTS_PER_GROUP):
            for b in range(a + 1, EXPERTS_PER_GROUP):
                lo.append(g * EXPERTS_PER_GROUP + a)
                hi.append(g * EXPERTS_PER_GROUP + b)
    return np.asarray(lo, np.int32), np.asarray(hi, np.int32)


def _sort_plan(cls, n_tiles):
    onehot = (cls[:, None] == jnp.arange(MOE_CLASSES, dtype=jnp.int32)[None, :]).astype(jnp.int32)
    running = jnp.cumsum(onehot, axis=0)
    counts = running[-1]
    padded = (counts + MOE_TM - 1) // MOE_TM * MOE_TM
    ends = jnp.cumsum(padded)
    rank = jnp.sum(running * onehot, axis=1) - 1
    pos = jnp.sum(onehot * (ends - padded)[None, :], axis=1) + rank
    tile_cls = jnp.searchsorted(ends, jnp.arange(n_tiles, dtype=jnp.int32) * MOE_TM, side="right")
    used = (tile_cls < MOE_CLASSES).astype(jnp.int32)
    tile_cls = jnp.minimum(tile_cls, MOE_CLASSES - 1).astype(jnp.int32)
    fresh = jnp.concatenate([jnp.ones((1,), jnp.int32),
                             (tile_cls[1:] != tile_cls[:-1]).astype(jnp.int32)])
    lo, hi = _pair_table()
    return pos.astype(jnp.int32), jnp.asarray(lo)[tile_cls], jnp.asarray(hi)[tile_cls], fresh, used


def _moe_layer(x, buf, wr_t, br, w1, w3, w2, ln_g, ln_b, layer):
    t = x.shape[0]
    n_tiles = buf.shape[0] // (MOE_TM * FEATURE_ROWS)
    routed = _router(x, wr_t, br)
    pos, ea, eb, fresh, used = _sort_plan(routed[0].astype(jnp.int32), n_tiles)
    buf = _dispatch(pos, x, buf)
    ys = _expert_ffn(ea, eb, fresh, used, buf, w1, w3, w2, layer)
    return _combine(pos, ys, x, routed[1:1 + TOP_K].T, ln_g, ln_b), buf


PROJ_TM = 512


def _proj_t_kernel(x_ref, wt_ref, o_ref, *, scale):
    acc = lax.dot_general(wt_ref[...], x_ref[...].astype(BF16), _NT, preferred_element_type=F32)
    o_ref[...] = (acc * scale).astype(o_ref.dtype)


def _project_t(x, w_t, scale):
    b, s, _ = x.shape
    n = w_t.shape[0]
    return pl.pallas_call(
        functools.partial(_proj_t_kernel, scale=scale),
        out_shape=jax.ShapeDtypeStruct((b, n, s), BF16),
        grid=(b, s // PROJ_TM),
        in_specs=[pl.BlockSpec((None, PROJ_TM, D_MODEL), lambda bi, i: (bi, i, 0)),
                  _const_spec(w_t.shape)],
        out_specs=pl.BlockSpec((None, n, PROJ_TM), lambda bi, i: (bi, 0, i)),
        compiler_params=_params("arbitrary", "arbitrary"),
        name="project_t",
    )(x, w_t)


def _kv_kernel(x_ref, wk_ref, wvt_ref, k_ref, vt_ref, kmean_ref):
    xb = x_ref[...].astype(BF16)
    k = jnp.dot(xb, wk_ref[...], preferred_element_type=F32)
    k_ref[...] = k.astype(BF16)
    for j in range(PROJ_TM // MOBA_BLOCK):
        kmean_ref[j] = jnp.mean(k[j * MOBA_BLOCK:(j + 1) * MOBA_BLOCK], axis=0, keepdims=True)
    vt_ref[...] = lax.dot_general(wvt_ref[...], xb, _NT, preferred_element_type=F32).astype(BF16)


def _project_kv(x, w_k, w_v_t):
    b, s, _ = x.shape
    blocks_per_tile = PROJ_TM // MOBA_BLOCK
    tiles = s // PROJ_TM
    return pl.pallas_call(
        _kv_kernel,
        out_shape=(jax.ShapeDtypeStruct((b, s, D_MODEL), BF16),
                   jax.ShapeDtypeStruct((b, D_MODEL, s), BF16),
                   jax.ShapeDtypeStruct((b * s // MOBA_BLOCK, 1, D_MODEL), F32)),
        grid=(b, tiles),
        in_specs=[pl.BlockSpec((None, PROJ_TM, D_MODEL), lambda bi, i: (bi, i, 0)),
                  _const_spec(w_k.shape), _const_spec(w_v_t.shape)],
        out_specs=(pl.BlockSpec((None, PROJ_TM, D_MODEL), lambda bi, i: (bi, i, 0)),
                   pl.BlockSpec((None, D_MODEL, PROJ_TM), lambda bi, i: (bi, 0, i)),
                   pl.BlockSpec((blocks_per_tile, 1, D_MODEL), lambda bi, i: (bi * tiles + i, 0, 0))),
        compiler_params=_params("arbitrary", "arbitrary"),
        name="project_kv",
    )(x, w_k, w_v_t)


def _t5_bucket_np(dist):
    n = np.maximum(dist, 0)
    max_exact = N_BUCKETS // 2
    nf = np.maximum(n, 1).astype(np.float32)
    large = max_exact + (np.log(nf / np.float32(max_exact)) / np.float32(math.log(MAX_DISTANCE / max_exact))
                         * np.float32(N_BUCKETS - max_exact)).astype(np.int32)
    large = np.minimum(large, N_BUCKETS - 1)
    return np.where(n < max_exact, n, large).astype(np.int32)


def _bucket_tables():
    kpos = np.arange(MOBA_BLOCK)[:, None]
    qpos = np.arange(MOBA_BLOCK)[None, :]
    own = np.where(qpos >= kpos, _t5_bucket_np(qpos - kpos), -1)
    prev = _t5_bucket_np(qpos + MOBA_BLOCK - kpos)
    return np.stack([own, prev]).astype(np.int32)


def _bias_kernel(rel_ref, bucket_ref, o_ref):
    h = pl.program_id(0)
    far = rel_ref[N_BUCKETS - 1, h]
    for t in range(2):
        bucket = bucket_ref[t]
        acc = jnp.zeros(bucket.shape, F32)
        for b in range(N_BUCKETS):
            acc = jnp.where(bucket == b, rel_ref[b, h], acc)
        o_ref[t] = jnp.where(bucket < 0, NEG_INF, (acc - far) * LOG2_E)


def _bias_tables(rel_bias):
    buckets = jnp.asarray(_bucket_tables())
    return pl.pallas_call(
        _bias_kernel,
        out_shape=jax.ShapeDtypeStruct((N_HEADS, 2, MOBA_BLOCK, MOBA_BLOCK), F32),
        grid=(N_HEADS,),
        in_specs=[pl.BlockSpec(memory_space=pltpu.SMEM), _const_spec(buckets.shape)],
        out_specs=pl.BlockSpec((None, 2, MOBA_BLOCK, MOBA_BLOCK), lambda h: (h, 0, 0, 0)),
        compiler_params=_params("arbitrary"),
        name="rel_bias_tables",
    )(rel_bias, buckets)


ATTN_HEADS = 8
ATTN_WIDTH = ATTN_HEADS * HEAD_DIM
HEADS_PER_LANE_TILE = V7X_LANES // HEAD_DIM
FAR_TILES_PER_STEP = 2
QK_LOOKAHEAD = 6
LOG2_E = math.log2(math.e)


def _attn_kernel(qt_ref, k_ref, vt_ref, kmean_ref, bias_ref, o_ref,
                 sel_ref, ml_ref, acc_ref, qp_ref, sc_ref):
    n_blocks = kmean_ref.shape[0]
    blk = lax.broadcasted_iota(jnp.int32, (n_blocks, MOBA_BLOCK), 0)
    causal = (lax.broadcasted_iota(jnp.int32, (MOBA_BLOCK, MOBA_BLOCK), 0)
              <= lax.broadcasted_iota(jnp.int32, (MOBA_BLOCK, MOBA_BLOCK), 1))
    ones_rows = jnp.ones((BF16_SUBLANE_TILE, MOBA_BLOCK), BF16)

    def lane_slab(h):
        return slice((h // HEADS_PER_LANE_TILE) * V7X_LANES, (h // HEADS_PER_LANE_TILE + 1) * V7X_LANES)

    def padded_q(h, q0):
        qh = qt_ref[h * HEAD_DIM:(h + 1) * HEAD_DIM, pl.ds(q0, MOBA_BLOCK)]
        pieces = [jnp.zeros_like(qh)] * HEADS_PER_LANE_TILE
        pieces[h % HEADS_PER_LANE_TILE] = qh
        return jnp.concatenate(pieces, axis=0)

    def scores(h, j):
        k0 = pl.multiple_of(j * MOBA_BLOCK, MOBA_BLOCK)
        kj = k_ref[pl.ds(k0, MOBA_BLOCK), lane_slab(h)]
        return jnp.dot(kj, qp_ref[h], preferred_element_type=F32)

    def weights_and_values(h, j, s, bias, sel):
        if bias is not None:
            keep = causal if sel is None else sel
            s = jnp.where(keep, s, NEG_INF) + bias[...]
        m_loc = jnp.max(s, axis=0, keepdims=True)
        if sel is None:
            shift = m_loc
        else:
            shift = jnp.where(sel, m_loc, -NEG_INF)
            m_loc = jnp.where(sel, m_loc, NEG_INF)
        p = jnp.exp2(s - shift).astype(BF16)
        k0 = pl.multiple_of(j * MOBA_BLOCK, MOBA_BLOCK)
        vtj = vt_ref[h * HEAD_DIM:(h + 1) * HEAD_DIM, pl.ds(k0, MOBA_BLOCK)]
        pv = jnp.dot(jnp.concatenate([vtj, ones_rows], axis=0), p, preferred_element_type=F32)
        return m_loc, pv[HEAD_DIM:HEAD_DIM + 1], pv[:HEAD_DIM]

    def run_tiles(tiles, carried_in, prefetch):
        n = len(tiles)
        s = [None] * n
        out = [None] * n
        for t in range(QK_LOOKAHEAD if carried_in else 0, n + QK_LOOKAHEAD):
            if t < n:
                h, j, _, _ = tiles[t]
                s[t] = scores(h, j)
            else:
                h, j = prefetch[t - n]
                sc_ref[t - n] = scores(h, j)
            d = t - QK_LOOKAHEAD
            if d >= 0:
                h, j, bias, sel = tiles[d]
                sd = sc_ref[d] if carried_in and d < QK_LOOKAHEAD else s[d]
                out[d] = weights_and_values(h, j, sd, bias, sel)
                s[d] = None
        return out

    def far_tile_ids(t):
        return [(h, t * FAR_TILES_PER_STEP + u) for h in range(ATTN_HEADS) for u in range(FAR_TILES_PER_STEP)]

    def merge(h, parts, first):
        m_new = parts[0][0]
        for m_loc, _, _ in parts[1:]:
            m_new = jnp.maximum(m_new, m_loc)
        if first:
            l_new = jnp.zeros_like(m_new)
            acc_new = jnp.zeros((HEAD_DIM, MOBA_BLOCK), F32)
        else:
            m_old = ml_ref[h, 0:1, :]
            m_new = jnp.maximum(m_new, m_old)
            alpha = jnp.exp2(m_old - m_new)
            l_new = alpha * ml_ref[h, 1:2, :]
            acc_new = alpha * acc_ref[h]
        for m_loc, l_loc, pv in parts:
            w = jnp.exp2(m_loc - m_new)
            l_new = l_new + w * l_loc
            acc_new = acc_new + w * pv
        ml_ref[h, 0:1, :] = m_new
        ml_ref[h, 1:2, :] = l_new
        acc_ref[h] = acc_new

    def q_block(i, _):
        q0 = pl.multiple_of(i * MOBA_BLOCK, MOBA_BLOCK)
        jp = jnp.maximum(i - 1, 0)
        n_far = jnp.maximum(i - 1, 0)
        for h in range(ATTN_HEADS):
            qp_ref[h] = padded_q(h, q0)

        for h in range(ATTN_HEADS):
            gate = jnp.dot(kmean_ref[:, lane_slab(h)].astype(BF16), qp_ref[h], preferred_element_type=F32)
            gate = jnp.where(blk < i, gate, NEG_INF)
            sel = jnp.zeros(gate.shape, F32)
            for r in range(MOBA_TOPK):
                top = jnp.max(gate, axis=0, keepdims=True)
                first = jnp.min(jnp.where(gate == top, blk, n_blocks), axis=0, keepdims=True)
                hit = blk == first
                sel = jnp.where(hit & (r < i), 1.0, sel)
                gate = jnp.where(hit, -jnp.inf, gate)
            sel_ref[h] = sel

        tiles = []
        for h in range(ATTN_HEADS):
            tiles.append((h, i, bias_ref.at[h, 0], None))
            tiles.append((h, jp, bias_ref.at[h, 1], sel_ref[h, pl.ds(jp, 1), :] > 0.5))
        last_far = jnp.maximum(n_far - 1, 0)

        def prefetch_ids(t):
            return [(h, jnp.minimum(j, last_far)) for h, j in far_tile_ids(t)[:QK_LOOKAHEAD]]

        parts = run_tiles(tiles, carried_in=False, prefetch=prefetch_ids(0))
        for h in range(ATTN_HEADS):
            merge(h, parts[2 * h:2 * h + 2], first=True)

        def far_step(t, _):
            tiles = []
            for h, j in far_tile_ids(t):
                jc = jnp.minimum(j, last_far)
                sel = (sel_ref[h, pl.ds(jc, 1), :] > 0.5) & (j < n_far)
                tiles.append((h, jc, None, sel))
            parts = run_tiles(tiles, carried_in=True, prefetch=prefetch_ids(t + 1))
            for h in range(ATTN_HEADS):
                merge(h, parts[h * FAR_TILES_PER_STEP:(h + 1) * FAR_TILES_PER_STEP], first=False)
            return 0

        lax.fori_loop(0, (n_far + FAR_TILES_PER_STEP - 1) // FAR_TILES_PER_STEP, far_step, 0)

        for h in range(ATTN_HEADS):
            o_ref[h * HEAD_DIM:(h + 1) * HEAD_DIM, pl.ds(q0, MOBA_BLOCK)] = (
                acc_ref[h] / ml_ref[h, 1:2, :]).astype(o_ref.dtype)
        return 0

    lax.fori_loop(0, n_blocks, q_block, 0)


def _moba_attention(qt, k, vt, kmean, bias):
    b, _, s = qt.shape
    n_blocks = s // MOBA_BLOCK
    single = dict(pipeline_mode=pl.Buffered(1))
    t_spec = pl.BlockSpec((None, ATTN_WIDTH, s), lambda bi, hi: (bi, hi, 0), **single)
    return pl.pallas_call(
        _attn_kernel,
        out_shape=jax.ShapeDtypeStruct((b, D_MODEL, s), BF16),
        grid=(b, N_HEADS // ATTN_HEADS),
        in_specs=[t_spec,
                  pl.BlockSpec((None, s, ATTN_WIDTH), lambda bi, hi: (bi, 0, hi), **single),
                  t_spec,
                  pl.BlockSpec((None, n_blocks, ATTN_WIDTH), lambda bi, hi: (bi, 0, hi)),
                  pl.BlockSpec((ATTN_HEADS, 2, MOBA_BLOCK, MOBA_BLOCK), lambda bi, hi: (hi, 0, 0, 0), **single)],
        out_specs=pl.BlockSpec((None, ATTN_WIDTH, s), lambda bi, hi: (bi, hi, 0)),
        scratch_shapes=[pltpu.VMEM((ATTN_HEADS, n_blocks, MOBA_BLOCK), F32),
                        pltpu.VMEM((ATTN_HEADS, 2, MOBA_BLOCK), F32),
                        pltpu.VMEM((ATTN_HEADS, HEAD_DIM, MOBA_BLOCK), F32),
                        pltpu.VMEM((ATTN_HEADS, V7X_LANES, MOBA_BLOCK), BF16),
                        pltpu.VMEM((QK_LOOKAHEAD, MOBA_BLOCK, MOBA_BLOCK), F32)],
        compiler_params=_params("arbitrary", "arbitrary"),
        name="moba_attention",
    )(qt, k, vt, kmean, bias)


OPROJ_TM = 512


def _oproj_kernel(at_ref, w_ref, x_ref, lng_ref, lnb_ref, o_ref):
    h = lax.dot_general(at_ref[...], w_ref[...], _TN, preferred_element_type=F32)
    o_ref[...] = _layer_norm(DEEPNORM_ALPHA * x_ref[...] + h, lng_ref[...], lnb_ref[...])


def _out_project(a_t, w_o, x, ln_g, ln_b):
    b, s, _ = x.shape
    row_spec = pl.BlockSpec((None, OPROJ_TM, D_MODEL), lambda bi, i: (bi, i, 0))
    return pl.pallas_call(
        _oproj_kernel,
        out_shape=jax.ShapeDtypeStruct((b, s, D_MODEL), F32),
        grid=(b, s // OPROJ_TM),
        in_specs=[pl.BlockSpec((None, D_MODEL, OPROJ_TM), lambda bi, i: (bi, 0, i)),
                  _const_spec(w_o.shape), row_spec,
                  _const_spec(ln_g.shape), _const_spec(ln_b.shape)],
        out_specs=row_spec,
        compiler_params=_params("arbitrary", "arbitrary"),
        name="out_project",
    )(a_t, w_o, x, ln_g, ln_b)


def kernel(x, ln_g, ln_b, a_w_in, a_v_g, a_v_b, a_w_s, a_b_s, a_w_out, w_k, w_v, b_w_q, b_w_o,
           rel_bias, w_router, b_router, e_w1, e_w3, e_w2):
    batch, seq, _ = x.shape
    t = batch * seq
    n_blocks = seq // MOBA_BLOCK
    xt = x.reshape(t, D_MODEL)
    wr_t = w_router.T.astype(BF16)
    br = b_router.reshape(N_EXPERTS, 1)
    n_tiles = t // MOE_TM + MOE_CLASSES
    sorted_buf = jnp.zeros((n_tiles * MOE_TM * FEATURE_ROWS, V7X_LANES), F32)

    def ln_rows(l, j):
        return ln_g[l, j].reshape(1, D_MODEL), ln_b[l, j].reshape(1, D_MODEL)

    k = vt = kmean = bias = None
    for l in range(DEPTH):
        g0, b0 = ln_rows(l, 0)
        if l < N_A_LAYERS:
            xt = _gmlp_layer(xt, a_w_in[l].astype(BF16),
                             a_v_g[l].reshape(1, GMLP_HALF), a_v_b[l].reshape(1, GMLP_HALF),
                             a_w_s[l], a_b_s[l].T, a_w_out[l].astype(BF16), g0, b0)
        else:
            x3 = xt.reshape(batch, seq, D_MODEL)
            if l == N_A_LAYERS:
                k, vt, kmean = _project_kv(x3, w_k.astype(BF16), w_v.T.astype(BF16))
                kmean = kmean.reshape(batch, n_blocks, D_MODEL)
                bias = _bias_tables(rel_bias)
            j = l - N_A_LAYERS
            qt = _project_t(x3, b_w_q[j].T.astype(BF16), HEAD_DIM ** -0.5 * LOG2_E)
            ot = _moba_attention(qt, k, vt, kmean, bias)
            xt = _out_project(ot, b_w_o[j].astype(BF16), x3, g0, b0).reshape(t, D_MODEL)
        g1, b1 = ln_rows(l, 1)
        xt, sorted_buf = _moe_layer(xt, sorted_buf, wr_t, br, e_w1, e_w3, e_w2, g1, b1, layer=l)
    return xt.reshape(batch, seq, D_MODEL)
```

```python
import functools
import math

import numpy as np
import jax
import jax.numpy as jnp
from jax import lax
from jax.experimental import pallas as pl
from jax.experimental.pallas import tpu as pltpu

D_MODEL = 1024
DEPTH = 4
N_A_LAYERS = DEPTH // 2
GMLP_HALF = 3 * D_MODEL
GMLP_GROUPS = 8
GMLP_GROUP_WIDTH = GMLP_HALF // GMLP_GROUPS
GMLP_CHUNK = 128
N_HEADS = 16
HEAD_DIM = D_MODEL // N_HEADS
MOBA_BLOCK = 256
MOBA_TOPK = 3
N_BUCKETS = 32
MAX_DISTANCE = 128
N_EXPERTS = 16
N_GROUPS = 4
EXPERTS_PER_GROUP = N_EXPERTS // N_GROUPS
TOP_K = 2
MOE_PAIRS = EXPERTS_PER_GROUP * (EXPERTS_PER_GROUP - 1) // 2
MOE_CLASSES = N_GROUPS * MOE_PAIRS
ROUTER_ROWS = 8
D_EXPERT = D_MODEL // 4
DEEPNORM_ALPHA = (2 * DEPTH) ** 0.25
LN_EPS = 1e-5
NEG_INF = -1e30

V7X_VMEM_LIMIT_BYTES = 56 * 1024 * 1024
V7X_LANES = 128
BF16_SUBLANE_TILE = 16

BF16 = jnp.bfloat16
F32 = jnp.float32

_NT = (((1,), (1,)), ((), ()))
_TN = (((0,), (0,)), ((), ()))


def _params(*semantics):
    return pltpu.CompilerParams(dimension_semantics=semantics,
                                vmem_limit_bytes=V7X_VMEM_LIMIT_BYTES)


def _const_spec(shape):
    zeros = (0,) * len(shape)
    return pl.BlockSpec(shape, lambda *_: zeros, pipeline_mode=pl.Buffered(1))


def _layer_norm(y, g, b):
    mu = jnp.mean(y, axis=-1, keepdims=True)
    d = y - mu
    var = jnp.mean(d * d, axis=-1, keepdims=True)
    return d * lax.rsqrt(var + LN_EPS) * g + b


GMLP_TM = 512
GMLP_GROUPS_PER_STEP = 2


def _gmlp_kernel(x_ref, w_in_ref, vg_ref, vb_ref, ws_ref, bs_ref, w_out_ref,
                 lng_ref, lnb_ref, o_ref, vn_ref, sv_ref):
    x = x_ref[...]
    xb = x.astype(BF16)
    v = jax.nn.gelu(jnp.dot(xb, w_in_ref[:, GMLP_HALF:], preferred_element_type=F32))
    vn_ref[...] = _layer_norm(v, vg_ref[...], vb_ref[...]).astype(BF16)

    row = lax.broadcasted_iota(jnp.int32, (GMLP_CHUNK, GMLP_CHUNK), 0)
    col = lax.broadcasted_iota(jnp.int32, (GMLP_CHUNK, GMLP_CHUNK), 1)
    causal = row >= col
    width = GMLP_GROUPS_PER_STEP * GMLP_GROUP_WIDTH
    acc = jnp.zeros((GMLP_TM, D_MODEL), F32)
    for step in range(GMLP_GROUPS // GMLP_GROUPS_PER_STEP):
        lo = step * width
        u = jax.nn.gelu(jnp.dot(xb, w_in_ref[:, lo:lo + width], preferred_element_type=F32))
        for gi in range(GMLP_GROUPS_PER_STEP):
            g = step * GMLP_GROUPS_PER_STEP + gi
            w_s = jnp.where(causal, ws_ref[g], 0.0).astype(BF16)
            b_s = bs_ref[:, g:g + 1]
            glo = g * GMLP_GROUP_WIDTH
            for c in range(GMLP_TM // GMLP_CHUNK):
                rows = slice(c * GMLP_CHUNK, (c + 1) * GMLP_CHUNK)
                vc = vn_ref[rows, glo:glo + GMLP_GROUP_WIDTH]
                sv_ref[rows, gi * GMLP_GROUP_WIDTH:(gi + 1) * GMLP_GROUP_WIDTH] = (
                    jnp.dot(w_s, vc, preferred_element_type=F32) + b_s)
        gated = (u * sv_ref[...]).astype(BF16)
        acc = acc + jnp.dot(gated, w_out_ref[lo:lo + width, :], preferred_element_type=F32)
    o_ref[...] = _layer_norm(DEEPNORM_ALPHA * x + acc, lng_ref[...], lnb_ref[...])


def _gmlp_layer(x, w_in, v_g, v_b, w_s, b_s_t, w_out, ln_g, ln_b):
    t = x.shape[0]
    row_spec = pl.BlockSpec((GMLP_TM, D_MODEL), lambda i: (i, 0))
    return pl.pallas_call(
        _gmlp_kernel,
        out_shape=jax.ShapeDtypeStruct((t, D_MODEL), F32),
        grid=(t // GMLP_TM,),
        in_specs=[row_spec,
                  _const_spec(w_in.shape), _const_spec(v_g.shape), _const_spec(v_b.shape),
                  _const_spec(w_s.shape), _const_spec(b_s_t.shape), _const_spec(w_out.shape),
                  _const_spec(ln_g.shape), _const_spec(ln_b.shape)],
        out_specs=row_spec,
        scratch_shapes=[pltpu.VMEM((GMLP_TM, GMLP_HALF), BF16),
                        pltpu.VMEM((GMLP_TM, GMLP_GROUPS_PER_STEP * GMLP_GROUP_WIDTH), F32)],
        compiler_params=_params("arbitrary"),
        name="gmlp_layer",
    )(x, w_in, v_g, v_b, w_s, b_s_t, w_out, ln_g, ln_b)


ROUTER_TM = 2048


def _router_kernel(x_ref, wr_ref, br_ref, o_ref):
    xb = x_ref[...].astype(BF16)
    logits = lax.dot_general(wr_ref[...], xb, _NT, preferred_element_type=F32) + br_ref[...]
    m = jnp.max(logits, axis=0, keepdims=True)
    ex = jnp.exp(logits - m)
    probs = ex / jnp.sum(ex, axis=0, keepdims=True)
    p = [probs[e:e + 1, :] for e in range(N_EXPERTS)]

    scores = []
    for g in range(N_GROUPS):
        q = p[g * EXPERTS_PER_GROUP:(g + 1) * EXPERTS_PER_GROUP]
        best = None
        for a in range(EXPERTS_PER_GROUP):
            for b in range(a + 1, EXPERTS_PER_GROUP):
                s = q[a] + q[b]
                best = s if best is None else jnp.maximum(best, s)
        scores.append(best)
    best_score = scores[0]
    gsel = jnp.zeros_like(best_score, dtype=jnp.int32)
    for g in range(1, N_GROUPS):
        better = scores[g] > best_score
        best_score = jnp.where(better, scores[g], best_score)
        gsel = jnp.where(better, g, gsel)

    slot_chosen = [None] * EXPERTS_PER_GROUP
    slot_gate = [None] * EXPERTS_PER_GROUP
    for e in range(N_EXPERTS):
        g, k = divmod(e, EXPERTS_PER_GROUP)
        rank = jnp.zeros_like(gsel)
        for e2 in range(g * EXPERTS_PER_GROUP, (g + 1) * EXPERTS_PER_GROUP):
            if e2 == e:
                continue
            ahead = (p[e2] > p[e]) | ((p[e2] == p[e]) & (e2 < e))
            rank = rank + ahead.astype(jnp.int32)
        chosen = (gsel == g) & (rank < TOP_K)
        gate = jnp.where(chosen, p[e] / best_score, 0.0)
        slot_chosen[k] = chosen if slot_chosen[k] is None else slot_chosen[k] | chosen
        slot_gate[k] = gate if slot_gate[k] is None else slot_gate[k] + gate
    c0, c1, c2, c3 = slot_chosen
    lo = jnp.where(c0, 0, jnp.where(c1, 1, 2))
    hi = jnp.where(c3, 3, jnp.where(c2, 2, 1))
    w_lo = jnp.where(c0, slot_gate[0], jnp.where(c1, slot_gate[1], slot_gate[2]))
    w_hi = jnp.where(c3, slot_gate[3], jnp.where(c2, slot_gate[2], slot_gate[1]))
    pair = jnp.right_shift(lo * (7 - lo), 1) + hi - lo - 1
    cls = gsel * MOE_PAIRS + pair
    zero = jnp.zeros_like(w_lo)
    o_ref[...] = jnp.concatenate([cls.astype(F32), w_lo, w_hi] + [zero] * (ROUTER_ROWS - 3), axis=0)


def _router(x, wr_t, br):
    t = x.shape[0]
    return pl.pallas_call(
        _router_kernel,
        out_shape=jax.ShapeDtypeStruct((ROUTER_ROWS, t), F32),
        grid=(t // ROUTER_TM,),
        in_specs=[pl.BlockSpec((ROUTER_TM, D_MODEL), lambda i: (i, 0)),
                  _const_spec(wr_t.shape), _const_spec(br.shape)],
        out_specs=pl.BlockSpec((ROUTER_ROWS, ROUTER_TM), lambda i: (0, i)),
        compiler_params=_params("arbitrary"),
        name="moe_router",
    )(x, wr_t, br)


MOE_TM = 256
FEATURE_ROWS = D_MODEL // V7X_LANES
DISPATCH_TD = 256
DMA_UNROLL = 8


def _token_rows(ref, n_tokens):
    return jnp.concatenate(
        [ref[pl.ds(c, n_tokens, stride=FEATURE_ROWS), :] for c in range(FEATURE_ROWS)], axis=1)


def _store_token_rows(ref, value, n_tokens):
    for c in range(FEATURE_ROWS):
        ref[pl.ds(c, n_tokens, stride=FEATURE_ROWS), :] = value[:, c * V7X_LANES:(c + 1) * V7X_LANES]


def _dispatch_kernel(pos_ref, x_ref, buf_in_ref, buf_ref, stage_ref, sems):
    del buf_in_ref
    i = pl.program_id(0)
    n = pl.num_programs(0)
    slot = i % 2
    stage = stage_ref.at[slot]
    rows = DISPATCH_TD * FEATURE_ROWS

    def wait_slot(s):
        pltpu.make_async_copy(stage_ref.at[s], buf_ref.at[pl.ds(0, rows)], sems.at[s]).wait()

    @pl.when(i >= 2)
    def _():
        wait_slot(slot)

    _store_token_rows(stage, x_ref[...], DISPATCH_TD)

    def issue(r, _):
        dst = pl.multiple_of(pos_ref[i * DISPATCH_TD + r] * FEATURE_ROWS, FEATURE_ROWS)
        src = pl.multiple_of(r * FEATURE_ROWS, FEATURE_ROWS)
        pltpu.make_async_copy(stage.at[pl.ds(src, FEATURE_ROWS)], buf_ref.at[pl.ds(dst, FEATURE_ROWS)],
                              sems.at[slot]).start()
        return 0

    lax.fori_loop(0, DISPATCH_TD, issue, 0, unroll=DMA_UNROLL)

    @pl.when(i == n - 1)
    def _():
        wait_slot(slot)

        @pl.when(n > 1)
        def _():
            wait_slot(1 - slot)


def _dispatch(pos, x, buf):
    t = x.shape[0]
    return pl.pallas_call(
        _dispatch_kernel,
        out_shape=jax.ShapeDtypeStruct(buf.shape, buf.dtype),
        grid_spec=pltpu.PrefetchScalarGridSpec(
            num_scalar_prefetch=1,
            grid=(t // DISPATCH_TD,),
            in_specs=[pl.BlockSpec((DISPATCH_TD, D_MODEL), lambda i, pos: (i, 0)),
                      pl.BlockSpec(memory_space=pl.ANY)],
            out_specs=pl.BlockSpec(memory_space=pl.ANY),
            scratch_shapes=[pltpu.VMEM((2, DISPATCH_TD * FEATURE_ROWS, V7X_LANES), F32),
                            pltpu.SemaphoreType.DMA((2,))]),
        input_output_aliases={2: 0},
        compiler_params=_params("arbitrary"),
        name="moe_dispatch",
    )(pos, x, buf)


def _expert_kernel(ea_ref, eb_ref, fresh_ref, used_ref, xs_ref, gate_ref,
                   w1a_ref, w1b_ref, w3a_ref, w3b_ref, w2a_ref, w2b_ref, ys_ref,
                   w1_s, w3_s, w2_s):
    del ea_ref, eb_ref
    i = pl.program_id(0)

    @pl.when(fresh_ref[i] == 1)
    def _():
        for e, (w1, w3, w2) in enumerate(((w1a_ref, w3a_ref, w2a_ref), (w1b_ref, w3b_ref, w2b_ref))):
            w1_s[e] = w1[...].astype(BF16)
            w3_s[e] = w3[...].astype(BF16)
            w2_s[e] = w2[...].astype(BF16)

    @pl.when(used_ref[i] == 1)
    def _():
        xb = _token_rows(xs_ref, MOE_TM).astype(BF16)
        gates = gate_ref[...]
        y = None
        for e in range(TOP_K):
            h1 = jnp.dot(xb, w1_s[e], preferred_element_type=F32)
            h3 = jnp.dot(xb, w3_s[e], preferred_element_type=F32)
            h = (jax.nn.silu(h1) * h3 * gates[:, e:e + 1]).astype(BF16)
            ye = jnp.dot(h, w2_s[e], preferred_element_type=F32)
            y = ye if y is None else y + ye
        _store_token_rows(ys_ref, y, MOE_TM)

    @pl.when(used_ref[i] == 0)
    def _():
        ys_ref[...] = jnp.zeros_like(ys_ref)


def _expert_ffn(ea, eb, fresh, used, xs, gates, w1, w3, w2, layer):
    n_tiles = ea.shape[0]

    def w_spec(shape, which):
        return pl.BlockSpec((None, None) + shape,
                            lambda i, ea, eb, fresh, used: (layer, (ea, eb)[which][i], 0, 0))

    up, down = (D_MODEL, D_EXPERT), (D_EXPERT, D_MODEL)
    return pl.pallas_call(
        _expert_kernel,
        out_shape=jax.ShapeDtypeStruct((n_tiles * MOE_TM * FEATURE_ROWS, V7X_LANES), F32),
        grid_spec=pltpu.PrefetchScalarGridSpec(
            num_scalar_prefetch=4,
            grid=(n_tiles,),
            in_specs=[pl.BlockSpec((MOE_TM * FEATURE_ROWS, V7X_LANES), lambda i, *_: (i, 0)),
                      pl.BlockSpec((MOE_TM, TOP_K), lambda i, *_: (i, 0)),
                      w_spec(up, 0), w_spec(up, 1), w_spec(up, 0), w_spec(up, 1),
                      w_spec(down, 0), w_spec(down, 1)],
            out_specs=pl.BlockSpec((MOE_TM * FEATURE_ROWS, V7X_LANES), lambda i, *_: (i, 0)),
            scratch_shapes=[pltpu.VMEM((TOP_K,) + up, BF16), pltpu.VMEM((TOP_K,) + up, BF16),
                            pltpu.VMEM((TOP_K,) + down, BF16)]),
        compiler_params=_params("arbitrary"),
        name="moe_experts",
    )(ea, eb, fresh, used, xs, gates, w1, w1, w3, w3, w2, w2)


def _combine_kernel(pos_ref, ys_ref, x_ref, lng_ref, lnb_ref, o_ref, stage_ref, sems):
    i = pl.program_id(0)
    n = pl.num_programs(0)
    slot = i % 2
    rows = DISPATCH_TD * FEATURE_ROWS

    def issue_step(step, s):
        def issue(r, _):
            src = pl.multiple_of(pos_ref[step * DISPATCH_TD + r] * FEATURE_ROWS, FEATURE_ROWS)
            dst = pl.multiple_of(r * FEATURE_ROWS, FEATURE_ROWS)
            pltpu.make_async_copy(ys_ref.at[pl.ds(src, FEATURE_ROWS)],
                                  stage_ref.at[s, pl.ds(dst, FEATURE_ROWS)], sems.at[s]).start()
            return 0
        lax.fori_loop(0, DISPATCH_TD, issue, 0, unroll=DMA_UNROLL)

    @pl.when(i == 0)
    def _():
        issue_step(0, 0)

    @pl.when(i + 1 < n)
    def _():
        issue_step(i + 1, 1 - slot)

    pltpu.make_async_copy(ys_ref.at[pl.ds(0, rows)], stage_ref.at[slot], sems.at[slot]).wait()

    f = _token_rows(stage_ref.at[slot], DISPATCH_TD)
    o_ref[...] = _layer_norm(DEEPNORM_ALPHA * x_ref[...] + f, lng_ref[...], lnb_ref[...])


def _combine(pos, ys, x, ln_g, ln_b):
    t = x.shape[0]
    row_spec = pl.BlockSpec((DISPATCH_TD, D_MODEL), lambda i, pos: (i, 0))
    return pl.pallas_call(
        _combine_kernel,
        out_shape=jax.ShapeDtypeStruct((t, D_MODEL), F32),
        grid_spec=pltpu.PrefetchScalarGridSpec(
            num_scalar_prefetch=1,
            grid=(t // DISPATCH_TD,),
            in_specs=[pl.BlockSpec(memory_space=pl.ANY), row_spec,
                      pl.BlockSpec((1, D_MODEL), lambda i, pos: (0, 0)),
                      pl.BlockSpec((1, D_MODEL), lambda i, pos: (0, 0))],
            out_specs=row_spec,
            scratch_shapes=[pltpu.VMEM((2, DISPATCH_TD * FEATURE_ROWS, V7X_LANES), F32),
                            pltpu.SemaphoreType.DMA((2,))]),
        compiler_params=_params("arbitrary"),
        name="moe_combine",
    )(pos, ys, x, ln_g, ln_b)


def _pair_table():
    lo, hi = [], []
    for g in range(N_GROUPS):
        for a in range(EXPERTS_PER_GROUP):
            for b in range(a + 1, EXPERTS_PER_GROUP):
                lo.append(g * EXPERTS_PER_GROUP + a)
                hi.append(g * EXPERTS_PER_GROUP + b)
    return np.asarray(lo, np.int32), np.asarray(hi, np.int32)


def _sort_plan(cls, n_tiles):
    onehot = (cls[:, None] == jnp.arange(MOE_CLASSES, dtype=jnp.int32)[None, :]).astype(jnp.int32)
    running = jnp.cumsum(onehot, axis=0)
    counts = running[-1]
    padded = (counts + MOE_TM - 1) // MOE_TM * MOE_TM
    ends = jnp.cumsum(padded)
    rank = jnp.sum(running * onehot, axis=1) - 1
    pos = jnp.sum(onehot * (ends - padded)[None, :], axis=1) + rank
    tile_cls = jnp.searchsorted(ends, jnp.arange(n_tiles, dtype=jnp.int32) * MOE_TM, side="right")
    used = (tile_cls < MOE_CLASSES).astype(jnp.int32)
    tile_cls = jnp.minimum(tile_cls, MOE_CLASSES - 1).astype(jnp.int32)
    fresh = jnp.concatenate([jnp.ones((1,), jnp.int32),
                             (tile_cls[1:] != tile_cls[:-1]).astype(jnp.int32)])
    lo, hi = _pair_table()
    return pos.astype(jnp.int32), jnp.asarray(lo)[tile_cls], jnp.asarray(hi)[tile_cls], fresh, used


def _moe_layer(x, buf, wr_t, br, w1, w3, w2, ln_g, ln_b, layer):
    t = x.shape[0]
    n_tiles = buf.shape[0] // (MOE_TM * FEATURE_ROWS)
    routed = _router(x, wr_t, br)
    pos, ea, eb, fresh, used = _sort_plan(routed[0].astype(jnp.int32), n_tiles)
    gates = jnp.zeros((n_tiles * MOE_TM, TOP_K), F32).at[pos].set(
        routed[1:1 + TOP_K].T, unique_indices=True, mode="promise_in_bounds")
    buf = _dispatch(pos, x, buf)
    ys = _expert_ffn(ea, eb, fresh, used, buf, gates, w1, w3, w2, layer)
    return _combine(pos, ys, x, ln_g, ln_b), buf


PROJ_TM = 512


def _proj_t_kernel(x_ref, wt_ref, o_ref, *, scale):
    acc = lax.dot_general(wt_ref[...], x_ref[...].astype(BF16), _NT, preferred_element_type=F32)
    o_ref[...] = (acc * scale).astype(o_ref.dtype)


def _project_t(x, w_t, scale):
    b, s, _ = x.shape
    n = w_t.shape[0]
    return pl.pallas_call(
        functools.partial(_proj_t_kernel, scale=scale),
        out_shape=jax.ShapeDtypeStruct((b, n, s), BF16),
        grid=(b, s // PROJ_TM),
        in_specs=[pl.BlockSpec((None, PROJ_TM, D_MODEL), lambda bi, i: (bi, i, 0)),
                  _const_spec(w_t.shape)],
        out_specs=pl.BlockSpec((None, n, PROJ_TM), lambda bi, i: (bi, 0, i)),
        compiler_params=_params("arbitrary", "arbitrary"),
        name="project_t",
    )(x, w_t)


def _kv_kernel(x_ref, wk_ref, wvt_ref, k_ref, vt_ref, kmean_ref):
    xb = x_ref[...].astype(BF16)
    k = jnp.dot(xb, wk_ref[...], preferred_element_type=F32)
    k_ref[...] = k.astype(BF16)
    for j in range(PROJ_TM // MOBA_BLOCK):
        kmean_ref[j] = jnp.mean(k[j * MOBA_BLOCK:(j + 1) * MOBA_BLOCK], axis=0, keepdims=True)
    vt_ref[...] = lax.dot_general(wvt_ref[...], xb, _NT, preferred_element_type=F32).astype(BF16)


def _project_kv(x, w_k, w_v_t):
    b, s, _ = x.shape
    blocks_per_tile = PROJ_TM // MOBA_BLOCK
    tiles = s // PROJ_TM
    return pl.pallas_call(
        _kv_kernel,
        out_shape=(jax.ShapeDtypeStruct((b, s, D_MODEL), BF16),
                   jax.ShapeDtypeStruct((b, D_MODEL, s), BF16),
                   jax.ShapeDtypeStruct((b * s // MOBA_BLOCK, 1, D_MODEL), F32)),
        grid=(b, tiles),
        in_specs=[pl.BlockSpec((None, PROJ_TM, D_MODEL), lambda bi, i: (bi, i, 0)),
                  _const_spec(w_k.shape), _const_spec(w_v_t.shape)],
        out_specs=(pl.BlockSpec((None, PROJ_TM, D_MODEL), lambda bi, i: (bi, i, 0)),
                   pl.BlockSpec((None, D_MODEL, PROJ_TM), lambda bi, i: (bi, 0, i)),
                   pl.BlockSpec((blocks_per_tile, 1, D_MODEL), lambda bi, i: (bi * tiles + i, 0, 0))),
        compiler_params=_params("arbitrary", "arbitrary"),
        name="project_kv",
    )(x, w_k, w_v_t)


def _t5_bucket_np(dist):
    n = np.maximum(dist, 0)
    max_exact = N_BUCKETS // 2
    nf = np.maximum(n, 1).astype(np.float32)
    large = max_exact + (np.log(nf / np.float32(max_exact)) / np.float32(math.log(MAX_DISTANCE / max_exact))
                         * np.float32(N_BUCKETS - max_exact)).astype(np.int32)
    large = np.minimum(large, N_BUCKETS - 1)
    return np.where(n < max_exact, n, large).astype(np.int32)


def _bucket_tables():
    kpos = np.arange(MOBA_BLOCK)[:, None]
    qpos = np.arange(MOBA_BLOCK)[None, :]
    own = np.where(qpos >= kpos, _t5_bucket_np(qpos - kpos), -1)
    prev = _t5_bucket_np(qpos + MOBA_BLOCK - kpos)
    return np.stack([own, prev]).astype(np.int32)


def _bias_kernel(rel_ref, bucket_ref, o_ref):
    h = pl.program_id(0)
    far = rel_ref[N_BUCKETS - 1, h]
    for t in range(2):
        bucket = bucket_ref[t]
        acc = jnp.zeros(bucket.shape, F32)
        for b in range(N_BUCKETS):
            acc = jnp.where(bucket == b, rel_ref[b, h], acc)
        o_ref[t] = jnp.where(bucket < 0, NEG_INF, (acc - far) * LOG2_E)


def _bias_tables(rel_bias):
    buckets = jnp.asarray(_bucket_tables())
    return pl.pallas_call(
        _bias_kernel,
        out_shape=jax.ShapeDtypeStruct((N_HEADS, 2, MOBA_BLOCK, MOBA_BLOCK), F32),
        grid=(N_HEADS,),
        in_specs=[pl.BlockSpec(memory_space=pltpu.SMEM), _const_spec(buckets.shape)],
        out_specs=pl.BlockSpec((None, 2, MOBA_BLOCK, MOBA_BLOCK), lambda h: (h, 0, 0, 0)),
        compiler_params=_params("arbitrary"),
        name="rel_bias_tables",
    )(rel_bias, buckets)


ATTN_HEADS = 8
ATTN_WIDTH = ATTN_HEADS * HEAD_DIM
HEADS_PER_LANE_TILE = V7X_LANES // HEAD_DIM
FAR_TILES_PER_STEP = 2
QK_LOOKAHEAD = 6
LOG2_E = math.log2(math.e)


def _attn_kernel(qt_ref, k_ref, vt_ref, kmean_ref, bias_ref, o_ref,
                 sel_ref, ml_ref, acc_ref, qp_ref, sc_ref):
    n_blocks = kmean_ref.shape[0]
    blk = lax.broadcasted_iota(jnp.int32, (n_blocks, MOBA_BLOCK), 0)
    causal = (lax.broadcasted_iota(jnp.int32, (MOBA_BLOCK, MOBA_BLOCK), 0)
              <= lax.broadcasted_iota(jnp.int32, (MOBA_BLOCK, MOBA_BLOCK), 1))
    ones_rows = jnp.ones((BF16_SUBLANE_TILE, MOBA_BLOCK), BF16)

    def lane_slab(h):
        return slice((h // HEADS_PER_LANE_TILE) * V7X_LANES, (h // HEADS_PER_LANE_TILE + 1) * V7X_LANES)

    def padded_q(h, q0):
        qh = qt_ref[h * HEAD_DIM:(h + 1) * HEAD_DIM, pl.ds(q0, MOBA_BLOCK)]
        pieces = [jnp.zeros_like(qh)] * HEADS_PER_LANE_TILE
        pieces[h % HEADS_PER_LANE_TILE] = qh
        return jnp.concatenate(pieces, axis=0)

    def scores(h, j):
        k0 = pl.multiple_of(j * MOBA_BLOCK, MOBA_BLOCK)
        kj = k_ref[pl.ds(k0, MOBA_BLOCK), lane_slab(h)]
        return jnp.dot(kj, qp_ref[h], preferred_element_type=F32)

    def weights_and_values(h, j, s, bias, sel):
        if bias is not None:
            keep = causal if sel is None else sel
            s = jnp.where(keep, s, NEG_INF) + bias[...]
        m_loc = jnp.max(s, axis=0, keepdims=True)
        if sel is None:
            shift = m_loc
        else:
            shift = jnp.where(sel, m_loc, -NEG_INF)
            m_loc = jnp.where(sel, m_loc, NEG_INF)
        p = jnp.exp2(s - shift).astype(BF16)
        k0 = pl.multiple_of(j * MOBA_BLOCK, MOBA_BLOCK)
        vtj = vt_ref[h * HEAD_DIM:(h + 1) * HEAD_DIM, pl.ds(k0, MOBA_BLOCK)]
        pv = jnp.dot(jnp.concatenate([vtj, ones_rows], axis=0), p, preferred_element_type=F32)
        return m_loc, pv[HEAD_DIM:HEAD_DIM + 1], pv[:HEAD_DIM]

    def run_tiles(tiles, carried_in, prefetch):
        n = len(tiles)
        s = [None] * n
        out = [None] * n
        for t in range(QK_LOOKAHEAD if carried_in else 0, n + QK_LOOKAHEAD):
            if t < n:
                h, j, _, _ = tiles[t]
                s[t] = scores(h, j)
            else:
                h, j = prefetch[t - n]
                sc_ref[t - n] = scores(h, j)
            d = t - QK_LOOKAHEAD
            if d >= 0:
                h, j, bias, sel = tiles[d]
                sd = sc_ref[d] if carried_in and d < QK_LOOKAHEAD else s[d]
                out[d] = weights_and_values(h, j, sd, bias, sel)
                s[d] = None
        return out

    def far_tile_ids(t):
        return [(h, t * FAR_TILES_PER_STEP + u) for h in range(ATTN_HEADS) for u in range(FAR_TILES_PER_STEP)]

    def merge(h, parts, first):
        m_new = parts[0][0]
        for m_loc, _, _ in parts[1:]:
            m_new = jnp.maximum(m_new, m_loc)
        if first:
            l_new = jnp.zeros_like(m_new)
            acc_new = jnp.zeros((HEAD_DIM, MOBA_BLOCK), F32)
        else:
            m_old = ml_ref[h, 0:1, :]
            m_new = jnp.maximum(m_new, m_old)
            alpha = jnp.exp2(m_old - m_new)
            l_new = alpha * ml_ref[h, 1:2, :]
            acc_new = alpha * acc_ref[h]
        for m_loc, l_loc, pv in parts:
            w = jnp.exp2(m_loc - m_new)
            l_new = l_new + w * l_loc
            acc_new = acc_new + w * pv
        ml_ref[h, 0:1, :] = m_new
        ml_ref[h, 1:2, :] = l_new
        acc_ref[h] = acc_new

    def q_block(i, _):
        q0 = pl.multiple_of(i * MOBA_BLOCK, MOBA_BLOCK)
        jp = jnp.maximum(i - 1, 0)
        n_far = jnp.maximum(i - 1, 0)
        for h in range(ATTN_HEADS):
            qp_ref[h] = padded_q(h, q0)

        for h in range(ATTN_HEADS):
            gate = jnp.dot(kmean_ref[:, lane_slab(h)].astype(BF16), qp_ref[h], preferred_element_type=F32)
            gate = jnp.where(blk < i, gate, NEG_INF)
            sel = jnp.zeros(gate.shape, F32)
            for r in range(MOBA_TOPK):
                top = jnp.max(gate, axis=0, keepdims=True)
                first = jnp.min(jnp.where(gate == top, blk, n_blocks), axis=0, keepdims=True)
                hit = blk == first
                sel = jnp.where(hit & (r < i), 1.0, sel)
                gate = jnp.where(hit, -jnp.inf, gate)
            sel_ref[h] = sel

        tiles = []
        for h in range(ATTN_HEADS):
            tiles.append((h, i, bias_ref.at[h, 0], None))
            tiles.append((h, jp, bias_ref.at[h, 1], sel_ref[h, pl.ds(jp, 1), :] > 0.5))
        last_far = jnp.maximum(n_far - 1, 0)

        def prefetch_ids(t):
            return [(h, jnp.minimum(j, last_far)) for h, j in far_tile_ids(t)[:QK_LOOKAHEAD]]

        parts = run_tiles(tiles, carried_in=False, prefetch=prefetch_ids(0))
        for h in range(ATTN_HEADS):
            merge(h, parts[2 * h:2 * h + 2], first=True)

        def far_step(t, _):
            tiles = []
            for h, j in far_tile_ids(t):
                jc = jnp.minimum(j, last_far)
                sel = (sel_ref[h, pl.ds(jc, 1), :] > 0.5) & (j < n_far)
                tiles.append((h, jc, None, sel))
            parts = run_tiles(tiles, carried_in=True, prefetch=prefetch_ids(t + 1))
            for h in range(ATTN_HEADS):
                merge(h, parts[h * FAR_TILES_PER_STEP:(h + 1) * FAR_TILES_PER_STEP], first=False)
            return 0

        lax.fori_loop(0, (n_far + FAR_TILES_PER_STEP - 1) // FAR_TILES_PER_STEP, far_step, 0)

        for h in range(ATTN_HEADS):
            o_ref[h * HEAD_DIM:(h + 1) * HEAD_DIM, pl.ds(q0, MOBA_BLOCK)] = (
                acc_ref[h] / ml_ref[h, 1:2, :]).astype(o_ref.dtype)
        return 0

    lax.fori_loop(0, n_blocks, q_block, 0)


def _moba_attention(qt, k, vt, kmean, bias):
    b, _, s = qt.shape
    n_blocks = s // MOBA_BLOCK
    single = dict(pipeline_mode=pl.Buffered(1))
    t_spec = pl.BlockSpec((None, ATTN_WIDTH, s), lambda bi, hi: (bi, hi, 0), **single)
    return pl.pallas_call(
        _attn_kernel,
        out_shape=jax.ShapeDtypeStruct((b, D_MODEL, s), BF16),
        grid=(b, N_HEADS // ATTN_HEADS),
        in_specs=[t_spec,
                  pl.BlockSpec((None, s, ATTN_WIDTH), lambda bi, hi: (bi, 0, hi), **single),
                  t_spec,
                  pl.BlockSpec((None, n_blocks, ATTN_WIDTH), lambda bi, hi: (bi, 0, hi)),
                  pl.BlockSpec((ATTN_HEADS, 2, MOBA_BLOCK, MOBA_BLOCK), lambda bi, hi: (hi, 0, 0, 0), **single)],
        out_specs=pl.BlockSpec((None, ATTN_WIDTH, s), lambda bi, hi: (bi, hi, 0)),
        scratch_shapes=[pltpu.VMEM((ATTN_HEADS, n_blocks, MOBA_BLOCK), F32),
                        pltpu.VMEM((ATTN_HEADS, 2, MOBA_BLOCK), F32),
                        pltpu.VMEM((ATTN_HEADS, HEAD_DIM, MOBA_BLOCK), F32),
                        pltpu.VMEM((ATTN_HEADS, V7X_LANES, MOBA_BLOCK), BF16),
                        pltpu.VMEM((QK_LOOKAHEAD, MOBA_BLOCK, MOBA_BLOCK), F32)],
        compiler_params=_params("arbitrary", "arbitrary"),
        name="moba_attention",
    )(qt, k, vt, kmean, bias)


OPROJ_TM = 512


def _oproj_kernel(at_ref, w_ref, x_ref, lng_ref, lnb_ref, o_ref):
    h = lax.dot_general(at_ref[...], w_ref[...], _TN, preferred_element_type=F32)
    o_ref[...] = _layer_norm(DEEPNORM_ALPHA * x_ref[...] + h, lng_ref[...], lnb_ref[...])


def _out_project(a_t, w_o, x, ln_g, ln_b):
    b, s, _ = x.shape
    row_spec = pl.BlockSpec((None, OPROJ_TM, D_MODEL), lambda bi, i: (bi, i, 0))
    return pl.pallas_call(
        _oproj_kernel,
        out_shape=jax.ShapeDtypeStruct((b, s, D_MODEL), F32),
        grid=(b, s // OPROJ_TM),
        in_specs=[pl.BlockSpec((None, D_MODEL, OPROJ_TM), lambda bi, i: (bi, 0, i)),
                  _const_spec(w_o.shape), row_spec,
                  _const_spec(ln_g.shape), _const_spec(ln_b.shape)],
        out_specs=row_spec,
        compiler_params=_params("arbitrary", "arbitrary"),
        name="out_project",
    )(a_t, w_o, x, ln_g, ln_b)


def kernel(x, ln_g, ln_b, a_w_in, a_v_g, a_v_b, a_w_s, a_b_s, a_w_out, w_k, w_v, b_w_q, b_w_o,
           rel_bias, w_router, b_router, e_w1, e_w3, e_w2):
    batch, seq, _ = x.shape
    t = batch * seq
    n_blocks = seq // MOBA_BLOCK
    xt = x.reshape(t, D_MODEL)
    wr_t = w_router.T.astype(BF16)
    br = b_router.reshape(N_EXPERTS, 1)
    n_tiles = t // MOE_TM + MOE_CLASSES
    sorted_buf = jnp.zeros((n_tiles * MOE_TM * FEATURE_ROWS, V7X_LANES), F32)

    def ln_rows(l, j):
        return ln_g[l, j].reshape(1, D_MODEL), ln_b[l, j].reshape(1, D_MODEL)

    k = vt = kmean = bias = None
    for l in range(DEPTH):
        g0, b0 = ln_rows(l, 0)
        if l < N_A_LAYERS:
            xt = _gmlp_layer(xt, a_w_in[l].astype(BF16),
                             a_v_g[l].reshape(1, GMLP_HALF), a_v_b[l].reshape(1, GMLP_HALF),
                             a_w_s[l], a_b_s[l].T, a_w_out[l].astype(BF16), g0, b0)
        else:
            x3 = xt.reshape(batch, seq, D_MODEL)
            if l == N_A_LAYERS:
                k, vt, kmean = _project_kv(x3, w_k.astype(BF16), w_v.T.astype(BF16))
                kmean = kmean.reshape(batch, n_blocks, D_MODEL)
                bias = _bias_tables(rel_bias)
            j = l - N_A_LAYERS
            qt = _project_t(x3, b_w_q[j].T.astype(BF16), HEAD_DIM ** -0.5 * LOG2_E)
            ot = _moba_attention(qt, k, vt, kmean, bias)
            xt = _out_project(ot, b_w_o[j].astype(BF16), x3, g0, b0).reshape(t, D_MODEL)
        g1, b1 = ln_rows(l, 1)
        xt, sorted_buf = _moe_layer(xt, sorted_buf, wr_t, br, e_w1, e_w3, e_w2, g1, b1, layer=l)
    return xt.reshape(batch, seq, D_MODEL)
```

```python
import functools
import math

import numpy as np
import jax
import jax.numpy as jnp
from jax import lax
from jax.experimental import pallas as pl
from jax.experimental.pallas import tpu as pltpu

D_MODEL = 1024
DEPTH = 4
N_A_LAYERS = DEPTH // 2
GMLP_HALF = 3 * D_MODEL
GMLP_GROUPS = 8
GMLP_GROUP_WIDTH = GMLP_HALF // GMLP_GROUPS
GMLP_CHUNK = 128
N_HEADS = 16
HEAD_DIM = D_MODEL // N_HEADS
MOBA_BLOCK = 256
MOBA_TOPK = 3
N_BUCKETS = 32
MAX_DISTANCE = 128
N_EXPERTS = 16
N_GROUPS = 4
EXPERTS_PER_GROUP = N_EXPERTS // N_GROUPS
TOP_K = 2
MOE_PAIRS = EXPERTS_PER_GROUP * (EXPERTS_PER_GROUP - 1) // 2
MOE_CLASSES = N_GROUPS * MOE_PAIRS
ROUTER_ROWS = 8
D_EXPERT = D_MODEL // 4
DEEPNORM_ALPHA = (2 * DEPTH) ** 0.25
LN_EPS = 1e-5
NEG_INF = -1e30

V7X_VMEM_LIMIT_BYTES = 56 * 1024 * 1024
V7X_LANES = 128
BF16_SUBLANE_TILE = 16

BF16 = jnp.bfloat16
F32 = jnp.float32

_NT = (((1,), (1,)), ((), ()))


def _params(*semantics):
    return pltpu.CompilerParams(dimension_semantics=semantics,
                                vmem_limit_bytes=V7X_VMEM_LIMIT_BYTES)


def _const_spec(shape):
    zeros = (0,) * len(shape)
    return pl.BlockSpec(shape, lambda *_: zeros, pipeline_mode=pl.Buffered(1))


def _layer_norm(y, g, b):
    mu = jnp.mean(y, axis=-1, keepdims=True)
    d = y - mu
    var = jnp.mean(d * d, axis=-1, keepdims=True)
    return d * lax.rsqrt(var + LN_EPS) * g + b


GMLP_TM = 512
GMLP_GROUPS_PER_STEP = 2


def _gmlp_kernel(x_ref, w_in_ref, vg_ref, vb_ref, ws_ref, bs_ref, w_out_ref,
                 lng_ref, lnb_ref, o_ref, vn_ref, sv_ref):
    x = x_ref[...]
    xb = x.astype(BF16)
    v = jax.nn.gelu(jnp.dot(xb, w_in_ref[:, GMLP_HALF:], preferred_element_type=F32))
    vn_ref[...] = _layer_norm(v, vg_ref[...], vb_ref[...]).astype(BF16)

    row = lax.broadcasted_iota(jnp.int32, (GMLP_CHUNK, GMLP_CHUNK), 0)
    col = lax.broadcasted_iota(jnp.int32, (GMLP_CHUNK, GMLP_CHUNK), 1)
    causal = row >= col
    width = GMLP_GROUPS_PER_STEP * GMLP_GROUP_WIDTH
    acc = jnp.zeros((GMLP_TM, D_MODEL), F32)
    for step in range(GMLP_GROUPS // GMLP_GROUPS_PER_STEP):
        lo = step * width
        u = jax.nn.gelu(jnp.dot(xb, w_in_ref[:, lo:lo + width], preferred_element_type=F32))
        for gi in range(GMLP_GROUPS_PER_STEP):
            g = step * GMLP_GROUPS_PER_STEP + gi
            w_s = jnp.where(causal, ws_ref[g], 0.0).astype(BF16)
            b_s = bs_ref[:, g:g + 1]
            glo = g * GMLP_GROUP_WIDTH
            for c in range(GMLP_TM // GMLP_CHUNK):
                rows = slice(c * GMLP_CHUNK, (c + 1) * GMLP_CHUNK)
                vc = vn_ref[rows, glo:glo + GMLP_GROUP_WIDTH]
                sv_ref[rows, gi * GMLP_GROUP_WIDTH:(gi + 1) * GMLP_GROUP_WIDTH] = (
                    jnp.dot(w_s, vc, preferred_element_type=F32) + b_s)
        gated = (u * sv_ref[...]).astype(BF16)
        acc = acc + jnp.dot(gated, w_out_ref[lo:lo + width, :], preferred_element_type=F32)
    o_ref[...] = _layer_norm(DEEPNORM_ALPHA * x + acc, lng_ref[...], lnb_ref[...])


def _gmlp_layer(x, w_in, v_g, v_b, w_s, b_s_t, w_out, ln_g, ln_b):
    t = x.shape[0]
    row_spec = pl.BlockSpec((GMLP_TM, D_MODEL), lambda i: (i, 0))
    return pl.pallas_call(
        _gmlp_kernel,
        out_shape=jax.ShapeDtypeStruct((t, D_MODEL), F32),
        grid=(t // GMLP_TM,),
        in_specs=[row_spec,
                  _const_spec(w_in.shape), _const_spec(v_g.shape), _const_spec(v_b.shape),
                  _const_spec(w_s.shape), _const_spec(b_s_t.shape), _const_spec(w_out.shape),
                  _const_spec(ln_g.shape), _const_spec(ln_b.shape)],
        out_specs=row_spec,
        scratch_shapes=[pltpu.VMEM((GMLP_TM, GMLP_HALF), BF16),
                        pltpu.VMEM((GMLP_TM, GMLP_GROUPS_PER_STEP * GMLP_GROUP_WIDTH), F32)],
        compiler_params=_params("arbitrary"),
        name="gmlp_layer",
    )(x, w_in, v_g, v_b, w_s, b_s_t, w_out, ln_g, ln_b)


ROUTER_TM = 2048


def _router_kernel(x_ref, wr_ref, br_ref, o_ref):
    xb = x_ref[...].astype(BF16)
    logits = lax.dot_general(wr_ref[...], xb, _NT, preferred_element_type=F32) + br_ref[...]
    m = jnp.max(logits, axis=0, keepdims=True)
    ex = jnp.exp(logits - m)
    probs = ex / jnp.sum(ex, axis=0, keepdims=True)
    p = [probs[e:e + 1, :] for e in range(N_EXPERTS)]

    scores = []
    for g in range(N_GROUPS):
        q = p[g * EXPERTS_PER_GROUP:(g + 1) * EXPERTS_PER_GROUP]
        best = None
        for a in range(EXPERTS_PER_GROUP):
            for b in range(a + 1, EXPERTS_PER_GROUP):
                s = q[a] + q[b]
                best = s if best is None else jnp.maximum(best, s)
        scores.append(best)
    best_score = scores[0]
    gsel = jnp.zeros_like(best_score, dtype=jnp.int32)
    for g in range(1, N_GROUPS):
        better = scores[g] > best_score
        best_score = jnp.where(better, scores[g], best_score)
        gsel = jnp.where(better, g, gsel)

    slot_chosen = [None] * EXPERTS_PER_GROUP
    slot_gate = [None] * EXPERTS_PER_GROUP
    for e in range(N_EXPERTS):
        g, k = divmod(e, EXPERTS_PER_GROUP)
        rank = jnp.zeros_like(gsel)
        for e2 in range(g * EXPERTS_PER_GROUP, (g + 1) * EXPERTS_PER_GROUP):
            if e2 == e:
                continue
            ahead = (p[e2] > p[e]) | ((p[e2] == p[e]) & (e2 < e))
            rank = rank + ahead.astype(jnp.int32)
        chosen = (gsel == g) & (rank < TOP_K)
        gate = jnp.where(chosen, p[e] / best_score, 0.0)
        slot_chosen[k] = chosen if slot_chosen[k] is None else slot_chosen[k] | chosen
        slot_gate[k] = gate if slot_gate[k] is None else slot_gate[k] + gate
    c0, c1, c2, c3 = slot_chosen
    lo = jnp.where(c0, 0, jnp.where(c1, 1, 2))
    hi = jnp.where(c3, 3, jnp.where(c2, 2, 1))
    w_lo = jnp.where(c0, slot_gate[0], jnp.where(c1, slot_gate[1], slot_gate[2]))
    w_hi = jnp.where(c3, slot_gate[3], jnp.where(c2, slot_gate[2], slot_gate[1]))
    pair = jnp.right_shift(lo * (7 - lo), 1) + hi - lo - 1
    cls = gsel * MOE_PAIRS + pair
    zero = jnp.zeros_like(w_lo)
    o_ref[...] = jnp.concatenate([cls.astype(F32), w_lo, w_hi] + [zero] * (ROUTER_ROWS - 3), axis=0)


def _router(x, wr_t, br):
    t = x.shape[0]
    return pl.pallas_call(
        _router_kernel,
        out_shape=jax.ShapeDtypeStruct((ROUTER_ROWS, t), F32),
        grid=(t // ROUTER_TM,),
        in_specs=[pl.BlockSpec((ROUTER_TM, D_MODEL), lambda i: (i, 0)),
                  _const_spec(wr_t.shape), _const_spec(br.shape)],
        out_specs=pl.BlockSpec((ROUTER_ROWS, ROUTER_TM), lambda i: (0, i)),
        compiler_params=_params("arbitrary"),
        name="moe_router",
    )(x, wr_t, br)


MOE_TM = 256
FEATURE_ROWS = D_MODEL // V7X_LANES
DISPATCH_TD = 256
DMA_UNROLL = 8


def _token_rows(ref, n_tokens):
    return jnp.concatenate(
        [ref[pl.ds(c, n_tokens, stride=FEATURE_ROWS), :] for c in range(FEATURE_ROWS)], axis=1)


def _store_token_rows(ref, value, n_tokens):
    for c in range(FEATURE_ROWS):
        ref[pl.ds(c, n_tokens, stride=FEATURE_ROWS), :] = value[:, c * V7X_LANES:(c + 1) * V7X_LANES]


def _dispatch_kernel(pos_ref, x_ref, gate_ref, buf_in_ref, buf_ref, gate_tbl_ref, stage_ref, sems):
    del buf_in_ref
    i = pl.program_id(0)
    n = pl.num_programs(0)
    slot = i % 2
    stage = stage_ref.at[slot]
    rows = DISPATCH_TD * FEATURE_ROWS

    @pl.when(i == 0)
    def _():
        gate_tbl_ref[...] = jnp.zeros_like(gate_tbl_ref)

    def wait_slot(s):
        pltpu.make_async_copy(stage_ref.at[s], buf_ref.at[pl.ds(0, rows)], sems.at[s]).wait()

    @pl.when(i >= 2)
    def _():
        wait_slot(slot)

    _store_token_rows(stage, x_ref[...], DISPATCH_TD)

    def issue(r, _):
        p = pos_ref[i * DISPATCH_TD + r]
        dst = pl.multiple_of(p * FEATURE_ROWS, FEATURE_ROWS)
        src = pl.multiple_of(r * FEATURE_ROWS, FEATURE_ROWS)
        pltpu.make_async_copy(stage.at[pl.ds(src, FEATURE_ROWS)], buf_ref.at[pl.ds(dst, FEATURE_ROWS)],
                              sems.at[slot]).start()
        gate_tbl_ref[pl.ds(p, 1), :] = gate_ref[pl.ds(r, 1), :]
        return 0

    lax.fori_loop(0, DISPATCH_TD, issue, 0, unroll=DMA_UNROLL)

    @pl.when(i == n - 1)
    def _():
        wait_slot(slot)

        @pl.when(n > 1)
        def _():
            wait_slot(1 - slot)


def _dispatch(pos, x, gates, buf):
    t = x.shape[0]
    n_slots = buf.shape[0] // FEATURE_ROWS
    return pl.pallas_call(
        _dispatch_kernel,
        out_shape=(jax.ShapeDtypeStruct(buf.shape, buf.dtype),
                   jax.ShapeDtypeStruct((n_slots, V7X_LANES), F32)),
        grid_spec=pltpu.PrefetchScalarGridSpec(
            num_scalar_prefetch=1,
            grid=(t // DISPATCH_TD,),
            in_specs=[pl.BlockSpec((DISPATCH_TD, D_MODEL), lambda i, pos: (i, 0)),
                      pl.BlockSpec((DISPATCH_TD, V7X_LANES), lambda i, pos: (i, 0)),
                      pl.BlockSpec(memory_space=pl.ANY)],
            out_specs=(pl.BlockSpec(memory_space=pl.ANY),
                       pl.BlockSpec((n_slots, V7X_LANES), lambda i, pos: (0, 0))),
            scratch_shapes=[pltpu.VMEM((2, DISPATCH_TD * FEATURE_ROWS, V7X_LANES), F32),
                            pltpu.SemaphoreType.DMA((2,))]),
        input_output_aliases={3: 0},
        compiler_params=_params("arbitrary"),
        name="moe_dispatch",
    )(pos, x, gates, buf)


def _expert_kernel(ea_ref, eb_ref, fresh_ref, used_ref, xs_ref, gate_ref,
                   w1a_ref, w1b_ref, w3a_ref, w3b_ref, w2a_ref, w2b_ref, ys_ref,
                   w1_s, w3_s, w2_s):
    del ea_ref, eb_ref
    i = pl.program_id(0)

    @pl.when(fresh_ref[i] == 1)
    def _():
        for e, (w1, w3, w2) in enumerate(((w1a_ref, w3a_ref, w2a_ref), (w1b_ref, w3b_ref, w2b_ref))):
            w1_s[e] = w1[...].astype(BF16)
            w3_s[e] = w3[...].astype(BF16)
            w2_s[e] = w2[...].astype(BF16)

    @pl.when(used_ref[i] == 1)
    def _():
        xb = _token_rows(xs_ref, MOE_TM).astype(BF16)
        gates = gate_ref[...]
        y = None
        for e in range(TOP_K):
            h1 = jnp.dot(xb, w1_s[e], preferred_element_type=F32)
            h3 = jnp.dot(xb, w3_s[e], preferred_element_type=F32)
            h = (jax.nn.silu(h1) * h3 * gates[:, e:e + 1]).astype(BF16)
            ye = jnp.dot(h, w2_s[e], preferred_element_type=F32)
            y = ye if y is None else y + ye
        _store_token_rows(ys_ref, y, MOE_TM)

    @pl.when(used_ref[i] == 0)
    def _():
        ys_ref[...] = jnp.zeros_like(ys_ref)


def _expert_ffn(ea, eb, fresh, used, xs, gates, w1, w3, w2, layer):
    n_tiles = ea.shape[0]

    def w_spec(shape, which):
        return pl.BlockSpec((None, None) + shape,
                            lambda i, ea, eb, fresh, used: (layer, (ea, eb)[which][i], 0, 0))

    up, down = (D_MODEL, D_EXPERT), (D_EXPERT, D_MODEL)
    return pl.pallas_call(
        _expert_kernel,
        out_shape=jax.ShapeDtypeStruct((n_tiles * MOE_TM * FEATURE_ROWS, V7X_LANES), F32),
        grid_spec=pltpu.PrefetchScalarGridSpec(
            num_scalar_prefetch=4,
            grid=(n_tiles,),
            in_specs=[pl.BlockSpec((MOE_TM * FEATURE_ROWS, V7X_LANES), lambda i, *_: (i, 0)),
                      pl.BlockSpec((MOE_TM, V7X_LANES), lambda i, *_: (i, 0)),
                      w_spec(up, 0), w_spec(up, 1), w_spec(up, 0), w_spec(up, 1),
                      w_spec(down, 0), w_spec(down, 1)],
            out_specs=pl.BlockSpec((MOE_TM * FEATURE_ROWS, V7X_LANES), lambda i, *_: (i, 0)),
            scratch_shapes=[pltpu.VMEM((TOP_K,) + up, BF16), pltpu.VMEM((TOP_K,) + up, BF16),
                            pltpu.VMEM((TOP_K,) + down, BF16)]),
        compiler_params=_params("arbitrary"),
        name="moe_experts",
    )(ea, eb, fresh, used, xs, gates, w1, w1, w3, w3, w2, w2)


def _combine_kernel(pos_ref, ys_ref, x_ref, lng_ref, lnb_ref, o_ref, stage_ref, sems):
    i = pl.program_id(0)
    n = pl.num_programs(0)
    slot = i % 2
    rows = DISPATCH_TD * FEATURE_ROWS

    def issue_step(step, s):
        def issue(r, _):
            src = pl.multiple_of(pos_ref[step * DISPATCH_TD + r] * FEATURE_ROWS, FEATURE_ROWS)
            dst = pl.multiple_of(r * FEATURE_ROWS, FEATURE_ROWS)
            pltpu.make_async_copy(ys_ref.at[pl.ds(src, FEATURE_ROWS)],
                                  stage_ref.at[s, pl.ds(dst, FEATURE_ROWS)], sems.at[s]).start()
            return 0
        lax.fori_loop(0, DISPATCH_TD, issue, 0, unroll=DMA_UNROLL)

    @pl.when(i == 0)
    def _():
        issue_step(0, 0)

    @pl.when(i + 1 < n)
    def _():
        issue_step(i + 1, 1 - slot)

    pltpu.make_async_copy(ys_ref.at[pl.ds(0, rows)], stage_ref.at[slot], sems.at[slot]).wait()

    f = _token_rows(stage_ref.at[slot], DISPATCH_TD)
    o_ref[...] = _layer_norm(DEEPNORM_ALPHA * x_ref[...] + f, lng_ref[...], lnb_ref[...])


def _combine(pos, ys, x, ln_g, ln_b):
    t = x.shape[0]
    row_spec = pl.BlockSpec((DISPATCH_TD, D_MODEL), lambda i, pos: (i, 0))
    return pl.pallas_call(
        _combine_kernel,
        out_shape=jax.ShapeDtypeStruct((t, D_MODEL), F32),
        grid_spec=pltpu.PrefetchScalarGridSpec(
            num_scalar_prefetch=1,
            grid=(t // DISPATCH_TD,),
            in_specs=[pl.BlockSpec(memory_space=pl.ANY), row_spec,
                      pl.BlockSpec((1, D_MODEL), lambda i, pos: (0, 0)),
                      pl.BlockSpec((1, D_MODEL), lambda i, pos: (0, 0))],
            out_specs=row_spec,
            scratch_shapes=[pltpu.VMEM((2, DISPATCH_TD * FEATURE_ROWS, V7X_LANES), F32),
                            pltpu.SemaphoreType.DMA((2,))]),
        compiler_params=_params("arbitrary"),
        name="moe_combine",
    )(pos, ys, x, ln_g, ln_b)


def _pair_table():
    lo, hi = [], []
    for g in range(N_GROUPS):
        for a in range(EXPERTS_PER_GROUP):
            for b in range(a + 1, EXPERTS_PER_GROUP):
                lo.append(g * EXPERTS_PER_GROUP + a)
                hi.append(g * EXPERTS_PER_GROUP + b)
    return np.asarray(lo, np.int32), np.asarray(hi, np.int32)


def _sort_plan(cls, n_tiles):
    onehot = (cls[:, None] == jnp.arange(MOE_CLASSES, dtype=jnp.int32)[None, :]).astype(jnp.int32)
    running = jnp.cumsum(onehot, axis=0)
    counts = running[-1]
    padded = (counts + MOE_TM - 1) // MOE_TM * MOE_TM
    ends = jnp.cumsum(padded)
    rank = jnp.sum(running * onehot, axis=1) - 1
    pos = jnp.sum(onehot * (ends - padded)[None, :], axis=1) + rank
    tile_cls = jnp.searchsorted(ends, jnp.arange(n_tiles, dtype=jnp.int32) * MOE_TM, side="right")
    used = (tile_cls < MOE_CLASSES).astype(jnp.int32)
    tile_cls = jnp.minimum(tile_cls, MOE_CLASSES - 1).astype(jnp.int32)
    fresh = jnp.concatenate([jnp.ones((1,), jnp.int32),
                             (tile_cls[1:] != tile_cls[:-1]).astype(jnp.int32)])
    lo, hi = _pair_table()
    return pos.astype(jnp.int32), jnp.asarray(lo)[tile_cls], jnp.asarray(hi)[tile_cls], fresh, used


def _moe_layer(x, buf, wr_t, br, w1, w3, w2, ln_g, ln_b, layer):
    t = x.shape[0]
    n_tiles = buf.shape[0] // (MOE_TM * FEATURE_ROWS)
    routed = _router(x, wr_t, br)
    pos, ea, eb, fresh, used = _sort_plan(routed[0].astype(jnp.int32), n_tiles)
    gate_rows = jnp.pad(routed[1:1 + TOP_K].T, ((0, 0), (0, V7X_LANES - TOP_K)))
    buf, gates = _dispatch(pos, x, gate_rows, buf)
    ys = _expert_ffn(ea, eb, fresh, used, buf, gates, w1, w3, w2, layer)
    return _combine(pos, ys, x, ln_g, ln_b), buf


PROJ_TM = 512


def _proj_t_kernel(x_ref, wt_ref, o_ref, *, scale):
    acc = lax.dot_general(wt_ref[...], x_ref[...].astype(BF16), _NT, preferred_element_type=F32)
    o_ref[...] = (acc * scale).astype(o_ref.dtype)


def _project_t(x, w_t, scale):
    b, s, _ = x.shape
    n = w_t.shape[0]
    return pl.pallas_call(
        functools.partial(_proj_t_kernel, scale=scale),
        out_shape=jax.ShapeDtypeStruct((b, n, s), BF16),
        grid=(b, s // PROJ_TM),
        in_specs=[pl.BlockSpec((None, PROJ_TM, D_MODEL), lambda bi, i: (bi, i, 0)),
                  _const_spec(w_t.shape)],
        out_specs=pl.BlockSpec((None, n, PROJ_TM), lambda bi, i: (bi, 0, i)),
        compiler_params=_params("arbitrary", "arbitrary"),
        name="project_t",
    )(x, w_t)


def _kv_kernel(x_ref, wk_ref, wvt_ref, k_ref, vt_ref, kmean_ref):
    xb = x_ref[...].astype(BF16)
    k = jnp.dot(xb, wk_ref[...], preferred_element_type=F32)
    k_ref[...] = k.astype(BF16)
    for j in range(PROJ_TM // MOBA_BLOCK):
        kmean_ref[j] = jnp.mean(k[j * MOBA_BLOCK:(j + 1) * MOBA_BLOCK], axis=0, keepdims=True)
    vt_ref[...] = lax.dot_general(wvt_ref[...], xb, _NT, preferred_element_type=F32).astype(BF16)


def _project_kv(x, w_k, w_v_t):
    b, s, _ = x.shape
    blocks_per_tile = PROJ_TM // MOBA_BLOCK
    tiles = s // PROJ_TM
    return pl.pallas_call(
        _kv_kernel,
        out_shape=(jax.ShapeDtypeStruct((b, s, D_MODEL), BF16),
                   jax.ShapeDtypeStruct((b, D_MODEL, s), BF16),
                   jax.ShapeDtypeStruct((b * s // MOBA_BLOCK, 1, D_MODEL), F32)),
        grid=(b, tiles),
        in_specs=[pl.BlockSpec((None, PROJ_TM, D_MODEL), lambda bi, i: (bi, i, 0)),
                  _const_spec(w_k.shape), _const_spec(w_v_t.shape)],
        out_specs=(pl.BlockSpec((None, PROJ_TM, D_MODEL), lambda bi, i: (bi, i, 0)),
                   pl.BlockSpec((None, D_MODEL, PROJ_TM), lambda bi, i: (bi, 0, i)),
                   pl.BlockSpec((blocks_per_tile, 1, D_MODEL), lambda bi, i: (bi * tiles + i, 0, 0))),
        compiler_params=_params("arbitrary", "arbitrary"),
        name="project_kv",
    )(x, w_k, w_v_t)


def _t5_bucket_np(dist):
    n = np.maximum(dist, 0)
    max_exact = N_BUCKETS // 2
    nf = np.maximum(n, 1).astype(np.float32)
    large = max_exact + (np.log(nf / np.float32(max_exact)) / np.float32(math.log(MAX_DISTANCE / max_exact))
                         * np.float32(N_BUCKETS - max_exact)).astype(np.int32)
    large = np.minimum(large, N_BUCKETS - 1)
    return np.where(n < max_exact, n, large).astype(np.int32)


def _bucket_tables():
    kpos = np.arange(MOBA_BLOCK)[:, None]
    qpos = np.arange(MOBA_BLOCK)[None, :]
    own = np.where(qpos >= kpos, _t5_bucket_np(qpos - kpos), -1)
    prev = _t5_bucket_np(qpos + MOBA_BLOCK - kpos)
    return np.stack([own, prev]).astype(np.int32)


def _bias_kernel(rel_ref, bucket_ref, o_ref):
    h = pl.program_id(0)
    far = rel_ref[N_BUCKETS - 1, h]
    for t in range(2):
        bucket = bucket_ref[t]
        acc = jnp.zeros(bucket.shape, F32)
        for b in range(N_BUCKETS):
            acc = jnp.where(bucket == b, rel_ref[b, h], acc)
        o_ref[t] = jnp.where(bucket < 0, NEG_INF, (acc - far) * LOG2_E)


def _bias_tables(rel_bias):
    buckets = jnp.asarray(_bucket_tables())
    return pl.pallas_call(
        _bias_kernel,
        out_shape=jax.ShapeDtypeStruct((N_HEADS, 2, MOBA_BLOCK, MOBA_BLOCK), F32),
        grid=(N_HEADS,),
        in_specs=[pl.BlockSpec(memory_space=pltpu.SMEM), _const_spec(buckets.shape)],
        out_specs=pl.BlockSpec((None, 2, MOBA_BLOCK, MOBA_BLOCK), lambda h: (h, 0, 0, 0)),
        compiler_params=_params("arbitrary"),
        name="rel_bias_tables",
    )(rel_bias, buckets)


ATTN_HEADS = 8
ATTN_WIDTH = ATTN_HEADS * HEAD_DIM
HEADS_PER_LANE_TILE = V7X_LANES // HEAD_DIM
FAR_TILES_PER_STEP = 2
QK_LOOKAHEAD = 6
LOG2_E = math.log2(math.e)


def _attn_kernel(qt_ref, k_ref, vt_ref, kmean_ref, bias_ref, o_ref,
                 sel_ref, ml_ref, acc_ref, qp_ref, sc_ref):
    n_blocks = kmean_ref.shape[0]
    blk = lax.broadcasted_iota(jnp.int32, (n_blocks, MOBA_BLOCK), 0)
    causal = (lax.broadcasted_iota(jnp.int32, (MOBA_BLOCK, MOBA_BLOCK), 0)
              <= lax.broadcasted_iota(jnp.int32, (MOBA_BLOCK, MOBA_BLOCK), 1))
    ones_rows = jnp.ones((BF16_SUBLANE_TILE, MOBA_BLOCK), BF16)

    def lane_slab(h):
        return slice((h // HEADS_PER_LANE_TILE) * V7X_LANES, (h // HEADS_PER_LANE_TILE + 1) * V7X_LANES)

    def padded_q(h, q0):
        qh = qt_ref[h * HEAD_DIM:(h + 1) * HEAD_DIM, pl.ds(q0, MOBA_BLOCK)]
        pieces = [jnp.zeros_like(qh)] * HEADS_PER_LANE_TILE
        pieces[h % HEADS_PER_LANE_TILE] = qh
        return jnp.concatenate(pieces, axis=0)

    def scores(h, j):
        k0 = pl.multiple_of(j * MOBA_BLOCK, MOBA_BLOCK)
        kj = k_ref[pl.ds(k0, MOBA_BLOCK), lane_slab(h)]
        return jnp.dot(kj, qp_ref[h], preferred_element_type=F32)

    def weights_and_values(h, j, s, bias, sel):
        if bias is not None:
            keep = causal if sel is None else sel
            s = jnp.where(keep, s, NEG_INF) + bias[...]
        m_loc = jnp.max(s, axis=0, keepdims=True)
        if sel is None:
            shift = m_loc
        else:
            shift = jnp.where(sel, m_loc, -NEG_INF)
            m_loc = jnp.where(sel, m_loc, NEG_INF)
        p = jnp.exp2(s - shift).astype(BF16)
        k0 = pl.multiple_of(j * MOBA_BLOCK, MOBA_BLOCK)
        vtj = vt_ref[h * HEAD_DIM:(h + 1) * HEAD_DIM, pl.ds(k0, MOBA_BLOCK)]
        pv = jnp.dot(jnp.concatenate([vtj, ones_rows], axis=0), p, preferred_element_type=F32)
        return m_loc, pv[HEAD_DIM:HEAD_DIM + 1], pv[:HEAD_DIM]

    def run_tiles(tiles, carried_in, prefetch):
        n = len(tiles)
        s = [None] * n
        out = [None] * n
        for t in range(QK_LOOKAHEAD if carried_in else 0, n + QK_LOOKAHEAD):
            if t < n:
                h, j, _, _ = tiles[t]
                s[t] = scores(h, j)
            else:
                h, j = prefetch[t - n]
                sc_ref[t - n] = scores(h, j)
            d = t - QK_LOOKAHEAD
            if d >= 0:
                h, j, bias, sel = tiles[d]
                sd = sc_ref[d] if carried_in and d < QK_LOOKAHEAD else s[d]
                out[d] = weights_and_values(h, j, sd, bias, sel)
                s[d] = None
        return out

    def far_tile_ids(t):
        return [(h, t * FAR_TILES_PER_STEP + u) for h in range(ATTN_HEADS) for u in range(FAR_TILES_PER_STEP)]

    def merge(h, parts, first):
        m_new = parts[0][0]
        for m_loc, _, _ in parts[1:]:
            m_new = jnp.maximum(m_new, m_loc)
        if first:
            l_new = jnp.zeros_like(m_new)
            acc_new = jnp.zeros((HEAD_DIM, MOBA_BLOCK), F32)
        else:
            m_old = ml_ref[h, 0:1, :]
            m_new = jnp.maximum(m_new, m_old)
            alpha = jnp.exp2(m_old - m_new)
            l_new = alpha * ml_ref[h, 1:2, :]
            acc_new = alpha * acc_ref[h]
        for m_loc, l_loc, pv in parts:
            w = jnp.exp2(m_loc - m_new)
            l_new = l_new + w * l_loc
            acc_new = acc_new + w * pv
        ml_ref[h, 0:1, :] = m_new
        ml_ref[h, 1:2, :] = l_new
        acc_ref[h] = acc_new

    def q_block(i, _):
        q0 = pl.multiple_of(i * MOBA_BLOCK, MOBA_BLOCK)
        jp = jnp.maximum(i - 1, 0)
        n_far = jnp.maximum(i - 1, 0)
        for h in range(ATTN_HEADS):
            qp_ref[h] = padded_q(h, q0)

        for h in range(ATTN_HEADS):
            gate = jnp.dot(kmean_ref[:, lane_slab(h)].astype(BF16), qp_ref[h], preferred_element_type=F32)
            gate = jnp.where(blk < i, gate, NEG_INF)
            sel = jnp.zeros(gate.shape, F32)
            for r in range(MOBA_TOPK):
                top = jnp.max(gate, axis=0, keepdims=True)
                first = jnp.min(jnp.where(gate == top, blk, n_blocks), axis=0, keepdims=True)
                hit = blk == first
                sel = jnp.where(hit & (r < i), 1.0, sel)
                gate = jnp.where(hit, -jnp.inf, gate)
            sel_ref[h] = sel

        tiles = []
        for h in range(ATTN_HEADS):
            tiles.append((h, i, bias_ref.at[h, 0], None))
            tiles.append((h, jp, bias_ref.at[h, 1], sel_ref[h, pl.ds(jp, 1), :] > 0.5))
        last_far = jnp.maximum(n_far - 1, 0)

        def prefetch_ids(t):
            return [(h, jnp.minimum(j, last_far)) for h, j in far_tile_ids(t)[:QK_LOOKAHEAD]]

        parts = run_tiles(tiles, carried_in=False, prefetch=prefetch_ids(0))
        for h in range(ATTN_HEADS):
            merge(h, parts[2 * h:2 * h + 2], first=True)

        def far_step(t, _):
            tiles = []
            for h, j in far_tile_ids(t):
                jc = jnp.minimum(j, last_far)
                sel = (sel_ref[h, pl.ds(jc, 1), :] > 0.5) & (j < n_far)
                tiles.append((h, jc, None, sel))
            parts = run_tiles(tiles, carried_in=True, prefetch=prefetch_ids(t + 1))
            for h in range(ATTN_HEADS):
                merge(h, parts[h * FAR_TILES_PER_STEP:(h + 1) * FAR_TILES_PER_STEP], first=False)
            return 0

        lax.fori_loop(0, (n_far + FAR_TILES_PER_STEP - 1) // FAR_TILES_PER_STEP, far_step, 0)

        for slab in range(ATTN_HEADS // HEADS_PER_LANE_TILE):
            heads = range(slab * HEADS_PER_LANE_TILE, (slab + 1) * HEADS_PER_LANE_TILE)
            out_t = jnp.concatenate([acc_ref[h] / ml_ref[h, 1:2, :] for h in heads], axis=0)
            o_ref[pl.ds(q0, MOBA_BLOCK), slab * V7X_LANES:(slab + 1) * V7X_LANES] = out_t.T.astype(o_ref.dtype)
        return 0

    lax.fori_loop(0, n_blocks, q_block, 0)


def _moba_attention(qt, k, vt, kmean, bias):
    b, _, s = qt.shape
    n_blocks = s // MOBA_BLOCK
    single = dict(pipeline_mode=pl.Buffered(1))
    t_spec = pl.BlockSpec((None, ATTN_WIDTH, s), lambda bi, hi: (bi, hi, 0), **single)
    return pl.pallas_call(
        _attn_kernel,
        out_shape=jax.ShapeDtypeStruct((b, s, D_MODEL), BF16),
        grid=(b, N_HEADS // ATTN_HEADS),
        in_specs=[t_spec,
                  pl.BlockSpec((None, s, ATTN_WIDTH), lambda bi, hi: (bi, 0, hi), **single),
                  t_spec,
                  pl.BlockSpec((None, n_blocks, ATTN_WIDTH), lambda bi, hi: (bi, 0, hi)),
                  pl.BlockSpec((ATTN_HEADS, 2, MOBA_BLOCK, MOBA_BLOCK), lambda bi, hi: (hi, 0, 0, 0), **single)],
        out_specs=pl.BlockSpec((None, s, ATTN_WIDTH), lambda bi, hi: (bi, 0, hi)),
        scratch_shapes=[pltpu.VMEM((ATTN_HEADS, n_blocks, MOBA_BLOCK), F32),
                        pltpu.VMEM((ATTN_HEADS, 2, MOBA_BLOCK), F32),
                        pltpu.VMEM((ATTN_HEADS, HEAD_DIM, MOBA_BLOCK), F32),
                        pltpu.VMEM((ATTN_HEADS, V7X_LANES, MOBA_BLOCK), BF16),
                        pltpu.VMEM((QK_LOOKAHEAD, MOBA_BLOCK, MOBA_BLOCK), F32)],
        compiler_params=_params("arbitrary", "arbitrary"),
        name="moba_attention",
    )(qt, k, vt, kmean, bias)


OPROJ_TM = 512


def _oproj_kernel(a_ref, w_ref, x_ref, lng_ref, lnb_ref, o_ref):
    h = jnp.dot(a_ref[...], w_ref[...], preferred_element_type=F32)
    o_ref[...] = _layer_norm(DEEPNORM_ALPHA * x_ref[...] + h, lng_ref[...], lnb_ref[...])


def _out_project(a, w_o, x, ln_g, ln_b):
    t = x.shape[0]
    row_spec = pl.BlockSpec((OPROJ_TM, D_MODEL), lambda i: (i, 0))
    return pl.pallas_call(
        _oproj_kernel,
        out_shape=jax.ShapeDtypeStruct((t, D_MODEL), F32),
        grid=(t // OPROJ_TM,),
        in_specs=[row_spec, _const_spec(w_o.shape), row_spec,
                  _const_spec(ln_g.shape), _const_spec(ln_b.shape)],
        out_specs=row_spec,
        compiler_params=_params("arbitrary"),
        name="out_project",
    )(a, w_o, x, ln_g, ln_b)


def kernel(x, ln_g, ln_b, a_w_in, a_v_g, a_v_b, a_w_s, a_b_s, a_w_out, w_k, w_v, b_w_q, b_w_o,
           rel_bias, w_router, b_router, e_w1, e_w3, e_w2):
    batch, seq, _ = x.shape
    t = batch * seq
    n_blocks = seq // MOBA_BLOCK
    xt = x.reshape(t, D_MODEL)
    wr_t = w_router.T.astype(BF16)
    br = b_router.reshape(N_EXPERTS, 1)
    n_tiles = t // MOE_TM + MOE_CLASSES
    sorted_buf = jnp.zeros((n_tiles * MOE_TM * FEATURE_ROWS, V7X_LANES), F32)

    def ln_rows(l, j):
        return ln_g[l, j].reshape(1, D_MODEL), ln_b[l, j].reshape(1, D_MODEL)

    k = vt = kmean = bias = None
    for l in range(DEPTH):
        g0, b0 = ln_rows(l, 0)
        if l < N_A_LAYERS:
            xt = _gmlp_layer(xt, a_w_in[l].astype(BF16),
                             a_v_g[l].reshape(1, GMLP_HALF), a_v_b[l].reshape(1, GMLP_HALF),
                             a_w_s[l], a_b_s[l].T, a_w_out[l].astype(BF16), g0, b0)
        else:
            x3 = xt.reshape(batch, seq, D_MODEL)
            if l == N_A_LAYERS:
                k, vt, kmean = _project_kv(x3, w_k.astype(BF16), w_v.T.astype(BF16))
                kmean = kmean.reshape(batch, n_blocks, D_MODEL)
                bias = _bias_tables(rel_bias)
            j = l - N_A_LAYERS
            qt = _project_t(x3, b_w_q[j].T.astype(BF16), HEAD_DIM ** -0.5 * LOG2_E)
            a = _moba_attention(qt, k, vt, kmean, bias)
            xt = _out_project(a.reshape(t, D_MODEL), b_w_o[j].astype(BF16), xt, g0, b0)
        g1, b1 = ln_rows(l, 1)
        xt, sorted_buf = _moe_layer(xt, sorted_buf, wr_t, br, e_w1, e_w3, e_w2, g1, b1, layer=l)
    return xt.reshape(batch, seq, D_MODEL)
```

```python
import functools
import math

import numpy as np
import jax
import jax.numpy as jnp
from jax import lax
from jax.experimental import pallas as pl
from jax.experimental.pallas import tpu as pltpu

D_MODEL = 1024
DEPTH = 4
N_A_LAYERS = DEPTH // 2
GMLP_HALF = 3 * D_MODEL
GMLP_GROUPS = 8
GMLP_GROUP_WIDTH = GMLP_HALF // GMLP_GROUPS
GMLP_CHUNK = 128
N_HEADS = 16
HEAD_DIM = D_MODEL // N_HEADS
MOBA_BLOCK = 256
MOBA_TOPK = 3
N_BUCKETS = 32
MAX_DISTANCE = 128
N_EXPERTS = 16
N_GROUPS = 4
EXPERTS_PER_GROUP = N_EXPERTS // N_GROUPS
TOP_K = 2
MOE_PAIRS = EXPERTS_PER_GROUP * (EXPERTS_PER_GROUP - 1) // 2
MOE_CLASSES = N_GROUPS * MOE_PAIRS
ROUTER_ROWS = 8
D_EXPERT = D_MODEL // 4
DEEPNORM_ALPHA = (2 * DEPTH) ** 0.25
LN_EPS = 1e-5
NEG_INF = -1e30

V7X_VMEM_LIMIT_BYTES = 56 * 1024 * 1024
V7X_LANES = 128
BF16_SUBLANE_TILE = 16

BF16 = jnp.bfloat16
F32 = jnp.float32

_NT = (((1,), (1,)), ((), ()))


def _params(*semantics):
    return pltpu.CompilerParams(dimension_semantics=semantics,
                                vmem_limit_bytes=V7X_VMEM_LIMIT_BYTES)


def _const_spec(shape):
    zeros = (0,) * len(shape)
    return pl.BlockSpec(shape, lambda *_: zeros, pipeline_mode=pl.Buffered(1))


def _layer_norm(y, g, b):
    mu = jnp.mean(y, axis=-1, keepdims=True)
    d = y - mu
    var = jnp.mean(d * d, axis=-1, keepdims=True)
    return d * lax.rsqrt(var + LN_EPS) * g + b


GMLP_TM = 512
GMLP_GROUPS_PER_STEP = 2


def _gmlp_kernel(x_ref, w_in_ref, vg_ref, vb_ref, ws_ref, bs_ref, w_out_ref,
                 lng_ref, lnb_ref, wr_ref, br_ref, o_ref, route_ref, vn_ref, sv_ref):
    x = x_ref[...]
    xb = x.astype(BF16)
    v = jax.nn.gelu(jnp.dot(xb, w_in_ref[:, GMLP_HALF:], preferred_element_type=F32))
    vn_ref[...] = _layer_norm(v, vg_ref[...], vb_ref[...]).astype(BF16)

    row = lax.broadcasted_iota(jnp.int32, (GMLP_CHUNK, GMLP_CHUNK), 0)
    col = lax.broadcasted_iota(jnp.int32, (GMLP_CHUNK, GMLP_CHUNK), 1)
    causal = row >= col
    width = GMLP_GROUPS_PER_STEP * GMLP_GROUP_WIDTH
    acc = jnp.zeros((GMLP_TM, D_MODEL), F32)
    for step in range(GMLP_GROUPS // GMLP_GROUPS_PER_STEP):
        lo = step * width
        u = jax.nn.gelu(jnp.dot(xb, w_in_ref[:, lo:lo + width], preferred_element_type=F32))
        for gi in range(GMLP_GROUPS_PER_STEP):
            g = step * GMLP_GROUPS_PER_STEP + gi
            w_s = jnp.where(causal, ws_ref[g], 0.0).astype(BF16)
            b_s = bs_ref[:, g:g + 1]
            glo = g * GMLP_GROUP_WIDTH
            for c in range(GMLP_TM // GMLP_CHUNK):
                rows = slice(c * GMLP_CHUNK, (c + 1) * GMLP_CHUNK)
                vc = vn_ref[rows, glo:glo + GMLP_GROUP_WIDTH]
                sv_ref[rows, gi * GMLP_GROUP_WIDTH:(gi + 1) * GMLP_GROUP_WIDTH] = (
                    jnp.dot(w_s, vc, preferred_element_type=F32) + b_s)
        gated = (u * sv_ref[...]).astype(BF16)
        acc = acc + jnp.dot(gated, w_out_ref[lo:lo + width, :], preferred_element_type=F32)
    out = _layer_norm(DEEPNORM_ALPHA * x + acc, lng_ref[...], lnb_ref[...])
    o_ref[...] = out
    route_ref[...] = _route(out, wr_ref[...], br_ref[...])


def _gmlp_layer(x, w_in, v_g, v_b, w_s, b_s_t, w_out, ln_g, ln_b, wr_t, br):
    t = x.shape[0]
    row_spec = pl.BlockSpec((GMLP_TM, D_MODEL), lambda i: (i, 0))
    return pl.pallas_call(
        _gmlp_kernel,
        out_shape=(jax.ShapeDtypeStruct((t, D_MODEL), F32), jax.ShapeDtypeStruct((ROUTER_ROWS, t), F32)),
        grid=(t // GMLP_TM,),
        in_specs=[row_spec,
                  _const_spec(w_in.shape), _const_spec(v_g.shape), _const_spec(v_b.shape),
                  _const_spec(w_s.shape), _const_spec(b_s_t.shape), _const_spec(w_out.shape),
                  _const_spec(ln_g.shape), _const_spec(ln_b.shape),
                  _const_spec(wr_t.shape), _const_spec(br.shape)],
        out_specs=(row_spec, pl.BlockSpec((ROUTER_ROWS, GMLP_TM), lambda i: (0, i))),
        scratch_shapes=[pltpu.VMEM((GMLP_TM, GMLP_HALF), BF16),
                        pltpu.VMEM((GMLP_TM, GMLP_GROUPS_PER_STEP * GMLP_GROUP_WIDTH), F32)],
        compiler_params=_params("arbitrary"),
        name="gmlp_layer",
    )(x, w_in, v_g, v_b, w_s, b_s_t, w_out, ln_g, ln_b, wr_t, br)


def _route(x, wr_t, br):
    logits = lax.dot_general(wr_t, x.astype(BF16), _NT, preferred_element_type=F32) + br
    m = jnp.max(logits, axis=0, keepdims=True)
    ex = jnp.exp(logits - m)
    probs = ex / jnp.sum(ex, axis=0, keepdims=True)
    p = [probs[e:e + 1, :] for e in range(N_EXPERTS)]

    scores = []
    for g in range(N_GROUPS):
        q = p[g * EXPERTS_PER_GROUP:(g + 1) * EXPERTS_PER_GROUP]
        best = None
        for a in range(EXPERTS_PER_GROUP):
            for b in range(a + 1, EXPERTS_PER_GROUP):
                s = q[a] + q[b]
                best = s if best is None else jnp.maximum(best, s)
        scores.append(best)
    best_score = scores[0]
    gsel = jnp.zeros_like(best_score, dtype=jnp.int32)
    for g in range(1, N_GROUPS):
        better = scores[g] > best_score
        best_score = jnp.where(better, scores[g], best_score)
        gsel = jnp.where(better, g, gsel)

    slot_chosen = [None] * EXPERTS_PER_GROUP
    slot_gate = [None] * EXPERTS_PER_GROUP
    for e in range(N_EXPERTS):
        g, k = divmod(e, EXPERTS_PER_GROUP)
        rank = jnp.zeros_like(gsel)
        for e2 in range(g * EXPERTS_PER_GROUP, (g + 1) * EXPERTS_PER_GROUP):
            if e2 == e:
                continue
            ahead = (p[e2] > p[e]) | ((p[e2] == p[e]) & (e2 < e))
            rank = rank + ahead.astype(jnp.int32)
        chosen = (gsel == g) & (rank < TOP_K)
        gate = jnp.where(chosen, p[e] / best_score, 0.0)
        slot_chosen[k] = chosen if slot_chosen[k] is None else slot_chosen[k] | chosen
        slot_gate[k] = gate if slot_gate[k] is None else slot_gate[k] + gate
    c0, c1, c2, c3 = slot_chosen
    lo = jnp.where(c0, 0, jnp.where(c1, 1, 2))
    hi = jnp.where(c3, 3, jnp.where(c2, 2, 1))
    w_lo = jnp.where(c0, slot_gate[0], jnp.where(c1, slot_gate[1], slot_gate[2]))
    w_hi = jnp.where(c3, slot_gate[3], jnp.where(c2, slot_gate[2], slot_gate[1]))
    pair = jnp.right_shift(lo * (7 - lo), 1) + hi - lo - 1
    cls = gsel * MOE_PAIRS + pair
    zero = jnp.zeros_like(w_lo)
    return jnp.concatenate([cls.astype(F32), w_lo, w_hi] + [zero] * (ROUTER_ROWS - 3), axis=0)


MOE_TM = 256
FEATURE_ROWS = D_MODEL // V7X_LANES
DISPATCH_TD = 256
DMA_UNROLL = 8


def _token_rows(ref, n_tokens):
    return jnp.concatenate(
        [ref[pl.ds(c, n_tokens, stride=FEATURE_ROWS), :] for c in range(FEATURE_ROWS)], axis=1)


def _store_token_rows(ref, value, n_tokens):
    for c in range(FEATURE_ROWS):
        ref[pl.ds(c, n_tokens, stride=FEATURE_ROWS), :] = value[:, c * V7X_LANES:(c + 1) * V7X_LANES]


def _dispatch_kernel(pos_ref, x_ref, gate_ref, buf_in_ref, buf_ref, gate_tbl_ref, stage_ref, sems):
    del buf_in_ref
    i = pl.program_id(0)
    n = pl.num_programs(0)
    slot = i % 2
    stage = stage_ref.at[slot]
    rows = DISPATCH_TD * FEATURE_ROWS

    @pl.when(i == 0)
    def _():
        gate_tbl_ref[...] = jnp.zeros_like(gate_tbl_ref)

    def wait_slot(s):
        pltpu.make_async_copy(stage_ref.at[s], buf_ref.at[pl.ds(0, rows)], sems.at[s]).wait()

    @pl.when(i >= 2)
    def _():
        wait_slot(slot)

    _store_token_rows(stage, x_ref[...], DISPATCH_TD)

    def issue(r, _):
        p = pos_ref[i * DISPATCH_TD + r]
        dst = pl.multiple_of(p * FEATURE_ROWS, FEATURE_ROWS)
        src = pl.multiple_of(r * FEATURE_ROWS, FEATURE_ROWS)
        pltpu.make_async_copy(stage.at[pl.ds(src, FEATURE_ROWS)], buf_ref.at[pl.ds(dst, FEATURE_ROWS)],
                              sems.at[slot]).start()
        gate_tbl_ref[pl.ds(p, 1), :] = gate_ref[pl.ds(r, 1), :]
        return 0

    lax.fori_loop(0, DISPATCH_TD, issue, 0, unroll=DMA_UNROLL)

    @pl.when(i == n - 1)
    def _():
        wait_slot(slot)

        @pl.when(n > 1)
        def _():
            wait_slot(1 - slot)


def _dispatch(pos, x, gates, buf):
    t = x.shape[0]
    n_slots = buf.shape[0] // FEATURE_ROWS
    return pl.pallas_call(
        _dispatch_kernel,
        out_shape=(jax.ShapeDtypeStruct(buf.shape, buf.dtype),
                   jax.ShapeDtypeStruct((n_slots, V7X_LANES), F32)),
        grid_spec=pltpu.PrefetchScalarGridSpec(
            num_scalar_prefetch=1,
            grid=(t // DISPATCH_TD,),
            in_specs=[pl.BlockSpec((DISPATCH_TD, D_MODEL), lambda i, pos: (i, 0)),
                      pl.BlockSpec((DISPATCH_TD, V7X_LANES), lambda i, pos: (i, 0)),
                      pl.BlockSpec(memory_space=pl.ANY)],
            out_specs=(pl.BlockSpec(memory_space=pl.ANY),
                       pl.BlockSpec((n_slots, V7X_LANES), lambda i, pos: (0, 0))),
            scratch_shapes=[pltpu.VMEM((2, DISPATCH_TD * FEATURE_ROWS, V7X_LANES), F32),
                            pltpu.SemaphoreType.DMA((2,))]),
        input_output_aliases={3: 0},
        compiler_params=_params("arbitrary"),
        name="moe_dispatch",
    )(pos, x, gates, buf)


def _expert_kernel(ea_ref, eb_ref, fresh_ref, used_ref, xs_ref, gate_ref,
                   w1a_ref, w1b_ref, w3a_ref, w3b_ref, w2a_ref, w2b_ref, ys_ref,
                   w1_s, w3_s, w2_s):
    del ea_ref, eb_ref
    i = pl.program_id(0)

    @pl.when(fresh_ref[i] == 1)
    def _():
        for e, (w1, w3, w2) in enumerate(((w1a_ref, w3a_ref, w2a_ref), (w1b_ref, w3b_ref, w2b_ref))):
            w1_s[e] = w1[...].astype(BF16)
            w3_s[e] = w3[...].astype(BF16)
            w2_s[e] = w2[...].astype(BF16)

    @pl.when(used_ref[i] == 1)
    def _():
        xb = _token_rows(xs_ref, MOE_TM).astype(BF16)
        gates = gate_ref[...]
        y = None
        for e in range(TOP_K):
            h1 = jnp.dot(xb, w1_s[e], preferred_element_type=F32)
            h3 = jnp.dot(xb, w3_s[e], preferred_element_type=F32)
            h = (jax.nn.silu(h1) * h3 * gates[:, e:e + 1]).astype(BF16)
            ye = jnp.dot(h, w2_s[e], preferred_element_type=F32)
            y = ye if y is None else y + ye
        _store_token_rows(ys_ref, y, MOE_TM)

    @pl.when(used_ref[i] == 0)
    def _():
        ys_ref[...] = jnp.zeros_like(ys_ref)


def _expert_ffn(ea, eb, fresh, used, xs, gates, w1, w3, w2, layer):
    n_tiles = ea.shape[0]

    def w_spec(shape, which):
        return pl.BlockSpec((None, None) + shape,
                            lambda i, ea, eb, fresh, used: (layer, (ea, eb)[which][i], 0, 0))

    up, down = (D_MODEL, D_EXPERT), (D_EXPERT, D_MODEL)
    return pl.pallas_call(
        _expert_kernel,
        out_shape=jax.ShapeDtypeStruct((n_tiles * MOE_TM * FEATURE_ROWS, V7X_LANES), F32),
        grid_spec=pltpu.PrefetchScalarGridSpec(
            num_scalar_prefetch=4,
            grid=(n_tiles,),
            in_specs=[pl.BlockSpec((MOE_TM * FEATURE_ROWS, V7X_LANES), lambda i, *_: (i, 0)),
                      pl.BlockSpec((MOE_TM, V7X_LANES), lambda i, *_: (i, 0)),
                      w_spec(up, 0), w_spec(up, 1), w_spec(up, 0), w_spec(up, 1),
                      w_spec(down, 0), w_spec(down, 1)],
            out_specs=pl.BlockSpec((MOE_TM * FEATURE_ROWS, V7X_LANES), lambda i, *_: (i, 0)),
            scratch_shapes=[pltpu.VMEM((TOP_K,) + up, BF16), pltpu.VMEM((TOP_K,) + up, BF16),
                            pltpu.VMEM((TOP_K,) + down, BF16)]),
        compiler_params=_params("arbitrary"),
        name="moe_experts",
    )(ea, eb, fresh, used, xs, gates, w1, w1, w3, w3, w2, w2)


def _combine_kernel(pos_ref, ys_ref, x_ref, lng_ref, lnb_ref, o_ref, stage_ref, sems):
    i = pl.program_id(0)
    n = pl.num_programs(0)
    slot = i % 2
    rows = DISPATCH_TD * FEATURE_ROWS

    def issue_step(step, s):
        def issue(r, _):
            src = pl.multiple_of(pos_ref[step * DISPATCH_TD + r] * FEATURE_ROWS, FEATURE_ROWS)
            dst = pl.multiple_of(r * FEATURE_ROWS, FEATURE_ROWS)
            pltpu.make_async_copy(ys_ref.at[pl.ds(src, FEATURE_ROWS)],
                                  stage_ref.at[s, pl.ds(dst, FEATURE_ROWS)], sems.at[s]).start()
            return 0
        lax.fori_loop(0, DISPATCH_TD, issue, 0, unroll=DMA_UNROLL)

    @pl.when(i == 0)
    def _():
        issue_step(0, 0)

    @pl.when(i + 1 < n)
    def _():
        issue_step(i + 1, 1 - slot)

    pltpu.make_async_copy(ys_ref.at[pl.ds(0, rows)], stage_ref.at[slot], sems.at[slot]).wait()

    f = _token_rows(stage_ref.at[slot], DISPATCH_TD)
    o_ref[...] = _layer_norm(DEEPNORM_ALPHA * x_ref[...] + f, lng_ref[...], lnb_ref[...])


def _combine(pos, ys, x, ln_g, ln_b):
    t = x.shape[0]
    row_spec = pl.BlockSpec((DISPATCH_TD, D_MODEL), lambda i, pos: (i, 0))
    return pl.pallas_call(
        _combine_kernel,
        out_shape=jax.ShapeDtypeStruct((t, D_MODEL), F32),
        grid_spec=pltpu.PrefetchScalarGridSpec(
            num_scalar_prefetch=1,
            grid=(t // DISPATCH_TD,),
            in_specs=[pl.BlockSpec(memory_space=pl.ANY), row_spec,
                      pl.BlockSpec((1, D_MODEL), lambda i, pos: (0, 0)),
                      pl.BlockSpec((1, D_MODEL), lambda i, pos: (0, 0))],
            out_specs=row_spec,
            scratch_shapes=[pltpu.VMEM((2, DISPATCH_TD * FEATURE_ROWS, V7X_LANES), F32),
                            pltpu.SemaphoreType.DMA((2,))]),
        compiler_params=_params("arbitrary"),
        name="moe_combine",
    )(pos, ys, x, ln_g, ln_b)


def _pair_table():
    lo, hi = [], []
    for g in range(N_GROUPS):
        for a in range(EXPERTS_PER_GROUP):
            for b in range(a + 1, EXPERTS_PER_GROUP):
                lo.append(g * EXPERTS_PER_GROUP + a)
                hi.append(g * EXPERTS_PER_GROUP + b)
    return np.asarray(lo, np.int32), np.asarray(hi, np.int32)


def _sort_plan(cls, n_tiles):
    onehot = (cls[:, None] == jnp.arange(MOE_CLASSES, dtype=jnp.int32)[None, :]).astype(jnp.int32)
    running = jnp.cumsum(onehot, axis=0)
    counts = running[-1]
    padded = (counts + MOE_TM - 1) // MOE_TM * MOE_TM
    ends = jnp.cumsum(padded)
    rank = jnp.sum(running * onehot, axis=1) - 1
    pos = jnp.sum(onehot * (ends - padded)[None, :], axis=1) + rank
    tile_cls = jnp.searchsorted(ends, jnp.arange(n_tiles, dtype=jnp.int32) * MOE_TM, side="right")
    used = (tile_cls < MOE_CLASSES).astype(jnp.int32)
    tile_cls = jnp.minimum(tile_cls, MOE_CLASSES - 1).astype(jnp.int32)
    fresh = jnp.concatenate([jnp.ones((1,), jnp.int32),
                             (tile_cls[1:] != tile_cls[:-1]).astype(jnp.int32)])
    lo, hi = _pair_table()
    return pos.astype(jnp.int32), jnp.asarray(lo)[tile_cls], jnp.asarray(hi)[tile_cls], fresh, used


def _moe_layer(x, routed, buf, w1, w3, w2, ln_g, ln_b, layer):
    n_tiles = buf.shape[0] // (MOE_TM * FEATURE_ROWS)
    pos, ea, eb, fresh, used = _sort_plan(routed[0].astype(jnp.int32), n_tiles)
    gate_rows = jnp.pad(routed[1:1 + TOP_K].T, ((0, 0), (0, V7X_LANES - TOP_K)))
    buf, gates = _dispatch(pos, x, gate_rows, buf)
    ys = _expert_ffn(ea, eb, fresh, used, buf, gates, w1, w3, w2, layer)
    return _combine(pos, ys, x, ln_g, ln_b), buf


PROJ_TM = 512


def _proj_t_kernel(x_ref, wt_ref, o_ref, *, scale):
    acc = lax.dot_general(wt_ref[...], x_ref[...].astype(BF16), _NT, preferred_element_type=F32)
    o_ref[...] = (acc * scale).astype(o_ref.dtype)


def _project_t(x, w_t, scale):
    b, s, _ = x.shape
    n = w_t.shape[0]
    return pl.pallas_call(
        functools.partial(_proj_t_kernel, scale=scale),
        out_shape=jax.ShapeDtypeStruct((b, n, s), BF16),
        grid=(b, s // PROJ_TM),
        in_specs=[pl.BlockSpec((None, PROJ_TM, D_MODEL), lambda bi, i: (bi, i, 0)),
                  _const_spec(w_t.shape)],
        out_specs=pl.BlockSpec((None, n, PROJ_TM), lambda bi, i: (bi, 0, i)),
        compiler_params=_params("arbitrary", "arbitrary"),
        name="project_t",
    )(x, w_t)


def _kv_kernel(x_ref, wk_ref, wvt_ref, k_ref, vt_ref, kmean_ref):
    xb = x_ref[...].astype(BF16)
    k = jnp.dot(xb, wk_ref[...], preferred_element_type=F32)
    k_ref[...] = k.astype(BF16)
    for j in range(PROJ_TM // MOBA_BLOCK):
        kmean_ref[j] = jnp.mean(k[j * MOBA_BLOCK:(j + 1) * MOBA_BLOCK], axis=0, keepdims=True)
    vt_ref[...] = lax.dot_general(wvt_ref[...], xb, _NT, preferred_element_type=F32).astype(BF16)


def _project_kv(x, w_k, w_v_t):
    b, s, _ = x.shape
    blocks_per_tile = PROJ_TM // MOBA_BLOCK
    tiles = s // PROJ_TM
    return pl.pallas_call(
        _kv_kernel,
        out_shape=(jax.ShapeDtypeStruct((b, s, D_MODEL), BF16),
                   jax.ShapeDtypeStruct((b, D_MODEL, s), BF16),
                   jax.ShapeDtypeStruct((b * s // MOBA_BLOCK, 1, D_MODEL), F32)),
        grid=(b, tiles),
        in_specs=[pl.BlockSpec((None, PROJ_TM, D_MODEL), lambda bi, i: (bi, i, 0)),
                  _const_spec(w_k.shape), _const_spec(w_v_t.shape)],
        out_specs=(pl.BlockSpec((None, PROJ_TM, D_MODEL), lambda bi, i: (bi, i, 0)),
                   pl.BlockSpec((None, D_MODEL, PROJ_TM), lambda bi, i: (bi, 0, i)),
                   pl.BlockSpec((blocks_per_tile, 1, D_MODEL), lambda bi, i: (bi * tiles + i, 0, 0))),
        compiler_params=_params("arbitrary", "arbitrary"),
        name="project_kv",
    )(x, w_k, w_v_t)


def _t5_bucket_np(dist):
    n = np.maximum(dist, 0)
    max_exact = N_BUCKETS // 2
    nf = np.maximum(n, 1).astype(np.float32)
    large = max_exact + (np.log(nf / np.float32(max_exact)) / np.float32(math.log(MAX_DISTANCE / max_exact))
                         * np.float32(N_BUCKETS - max_exact)).astype(np.int32)
    large = np.minimum(large, N_BUCKETS - 1)
    return np.where(n < max_exact, n, large).astype(np.int32)


def _bucket_tables():
    kpos = np.arange(MOBA_BLOCK)[:, None]
    qpos = np.arange(MOBA_BLOCK)[None, :]
    own = np.where(qpos >= kpos, _t5_bucket_np(qpos - kpos), -1)
    prev = _t5_bucket_np(qpos + MOBA_BLOCK - kpos)
    return np.stack([own, prev]).astype(np.int32)


def _bias_kernel(rel_ref, bucket_ref, o_ref):
    h = pl.program_id(0)
    far = rel_ref[N_BUCKETS - 1, h]
    for t in range(2):
        bucket = bucket_ref[t]
        acc = jnp.zeros(bucket.shape, F32)
        for b in range(N_BUCKETS):
            acc = jnp.where(bucket == b, rel_ref[b, h], acc)
        o_ref[t] = jnp.where(bucket < 0, NEG_INF, (acc - far) * LOG2_E)


def _bias_tables(rel_bias):
    buckets = jnp.asarray(_bucket_tables())
    return pl.pallas_call(
        _bias_kernel,
        out_shape=jax.ShapeDtypeStruct((N_HEADS, 2, MOBA_BLOCK, MOBA_BLOCK), F32),
        grid=(N_HEADS,),
        in_specs=[pl.BlockSpec(memory_space=pltpu.SMEM), _const_spec(buckets.shape)],
        out_specs=pl.BlockSpec((None, 2, MOBA_BLOCK, MOBA_BLOCK), lambda h: (h, 0, 0, 0)),
        compiler_params=_params("arbitrary"),
        name="rel_bias_tables",
    )(rel_bias, buckets)


ATTN_HEADS = 8
ATTN_WIDTH = ATTN_HEADS * HEAD_DIM
HEADS_PER_LANE_TILE = V7X_LANES // HEAD_DIM
FAR_TILES_PER_STEP = 2
QK_LOOKAHEAD = 6
LOG2_E = math.log2(math.e)


def _attn_kernel(qt_ref, k_ref, vt_ref, kmean_ref, bias_ref, o_ref,
                 sel_ref, ml_ref, acc_ref, qp_ref, sc_ref):
    n_blocks = kmean_ref.shape[0]
    blk = lax.broadcasted_iota(jnp.int32, (n_blocks, MOBA_BLOCK), 0)
    causal = (lax.broadcasted_iota(jnp.int32, (MOBA_BLOCK, MOBA_BLOCK), 0)
              <= lax.broadcasted_iota(jnp.int32, (MOBA_BLOCK, MOBA_BLOCK), 1))
    ones_rows = jnp.ones((BF16_SUBLANE_TILE, MOBA_BLOCK), BF16)

    def lane_slab(h):
        return slice((h // HEADS_PER_LANE_TILE) * V7X_LANES, (h // HEADS_PER_LANE_TILE + 1) * V7X_LANES)

    def padded_q(h, q0):
        qh = qt_ref[h * HEAD_DIM:(h + 1) * HEAD_DIM, pl.ds(q0, MOBA_BLOCK)]
        pieces = [jnp.zeros_like(qh)] * HEADS_PER_LANE_TILE
        pieces[h % HEADS_PER_LANE_TILE] = qh
        return jnp.concatenate(pieces, axis=0)

    def scores(h, j):
        k0 = pl.multiple_of(j * MOBA_BLOCK, MOBA_BLOCK)
        kj = k_ref[pl.ds(k0, MOBA_BLOCK), lane_slab(h)]
        return jnp.dot(kj, qp_ref[h], preferred_element_type=F32)

    def weights_and_values(h, j, s, bias, sel):
        if bias is not None:
            keep = causal if sel is None else sel
            s = jnp.where(keep, s, NEG_INF) + bias[...]
        m_loc = jnp.max(s, axis=0, keepdims=True)
        if sel is None:
            shift = m_loc
        else:
            shift = jnp.where(sel, m_loc, -NEG_INF)
            m_loc = jnp.where(sel, m_loc, NEG_INF)
        p = jnp.exp2(s - shift).astype(BF16)
        k0 = pl.multiple_of(j * MOBA_BLOCK, MOBA_BLOCK)
        vtj = vt_ref[h * HEAD_DIM:(h + 1) * HEAD_DIM, pl.ds(k0, MOBA_BLOCK)]
        pv = jnp.dot(jnp.concatenate([vtj, ones_rows], axis=0), p, preferred_element_type=F32)
        return m_loc, pv[HEAD_DIM:HEAD_DIM + 1], pv[:HEAD_DIM]

    def run_tiles(tiles, carried_in, prefetch):
        n = len(tiles)
        s = [None] * n
        out = [None] * n
        for t in range(QK_LOOKAHEAD if carried_in else 0, n + QK_LOOKAHEAD):
            if t < n:
                h, j, _, _ = tiles[t]
                s[t] = scores(h, j)
            else:
                h, j = prefetch[t - n]
                sc_ref[t - n] = scores(h, j)
            d = t - QK_LOOKAHEAD
            if d >= 0:
                h, j, bias, sel = tiles[d]
                sd = sc_ref[d] if carried_in and d < QK_LOOKAHEAD else s[d]
                out[d] = weights_and_values(h, j, sd, bias, sel)
                s[d] = None
        return out

    def far_tile_ids(t):
        return [(h, t * FAR_TILES_PER_STEP + u) for h in range(ATTN_HEADS) for u in range(FAR_TILES_PER_STEP)]

    def merge(h, parts, first):
        m_new = parts[0][0]
        for m_loc, _, _ in parts[1:]:
            m_new = jnp.maximum(m_new, m_loc)
        if first:
            l_new = jnp.zeros_like(m_new)
            acc_new = jnp.zeros((HEAD_DIM, MOBA_BLOCK), F32)
        else:
            m_old = ml_ref[h, 0:1, :]
            m_new = jnp.maximum(m_new, m_old)
            alpha = jnp.exp2(m_old - m_new)
            l_new = alpha * ml_ref[h, 1:2, :]
            acc_new = alpha * acc_ref[h]
        for m_loc, l_loc, pv in parts:
            w = jnp.exp2(m_loc - m_new)
            l_new = l_new + w * l_loc
            acc_new = acc_new + w * pv
        ml_ref[h, 0:1, :] = m_new
        ml_ref[h, 1:2, :] = l_new
        acc_ref[h] = acc_new

    def q_block(i, _):
        q0 = pl.multiple_of(i * MOBA_BLOCK, MOBA_BLOCK)
        jp = jnp.maximum(i - 1, 0)
        n_far = jnp.maximum(i - 1, 0)
        for h in range(ATTN_HEADS):
            qp_ref[h] = padded_q(h, q0)

        for h in range(ATTN_HEADS):
            gate = jnp.dot(kmean_ref[:, lane_slab(h)].astype(BF16), qp_ref[h], preferred_element_type=F32)
            gate = jnp.where(blk < i, gate, NEG_INF)
            sel = jnp.zeros(gate.shape, F32)
            for r in range(MOBA_TOPK):
                top = jnp.max(gate, axis=0, keepdims=True)
                first = jnp.min(jnp.where(gate == top, blk, n_blocks), axis=0, keepdims=True)
                hit = blk == first
                sel = jnp.where(hit & (r < i), 1.0, sel)
                gate = jnp.where(hit, -jnp.inf, gate)
            sel_ref[h] = sel

        tiles = []
        for h in range(ATTN_HEADS):
            tiles.append((h, i, bias_ref.at[h, 0], None))
            tiles.append((h, jp, bias_ref.at[h, 1], sel_ref[h, pl.ds(jp, 1), :] > 0.5))
        last_far = jnp.maximum(n_far - 1, 0)

        def prefetch_ids(t):
            return [(h, jnp.minimum(j, last_far)) for h, j in far_tile_ids(t)[:QK_LOOKAHEAD]]

        parts = run_tiles(tiles, carried_in=False, prefetch=prefetch_ids(0))
        for h in range(ATTN_HEADS):
            merge(h, parts[2 * h:2 * h + 2], first=True)

        def far_step(t, _):
            tiles = []
            for h, j in far_tile_ids(t):
                jc = jnp.minimum(j, last_far)
                sel = (sel_ref[h, pl.ds(jc, 1), :] > 0.5) & (j < n_far)
                tiles.append((h, jc, None, sel))
            parts = run_tiles(tiles, carried_in=True, prefetch=prefetch_ids(t + 1))
            for h in range(ATTN_HEADS):
                merge(h, parts[h * FAR_TILES_PER_STEP:(h + 1) * FAR_TILES_PER_STEP], first=False)
            return 0

        lax.fori_loop(0, (n_far + FAR_TILES_PER_STEP - 1) // FAR_TILES_PER_STEP, far_step, 0)

        for slab in range(ATTN_HEADS // HEADS_PER_LANE_TILE):
            heads = range(slab * HEADS_PER_LANE_TILE, (slab + 1) * HEADS_PER_LANE_TILE)
            out_t = jnp.concatenate([acc_ref[h] / ml_ref[h, 1:2, :] for h in heads], axis=0)
            o_ref[pl.ds(q0, MOBA_BLOCK), slab * V7X_LANES:(slab + 1) * V7X_LANES] = out_t.T.astype(o_ref.dtype)
        return 0

    lax.fori_loop(0, n_blocks, q_block, 0)


def _moba_attention(qt, k, vt, kmean, bias):
    b, _, s = qt.shape
    n_blocks = s // MOBA_BLOCK
    single = dict(pipeline_mode=pl.Buffered(1))
    t_spec = pl.BlockSpec((None, ATTN_WIDTH, s), lambda bi, hi: (bi, hi, 0), **single)
    return pl.pallas_call(
        _attn_kernel,
        out_shape=jax.ShapeDtypeStruct((b, s, D_MODEL), BF16),
        grid=(b, N_HEADS // ATTN_HEADS),
        in_specs=[t_spec,
                  pl.BlockSpec((None, s, ATTN_WIDTH), lambda bi, hi: (bi, 0, hi), **single),
                  t_spec,
                  pl.BlockSpec((None, n_blocks, ATTN_WIDTH), lambda bi, hi: (bi, 0, hi)),
                  pl.BlockSpec((ATTN_HEADS, 2, MOBA_BLOCK, MOBA_BLOCK), lambda bi, hi: (hi, 0, 0, 0), **single)],
        out_specs=pl.BlockSpec((None, s, ATTN_WIDTH), lambda bi, hi: (bi, 0, hi)),
        scratch_shapes=[pltpu.VMEM((ATTN_HEADS, n_blocks, MOBA_BLOCK), F32),
                        pltpu.VMEM((ATTN_HEADS, 2, MOBA_BLOCK), F32),
                        pltpu.VMEM((ATTN_HEADS, HEAD_DIM, MOBA_BLOCK), F32),
                        pltpu.VMEM((ATTN_HEADS, V7X_LANES, MOBA_BLOCK), BF16),
                        pltpu.VMEM((QK_LOOKAHEAD, MOBA_BLOCK, MOBA_BLOCK), F32)],
        compiler_params=_params("arbitrary", "arbitrary"),
        name="moba_attention",
    )(qt, k, vt, kmean, bias)


OPROJ_TM = 512


def _oproj_kernel(a_ref, w_ref, x_ref, lng_ref, lnb_ref, wr_ref, br_ref, o_ref, route_ref):
    h = jnp.dot(a_ref[...], w_ref[...], preferred_element_type=F32)
    out = _layer_norm(DEEPNORM_ALPHA * x_ref[...] + h, lng_ref[...], lnb_ref[...])
    o_ref[...] = out
    route_ref[...] = _route(out, wr_ref[...], br_ref[...])


def _out_project(a, w_o, x, ln_g, ln_b, wr_t, br):
    t = x.shape[0]
    row_spec = pl.BlockSpec((OPROJ_TM, D_MODEL), lambda i: (i, 0))
    return pl.pallas_call(
        _oproj_kernel,
        out_shape=(jax.ShapeDtypeStruct((t, D_MODEL), F32), jax.ShapeDtypeStruct((ROUTER_ROWS, t), F32)),
        grid=(t // OPROJ_TM,),
        in_specs=[row_spec, _const_spec(w_o.shape), row_spec,
                  _const_spec(ln_g.shape), _const_spec(ln_b.shape),
                  _const_spec(wr_t.shape), _const_spec(br.shape)],
        out_specs=(row_spec, pl.BlockSpec((ROUTER_ROWS, OPROJ_TM), lambda i: (0, i))),
        compiler_params=_params("arbitrary"),
        name="out_project",
    )(a, w_o, x, ln_g, ln_b, wr_t, br)


def kernel(x, ln_g, ln_b, a_w_in, a_v_g, a_v_b, a_w_s, a_b_s, a_w_out, w_k, w_v, b_w_q, b_w_o,
           rel_bias, w_router, b_router, e_w1, e_w3, e_w2):
    batch, seq, _ = x.shape
    t = batch * seq
    n_blocks = seq // MOBA_BLOCK
    xt = x.reshape(t, D_MODEL)
    wr_t = w_router.T.astype(BF16)
    br = b_router.reshape(N_EXPERTS, 1)
    n_tiles = t // MOE_TM + MOE_CLASSES
    sorted_buf = jnp.zeros((n_tiles * MOE_TM * FEATURE_ROWS, V7X_LANES), F32)

    def ln_rows(l, j):
        return ln_g[l, j].reshape(1, D_MODEL), ln_b[l, j].reshape(1, D_MODEL)

    k = vt = kmean = bias = None
    for l in range(DEPTH):
        g0, b0 = ln_rows(l, 0)
        if l < N_A_LAYERS:
            xt, routed = _gmlp_layer(xt, a_w_in[l].astype(BF16),
                                     a_v_g[l].reshape(1, GMLP_HALF), a_v_b[l].reshape(1, GMLP_HALF),
                                     a_w_s[l], a_b_s[l].T, a_w_out[l].astype(BF16), g0, b0, wr_t, br)
        else:
            x3 = xt.reshape(batch, seq, D_MODEL)
            if l == N_A_LAYERS:
                k, vt, kmean = _project_kv(x3, w_k.astype(BF16), w_v.T.astype(BF16))
                kmean = kmean.reshape(batch, n_blocks, D_MODEL)
                bias = _bias_tables(rel_bias)
            j = l - N_A_LAYERS
            qt = _project_t(x3, b_w_q[j].T.astype(BF16), HEAD_DIM ** -0.5 * LOG2_E)
            a = _moba_attention(qt, k, vt, kmean, bias)
            xt, routed = _out_project(a.reshape(t, D_MODEL), b_w_o[j].astype(BF16), xt, g0, b0, wr_t, br)
        g1, b1 = ln_rows(l, 1)
        xt, sorted_buf = _moe_layer(xt, routed, sorted_buf, e_w1, e_w3, e_w2, g1, b1, layer=l)
    return xt.reshape(batch, seq, D_MODEL)
```

```python
import functools
import math

import numpy as np
import jax
import jax.numpy as jnp
from jax import lax
from jax.experimental import pallas as pl
from jax.experimental.pallas import tpu as pltpu

D_MODEL = 1024
DEPTH = 4
N_A_LAYERS = DEPTH // 2
GMLP_HALF = 3 * D_MODEL
GMLP_GROUPS = 8
GMLP_GROUP_WIDTH = GMLP_HALF // GMLP_GROUPS
GMLP_CHUNK = 128
N_HEADS = 16
HEAD_DIM = D_MODEL // N_HEADS
MOBA_BLOCK = 256
MOBA_TOPK = 3
N_BUCKETS = 32
MAX_DISTANCE = 128
N_EXPERTS = 16
N_GROUPS = 4
EXPERTS_PER_GROUP = N_EXPERTS // N_GROUPS
TOP_K = 2
MOE_PAIRS = EXPERTS_PER_GROUP * (EXPERTS_PER_GROUP - 1) // 2
MOE_CLASSES = N_GROUPS * MOE_PAIRS
ROUTER_ROWS = 8
D_EXPERT = D_MODEL // 4
DEEPNORM_ALPHA = (2 * DEPTH) ** 0.25
LN_EPS = 1e-5
NEG_INF = -1e30

V7X_VMEM_LIMIT_BYTES = 56 * 1024 * 1024
V7X_LANES = 128
BF16_SUBLANE_TILE = 16

BF16 = jnp.bfloat16
F32 = jnp.float32

_NT = (((1,), (1,)), ((), ()))


def _params(*semantics):
    return pltpu.CompilerParams(dimension_semantics=semantics,
                                vmem_limit_bytes=V7X_VMEM_LIMIT_BYTES)


def _const_spec(shape):
    zeros = (0,) * len(shape)
    return pl.BlockSpec(shape, lambda *_: zeros, pipeline_mode=pl.Buffered(1))


def _layer_norm(y, g, b):
    mu = jnp.mean(y, axis=-1, keepdims=True)
    d = y - mu
    var = jnp.mean(d * d, axis=-1, keepdims=True)
    return d * lax.rsqrt(var + LN_EPS) * g + b


GMLP_TM = 512
GMLP_GROUPS_PER_STEP = 2


def _gmlp_kernel(x_ref, w_in_ref, vg_ref, vb_ref, ws_ref, bs_ref, w_out_ref,
                 lng_ref, lnb_ref, wr_ref, br_ref, o_ref, route_ref, vn_ref, sv_ref):
    x = x_ref[...]
    xb = x.astype(BF16)
    v = jax.nn.gelu(jnp.dot(xb, w_in_ref[:, GMLP_HALF:], preferred_element_type=F32))
    vn_ref[...] = _layer_norm(v, vg_ref[...], vb_ref[...]).astype(BF16)

    row = lax.broadcasted_iota(jnp.int32, (GMLP_CHUNK, GMLP_CHUNK), 0)
    col = lax.broadcasted_iota(jnp.int32, (GMLP_CHUNK, GMLP_CHUNK), 1)
    causal = row >= col
    width = GMLP_GROUPS_PER_STEP * GMLP_GROUP_WIDTH
    acc = jnp.zeros((GMLP_TM, D_MODEL), F32)
    for step in range(GMLP_GROUPS // GMLP_GROUPS_PER_STEP):
        lo = step * width
        u = jax.nn.gelu(jnp.dot(xb, w_in_ref[:, lo:lo + width], preferred_element_type=F32))
        for gi in range(GMLP_GROUPS_PER_STEP):
            g = step * GMLP_GROUPS_PER_STEP + gi
            w_s = jnp.where(causal, ws_ref[g], 0.0).astype(BF16)
            b_s = bs_ref[:, g:g + 1]
            glo = g * GMLP_GROUP_WIDTH
            for c in range(GMLP_TM // GMLP_CHUNK):
                rows = slice(c * GMLP_CHUNK, (c + 1) * GMLP_CHUNK)
                vc = vn_ref[rows, glo:glo + GMLP_GROUP_WIDTH]
                sv_ref[rows, gi * GMLP_GROUP_WIDTH:(gi + 1) * GMLP_GROUP_WIDTH] = (
                    jnp.dot(w_s, vc, preferred_element_type=F32) + b_s)
        gated = (u * sv_ref[...]).astype(BF16)
        acc = acc + jnp.dot(gated, w_out_ref[lo:lo + width, :], preferred_element_type=F32)
    out = _layer_norm(DEEPNORM_ALPHA * x + acc, lng_ref[...], lnb_ref[...])
    o_ref[...] = out
    route_ref[...] = _route(out, wr_ref[...], br_ref[...])


def _gmlp_layer(x, w_in, v_g, v_b, w_s, b_s_t, w_out, ln_g, ln_b, wr_t, br):
    t = x.shape[0]
    row_spec = pl.BlockSpec((GMLP_TM, D_MODEL), lambda i: (i, 0))
    return pl.pallas_call(
        _gmlp_kernel,
        out_shape=(jax.ShapeDtypeStruct((t, D_MODEL), F32), jax.ShapeDtypeStruct((ROUTER_ROWS, t), F32)),
        grid=(t // GMLP_TM,),
        in_specs=[row_spec,
                  _const_spec(w_in.shape), _const_spec(v_g.shape), _const_spec(v_b.shape),
                  _const_spec(w_s.shape), _const_spec(b_s_t.shape), _const_spec(w_out.shape),
                  _const_spec(ln_g.shape), _const_spec(ln_b.shape),
                  _const_spec(wr_t.shape), _const_spec(br.shape)],
        out_specs=(row_spec, pl.BlockSpec((ROUTER_ROWS, GMLP_TM), lambda i: (0, i))),
        scratch_shapes=[pltpu.VMEM((GMLP_TM, GMLP_HALF), BF16),
                        pltpu.VMEM((GMLP_TM, GMLP_GROUPS_PER_STEP * GMLP_GROUP_WIDTH), F32)],
        compiler_params=_params("arbitrary"),
        name="gmlp_layer",
    )(x, w_in, v_g, v_b, w_s, b_s_t, w_out, ln_g, ln_b, wr_t, br)


def _route(x, wr_t, br):
    logits = lax.dot_general(wr_t, x.astype(BF16), _NT, preferred_element_type=F32) + br
    m = jnp.max(logits, axis=0, keepdims=True)
    ex = jnp.exp(logits - m)
    probs = ex / jnp.sum(ex, axis=0, keepdims=True)
    p = [probs[e:e + 1, :] for e in range(N_EXPERTS)]

    scores = []
    for g in range(N_GROUPS):
        q = p[g * EXPERTS_PER_GROUP:(g + 1) * EXPERTS_PER_GROUP]
        best = None
        for a in range(EXPERTS_PER_GROUP):
            for b in range(a + 1, EXPERTS_PER_GROUP):
                s = q[a] + q[b]
                best = s if best is None else jnp.maximum(best, s)
        scores.append(best)
    best_score = scores[0]
    gsel = jnp.zeros_like(best_score, dtype=jnp.int32)
    for g in range(1, N_GROUPS):
        better = scores[g] > best_score
        best_score = jnp.where(better, scores[g], best_score)
        gsel = jnp.where(better, g, gsel)

    slot_chosen = [None] * EXPERTS_PER_GROUP
    slot_gate = [None] * EXPERTS_PER_GROUP
    for e in range(N_EXPERTS):
        g, k = divmod(e, EXPERTS_PER_GROUP)
        rank = jnp.zeros_like(gsel)
        for e2 in range(g * EXPERTS_PER_GROUP, (g + 1) * EXPERTS_PER_GROUP):
            if e2 == e:
                continue
            ahead = (p[e2] > p[e]) | ((p[e2] == p[e]) & (e2 < e))
            rank = rank + ahead.astype(jnp.int32)
        chosen = (gsel == g) & (rank < TOP_K)
        gate = jnp.where(chosen, p[e] / best_score, 0.0)
        slot_chosen[k] = chosen if slot_chosen[k] is None else slot_chosen[k] | chosen
        slot_gate[k] = gate if slot_gate[k] is None else slot_gate[k] + gate
    c0, c1, c2, c3 = slot_chosen
    lo = jnp.where(c0, 0, jnp.where(c1, 1, 2))
    hi = jnp.where(c3, 3, jnp.where(c2, 2, 1))
    w_lo = jnp.where(c0, slot_gate[0], jnp.where(c1, slot_gate[1], slot_gate[2]))
    w_hi = jnp.where(c3, slot_gate[3], jnp.where(c2, slot_gate[2], slot_gate[1]))
    pair = jnp.right_shift(lo * (7 - lo), 1) + hi - lo - 1
    cls = gsel * MOE_PAIRS + pair
    zero = jnp.zeros_like(w_lo)
    return jnp.concatenate([cls.astype(F32), w_lo, w_hi] + [zero] * (ROUTER_ROWS - 3), axis=0)


MOE_TM = 256
FEATURE_ROWS = D_MODEL // V7X_LANES
DISPATCH_TD = 512
DMA_UNROLL = 8


def _token_rows(ref, n_tokens):
    return jnp.concatenate(
        [ref[pl.ds(c, n_tokens, stride=FEATURE_ROWS), :] for c in range(FEATURE_ROWS)], axis=1)


def _store_token_rows(ref, value, n_tokens):
    for c in range(FEATURE_ROWS):
        ref[pl.ds(c, n_tokens, stride=FEATURE_ROWS), :] = value[:, c * V7X_LANES:(c + 1) * V7X_LANES]


def _dispatch_kernel(pos_ref, x_ref, gate_ref, buf_in_ref, buf_ref, gate_tbl_ref, stage_ref, sems):
    del buf_in_ref
    i = pl.program_id(0)
    n = pl.num_programs(0)
    slot = i % 2
    stage = stage_ref.at[slot]
    rows = DISPATCH_TD * FEATURE_ROWS

    @pl.when(i == 0)
    def _():
        gate_tbl_ref[...] = jnp.zeros_like(gate_tbl_ref)

    def wait_slot(s):
        pltpu.make_async_copy(stage_ref.at[s], buf_ref.at[pl.ds(0, rows)], sems.at[s]).wait()

    @pl.when(i >= 2)
    def _():
        wait_slot(slot)

    _store_token_rows(stage, x_ref[...], DISPATCH_TD)

    def issue(r, _):
        p = pos_ref[i * DISPATCH_TD + r]
        dst = pl.multiple_of(p * FEATURE_ROWS, FEATURE_ROWS)
        src = pl.multiple_of(r * FEATURE_ROWS, FEATURE_ROWS)
        pltpu.make_async_copy(stage.at[pl.ds(src, FEATURE_ROWS)], buf_ref.at[pl.ds(dst, FEATURE_ROWS)],
                              sems.at[slot]).start()
        gate_tbl_ref[pl.ds(p, 1), :] = gate_ref[pl.ds(r, 1), :]
        return 0

    lax.fori_loop(0, DISPATCH_TD, issue, 0, unroll=DMA_UNROLL)

    @pl.when(i == n - 1)
    def _():
        wait_slot(slot)

        @pl.when(n > 1)
        def _():
            wait_slot(1 - slot)


def _dispatch(pos, x, gates, buf):
    t = x.shape[0]
    n_slots = buf.shape[0] // FEATURE_ROWS
    return pl.pallas_call(
        _dispatch_kernel,
        out_shape=(jax.ShapeDtypeStruct(buf.shape, buf.dtype),
                   jax.ShapeDtypeStruct((n_slots, V7X_LANES), F32)),
        grid_spec=pltpu.PrefetchScalarGridSpec(
            num_scalar_prefetch=1,
            grid=(t // DISPATCH_TD,),
            in_specs=[pl.BlockSpec((DISPATCH_TD, D_MODEL), lambda i, pos: (i, 0)),
                      pl.BlockSpec((DISPATCH_TD, V7X_LANES), lambda i, pos: (i, 0)),
                      pl.BlockSpec(memory_space=pl.ANY)],
            out_specs=(pl.BlockSpec(memory_space=pl.ANY),
                       pl.BlockSpec((n_slots, V7X_LANES), lambda i, pos: (0, 0))),
            scratch_shapes=[pltpu.VMEM((2, DISPATCH_TD * FEATURE_ROWS, V7X_LANES), F32),
                            pltpu.SemaphoreType.DMA((2,))]),
        input_output_aliases={3: 0},
        compiler_params=_params("arbitrary"),
        name="moe_dispatch",
    )(pos, x, gates, buf)


def _expert_kernel(ea_ref, eb_ref, fresh_ref, used_ref, xs_ref, gate_ref,
                   w1a_ref, w1b_ref, w3a_ref, w3b_ref, w2a_ref, w2b_ref, ys_ref,
                   w1_s, w3_s, w2_s):
    del ea_ref, eb_ref
    i = pl.program_id(0)

    @pl.when(fresh_ref[i] == 1)
    def _():
        for e, (w1, w3, w2) in enumerate(((w1a_ref, w3a_ref, w2a_ref), (w1b_ref, w3b_ref, w2b_ref))):
            w1_s[e] = w1[...].astype(BF16)
            w3_s[e] = w3[...].astype(BF16)
            w2_s[e] = w2[...].astype(BF16)

    @pl.when(used_ref[i] == 1)
    def _():
        xb = _token_rows(xs_ref, MOE_TM).astype(BF16)
        gates = gate_ref[...]
        y = None
        for e in range(TOP_K):
            h1 = jnp.dot(xb, w1_s[e], preferred_element_type=F32)
            h3 = jnp.dot(xb, w3_s[e], preferred_element_type=F32)
            h = (jax.nn.silu(h1) * h3 * gates[:, e:e + 1]).astype(BF16)
            ye = jnp.dot(h, w2_s[e], preferred_element_type=F32)
            y = ye if y is None else y + ye
        _store_token_rows(ys_ref, y, MOE_TM)

    @pl.when(used_ref[i] == 0)
    def _():
        ys_ref[...] = jnp.zeros_like(ys_ref)


def _expert_ffn(ea, eb, fresh, used, xs, gates, w1, w3, w2, layer):
    n_tiles = ea.shape[0]

    def w_spec(shape, which):
        return pl.BlockSpec((None, None) + shape,
                            lambda i, ea, eb, fresh, used: (layer, (ea, eb)[which][i], 0, 0))

    up, down = (D_MODEL, D_EXPERT), (D_EXPERT, D_MODEL)
    return pl.pallas_call(
        _expert_kernel,
        out_shape=jax.ShapeDtypeStruct((n_tiles * MOE_TM * FEATURE_ROWS, V7X_LANES), F32),
        grid_spec=pltpu.PrefetchScalarGridSpec(
            num_scalar_prefetch=4,
            grid=(n_tiles,),
            in_specs=[pl.BlockSpec((MOE_TM * FEATURE_ROWS, V7X_LANES), lambda i, *_: (i, 0)),
                      pl.BlockSpec((MOE_TM, V7X_LANES), lambda i, *_: (i, 0)),
                      w_spec(up, 0), w_spec(up, 1), w_spec(up, 0), w_spec(up, 1),
                      w_spec(down, 0), w_spec(down, 1)],
            out_specs=pl.BlockSpec((MOE_TM * FEATURE_ROWS, V7X_LANES), lambda i, *_: (i, 0)),
            scratch_shapes=[pltpu.VMEM((TOP_K,) + up, BF16), pltpu.VMEM((TOP_K,) + up, BF16),
                            pltpu.VMEM((TOP_K,) + down, BF16)]),
        compiler_params=_params("arbitrary"),
        name="moe_experts",
    )(ea, eb, fresh, used, xs, gates, w1, w1, w3, w3, w2, w2)


def _combine_kernel(pos_ref, ys_ref, x_ref, lng_ref, lnb_ref, o_ref, stage_ref, sems):
    i = pl.program_id(0)
    n = pl.num_programs(0)
    slot = i % 2
    rows = DISPATCH_TD * FEATURE_ROWS

    def issue_step(step, s):
        def issue(r, _):
            src = pl.multiple_of(pos_ref[step * DISPATCH_TD + r] * FEATURE_ROWS, FEATURE_ROWS)
            dst = pl.multiple_of(r * FEATURE_ROWS, FEATURE_ROWS)
            pltpu.make_async_copy(ys_ref.at[pl.ds(src, FEATURE_ROWS)],
                                  stage_ref.at[s, pl.ds(dst, FEATURE_ROWS)], sems.at[s]).start()
            return 0
        lax.fori_loop(0, DISPATCH_TD, issue, 0, unroll=DMA_UNROLL)

    @pl.when(i == 0)
    def _():
        issue_step(0, 0)

    @pl.when(i + 1 < n)
    def _():
        issue_step(i + 1, 1 - slot)

    pltpu.make_async_copy(ys_ref.at[pl.ds(0, rows)], stage_ref.at[slot], sems.at[slot]).wait()

    f = _token_rows(stage_ref.at[slot], DISPATCH_TD)
    o_ref[...] = _layer_norm(DEEPNORM_ALPHA * x_ref[...] + f, lng_ref[...], lnb_ref[...])


def _combine(pos, ys, x, ln_g, ln_b):
    t = x.shape[0]
    row_spec = pl.BlockSpec((DISPATCH_TD, D_MODEL), lambda i, pos: (i, 0))
    return pl.pallas_call(
        _combine_kernel,
        out_shape=jax.ShapeDtypeStruct((t, D_MODEL), F32),
        grid_spec=pltpu.PrefetchScalarGridSpec(
            num_scalar_prefetch=1,
            grid=(t // DISPATCH_TD,),
            in_specs=[pl.BlockSpec(memory_space=pl.ANY), row_spec,
                      pl.BlockSpec((1, D_MODEL), lambda i, pos: (0, 0)),
                      pl.BlockSpec((1, D_MODEL), lambda i, pos: (0, 0))],
            out_specs=row_spec,
            scratch_shapes=[pltpu.VMEM((2, DISPATCH_TD * FEATURE_ROWS, V7X_LANES), F32),
                            pltpu.SemaphoreType.DMA((2,))]),
        compiler_params=_params("arbitrary"),
        name="moe_combine",
    )(pos, ys, x, ln_g, ln_b)


def _pair_table():
    lo, hi = [], []
    for g in range(N_GROUPS):
        for a in range(EXPERTS_PER_GROUP):
            for b in range(a + 1, EXPERTS_PER_GROUP):
                lo.append(g * EXPERTS_PER_GROUP + a)
                hi.append(g * EXPERTS_PER_GROUP + b)
    return np.asarray(lo, np.int32), np.asarray(hi, np.int32)


def _sort_plan(cls, n_tiles):
    onehot = (cls[:, None] == jnp.arange(MOE_CLASSES, dtype=jnp.int32)[None, :]).astype(jnp.int32)
    running = jnp.cumsum(onehot, axis=0)
    counts = running[-1]
    padded = (counts + MOE_TM - 1) // MOE_TM * MOE_TM
    ends = jnp.cumsum(padded)
    rank = jnp.sum(running * onehot, axis=1) - 1
    pos = jnp.sum(onehot * (ends - padded)[None, :], axis=1) + rank
    tile_cls = jnp.searchsorted(ends, jnp.arange(n_tiles, dtype=jnp.int32) * MOE_TM, side="right")
    used = (tile_cls < MOE_CLASSES).astype(jnp.int32)
    tile_cls = jnp.minimum(tile_cls, MOE_CLASSES - 1).astype(jnp.int32)
    fresh = jnp.concatenate([jnp.ones((1,), jnp.int32),
                             (tile_cls[1:] != tile_cls[:-1]).astype(jnp.int32)])
    lo, hi = _pair_table()
    return pos.astype(jnp.int32), jnp.asarray(lo)[tile_cls], jnp.asarray(hi)[tile_cls], fresh, used


def _moe_layer(x, routed, buf, w1, w3, w2, ln_g, ln_b, layer):
    n_tiles = buf.shape[0] // (MOE_TM * FEATURE_ROWS)
    pos, ea, eb, fresh, used = _sort_plan(routed[0].astype(jnp.int32), n_tiles)
    gate_rows = jnp.pad(routed[1:1 + TOP_K].T, ((0, 0), (0, V7X_LANES - TOP_K)))
    buf, gates = _dispatch(pos, x, gate_rows, buf)
    ys = _expert_ffn(ea, eb, fresh, used, buf, gates, w1, w3, w2, layer)
    return _combine(pos, ys, x, ln_g, ln_b), buf


PROJ_TM = 512


def _proj_t_kernel(x_ref, wt_ref, o_ref, *, scale):
    acc = lax.dot_general(wt_ref[...], x_ref[...].astype(BF16), _NT, preferred_element_type=F32)
    o_ref[...] = (acc * scale).astype(o_ref.dtype)


def _project_t(x, w_t, scale):
    b, s, _ = x.shape
    n = w_t.shape[0]
    return pl.pallas_call(
        functools.partial(_proj_t_kernel, scale=scale),
        out_shape=jax.ShapeDtypeStruct((b, n, s), BF16),
        grid=(b, s // PROJ_TM),
        in_specs=[pl.BlockSpec((None, PROJ_TM, D_MODEL), lambda bi, i: (bi, i, 0)),
                  _const_spec(w_t.shape)],
        out_specs=pl.BlockSpec((None, n, PROJ_TM), lambda bi, i: (bi, 0, i)),
        compiler_params=_params("arbitrary", "arbitrary"),
        name="project_t",
    )(x, w_t)


def _kv_kernel(x_ref, wk_ref, wvt_ref, k_ref, vt_ref, kmean_ref):
    xb = x_ref[...].astype(BF16)
    k = jnp.dot(xb, wk_ref[...], preferred_element_type=F32)
    k_ref[...] = k.astype(BF16)
    for j in range(PROJ_TM // MOBA_BLOCK):
        kmean_ref[j] = jnp.mean(k[j * MOBA_BLOCK:(j + 1) * MOBA_BLOCK], axis=0, keepdims=True)
    vt_ref[...] = lax.dot_general(wvt_ref[...], xb, _NT, preferred_element_type=F32).astype(BF16)


def _project_kv(x, w_k, w_v_t):
    b, s, _ = x.shape
    blocks_per_tile = PROJ_TM // MOBA_BLOCK
    tiles = s // PROJ_TM
    return pl.pallas_call(
        _kv_kernel,
        out_shape=(jax.ShapeDtypeStruct((b, s, D_MODEL), BF16),
                   jax.ShapeDtypeStruct((b, D_MODEL, s), BF16),
                   jax.ShapeDtypeStruct((b * s // MOBA_BLOCK, 1, D_MODEL), F32)),
        grid=(b, tiles),
        in_specs=[pl.BlockSpec((None, PROJ_TM, D_MODEL), lambda bi, i: (bi, i, 0)),
                  _const_spec(w_k.shape), _const_spec(w_v_t.shape)],
        out_specs=(pl.BlockSpec((None, PROJ_TM, D_MODEL), lambda bi, i: (bi, i, 0)),
                   pl.BlockSpec((None, D_MODEL, PROJ_TM), lambda bi, i: (bi, 0, i)),
                   pl.BlockSpec((blocks_per_tile, 1, D_MODEL), lambda bi, i: (bi * tiles + i, 0, 0))),
        compiler_params=_params("arbitrary", "arbitrary"),
        name="project_kv",
    )(x, w_k, w_v_t)


def _t5_bucket_np(dist):
    n = np.maximum(dist, 0)
    max_exact = N_BUCKETS // 2
    nf = np.maximum(n, 1).astype(np.float32)
    large = max_exact + (np.log(nf / np.float32(max_exact)) / np.float32(math.log(MAX_DISTANCE / max_exact))
                         * np.float32(N_BUCKETS - max_exact)).astype(np.int32)
    large = np.minimum(large, N_BUCKETS - 1)
    return np.where(n < max_exact, n, large).astype(np.int32)


def _bucket_tables():
    kpos = np.arange(MOBA_BLOCK)[:, None]
    qpos = np.arange(MOBA_BLOCK)[None, :]
    own = np.where(qpos >= kpos, _t5_bucket_np(qpos - kpos), -1)
    prev = _t5_bucket_np(qpos + MOBA_BLOCK - kpos)
    return np.stack([own, prev]).astype(np.int32)


def _bias_kernel(rel_ref, bucket_ref, o_ref):
    h = pl.program_id(0)
    far = rel_ref[N_BUCKETS - 1, h]
    for t in range(2):
        bucket = bucket_ref[t]
        acc = jnp.zeros(bucket.shape, F32)
        for b in range(N_BUCKETS):
            acc = jnp.where(bucket == b, rel_ref[b, h], acc)
        o_ref[t] = jnp.where(bucket < 0, NEG_INF, (acc - far) * LOG2_E)


def _bias_tables(rel_bias):
    buckets = jnp.asarray(_bucket_tables())
    return pl.pallas_call(
        _bias_kernel,
        out_shape=jax.ShapeDtypeStruct((N_HEADS, 2, MOBA_BLOCK, MOBA_BLOCK), F32),
        grid=(N_HEADS,),
        in_specs=[pl.BlockSpec(memory_space=pltpu.SMEM), _const_spec(buckets.shape)],
        out_specs=pl.BlockSpec((None, 2, MOBA_BLOCK, MOBA_BLOCK), lambda h: (h, 0, 0, 0)),
        compiler_params=_params("arbitrary"),
        name="rel_bias_tables",
    )(rel_bias, buckets)


ATTN_HEADS = 8
ATTN_WIDTH = ATTN_HEADS * HEAD_DIM
HEADS_PER_LANE_TILE = V7X_LANES // HEAD_DIM
FAR_TILES_PER_STEP = 3
QK_LOOKAHEAD = 6
LOG2_E = math.log2(math.e)


def _attn_kernel(qt_ref, k_ref, vt_ref, kmean_ref, bias_ref, o_ref,
                 sel_ref, ml_ref, acc_ref, qp_ref, sc_ref):
    n_blocks = kmean_ref.shape[0]
    blk = lax.broadcasted_iota(jnp.int32, (n_blocks, MOBA_BLOCK), 0)
    causal = (lax.broadcasted_iota(jnp.int32, (MOBA_BLOCK, MOBA_BLOCK), 0)
              <= lax.broadcasted_iota(jnp.int32, (MOBA_BLOCK, MOBA_BLOCK), 1))
    ones_rows = jnp.ones((BF16_SUBLANE_TILE, MOBA_BLOCK), BF16)

    def lane_slab(h):
        return slice((h // HEADS_PER_LANE_TILE) * V7X_LANES, (h // HEADS_PER_LANE_TILE + 1) * V7X_LANES)

    def padded_q(h, q0):
        qh = qt_ref[h * HEAD_DIM:(h + 1) * HEAD_DIM, pl.ds(q0, MOBA_BLOCK)]
        pieces = [jnp.zeros_like(qh)] * HEADS_PER_LANE_TILE
        pieces[h % HEADS_PER_LANE_TILE] = qh
        return jnp.concatenate(pieces, axis=0)

    def scores(h, j):
        k0 = pl.multiple_of(j * MOBA_BLOCK, MOBA_BLOCK)
        kj = k_ref[pl.ds(k0, MOBA_BLOCK), lane_slab(h)]
        return jnp.dot(kj, qp_ref[h], preferred_element_type=F32)

    def weights_and_values(h, j, s, bias, sel):
        if bias is not None:
            keep = causal if sel is None else sel
            s = jnp.where(keep, s, NEG_INF) + bias[...]
        m_loc = jnp.max(s, axis=0, keepdims=True)
        if sel is None:
            shift = m_loc
        else:
            shift = jnp.where(sel, m_loc, -NEG_INF)
            m_loc = jnp.where(sel, m_loc, NEG_INF)
        p = jnp.exp2(s - shift).astype(BF16)
        k0 = pl.multiple_of(j * MOBA_BLOCK, MOBA_BLOCK)
        vtj = vt_ref[h * HEAD_DIM:(h + 1) * HEAD_DIM, pl.ds(k0, MOBA_BLOCK)]
        pv = jnp.dot(jnp.concatenate([vtj, ones_rows], axis=0), p, preferred_element_type=F32)
        return m_loc, pv[HEAD_DIM:HEAD_DIM + 1], pv[:HEAD_DIM]

    def run_tiles(tiles, carried_in, prefetch):
        n = len(tiles)
        s = [None] * n
        out = [None] * n
        for t in range(QK_LOOKAHEAD if carried_in else 0, n + QK_LOOKAHEAD):
            if t < n:
                h, j, _, _ = tiles[t]
                s[t] = scores(h, j)
            else:
                h, j = prefetch[t - n]
                sc_ref[t - n] = scores(h, j)
            d = t - QK_LOOKAHEAD
            if d >= 0:
                h, j, bias, sel = tiles[d]
                sd = sc_ref[d] if carried_in and d < QK_LOOKAHEAD else s[d]
                out[d] = weights_and_values(h, j, sd, bias, sel)
                s[d] = None
        return out

    def far_tile_ids(t):
        return [(h, t * FAR_TILES_PER_STEP + u) for h in range(ATTN_HEADS) for u in range(FAR_TILES_PER_STEP)]

    def merge(h, parts, first):
        m_new = parts[0][0]
        for m_loc, _, _ in parts[1:]:
            m_new = jnp.maximum(m_new, m_loc)
        if first:
            l_new = jnp.zeros_like(m_new)
            acc_new = jnp.zeros((HEAD_DIM, MOBA_BLOCK), F32)
        else:
            m_old = ml_ref[h, 0:1, :]
            m_new = jnp.maximum(m_new, m_old)
            alpha = jnp.exp2(m_old - m_new)
            l_new = alpha * ml_ref[h, 1:2, :]
            acc_new = alpha * acc_ref[h]
        for m_loc, l_loc, pv in parts:
            w = jnp.exp2(m_loc - m_new)
            l_new = l_new + w * l_loc
            acc_new = acc_new + w * pv
        ml_ref[h, 0:1, :] = m_new
        ml_ref[h, 1:2, :] = l_new
        acc_ref[h] = acc_new

    def q_block(i, _):
        q0 = pl.multiple_of(i * MOBA_BLOCK, MOBA_BLOCK)
        jp = jnp.maximum(i - 1, 0)
        n_far = jnp.maximum(i - 1, 0)
        for h in range(ATTN_HEADS):
            qp_ref[h] = padded_q(h, q0)

        for h in range(ATTN_HEADS):
            gate = jnp.dot(kmean_ref[:, lane_slab(h)].astype(BF16), qp_ref[h], preferred_element_type=F32)
            gate = jnp.where(blk < i, gate, NEG_INF)
            sel = jnp.zeros(gate.shape, F32)
            for r in range(MOBA_TOPK):
                top = jnp.max(gate, axis=0, keepdims=True)
                first = jnp.min(jnp.where(gate == top, blk, n_blocks), axis=0, keepdims=True)
                hit = blk == first
                sel = jnp.where(hit & (r < i), 1.0, sel)
                gate = jnp.where(hit, -jnp.inf, gate)
            sel_ref[h] = sel

        tiles = []
        for h in range(ATTN_HEADS):
            tiles.append((h, i, bias_ref.at[h, 0], None))
            tiles.append((h, jp, bias_ref.at[h, 1], sel_ref[h, pl.ds(jp, 1), :] > 0.5))
        last_far = jnp.maximum(n_far - 1, 0)

        def prefetch_ids(t):
            return [(h, jnp.minimum(j, last_far)) for h, j in far_tile_ids(t)[:QK_LOOKAHEAD]]

        parts = run_tiles(tiles, carried_in=False, prefetch=prefetch_ids(0))
        for h in range(ATTN_HEADS):
            merge(h, parts[2 * h:2 * h + 2], first=True)

        def far_step(t, _):
            tiles = []
            for h, j in far_tile_ids(t):
                jc = jnp.minimum(j, last_far)
                sel = (sel_ref[h, pl.ds(jc, 1), :] > 0.5) & (j < n_far)
                tiles.append((h, jc, None, sel))
            parts = run_tiles(tiles, carried_in=True, prefetch=prefetch_ids(t + 1))
            for h in range(ATTN_HEADS):
                merge(h, parts[h * FAR_TILES_PER_STEP:(h + 1) * FAR_TILES_PER_STEP], first=False)
            return 0

        lax.fori_loop(0, (n_far + FAR_TILES_PER_STEP - 1) // FAR_TILES_PER_STEP, far_step, 0)

        for slab in range(ATTN_HEADS // HEADS_PER_LANE_TILE):
            heads = range(slab * HEADS_PER_LANE_TILE, (slab + 1) * HEADS_PER_LANE_TILE)
            out_t = jnp.concatenate([acc_ref[h] / ml_ref[h, 1:2, :] for h in heads], axis=0)
            o_ref[pl.ds(q0, MOBA_BLOCK), slab * V7X_LANES:(slab + 1) * V7X_LANES] = out_t.T.astype(o_ref.dtype)
        return 0

    lax.fori_loop(0, n_blocks, q_block, 0)


def _moba_attention(qt, k, vt, kmean, bias):
    b, _, s = qt.shape
    n_blocks = s // MOBA_BLOCK
    single = dict(pipeline_mode=pl.Buffered(1))
    t_spec = pl.BlockSpec((None, ATTN_WIDTH, s), lambda bi, hi: (bi, hi, 0), **single)
    return pl.pallas_call(
        _attn_kernel,
        out_shape=jax.ShapeDtypeStruct((b, s, D_MODEL), BF16),
        grid=(b, N_HEADS // ATTN_HEADS),
        in_specs=[t_spec,
                  pl.BlockSpec((None, s, ATTN_WIDTH), lambda bi, hi: (bi, 0, hi), **single),
                  t_spec,
                  pl.BlockSpec((None, n_blocks, ATTN_WIDTH), lambda bi, hi: (bi, 0, hi)),
                  pl.BlockSpec((ATTN_HEADS, 2, MOBA_BLOCK, MOBA_BLOCK), lambda bi, hi: (hi, 0, 0, 0), **single)],
        out_specs=pl.BlockSpec((None, s, ATTN_WIDTH), lambda bi, hi: (bi, 0, hi)),
        scratch_shapes=[pltpu.VMEM((ATTN_HEADS, n_blocks, MOBA_BLOCK), F32),
                        pltpu.VMEM((ATTN_HEADS, 2, MOBA_BLOCK), F32),
                        pltpu.VMEM((ATTN_HEADS, HEAD_DIM, MOBA_BLOCK), F32),
                        pltpu.VMEM((ATTN_HEADS, V7X_LANES, MOBA_BLOCK), BF16),
                        pltpu.VMEM((QK_LOOKAHEAD, MOBA_BLOCK, MOBA_BLOCK), F32)],
        compiler_params=_params("arbitrary", "arbitrary"),
        name="moba_attention",
    )(qt, k, vt, kmean, bias)


OPROJ_TM = 512


def _oproj_kernel(a_ref, w_ref, x_ref, lng_ref, lnb_ref, wr_ref, br_ref, o_ref, route_ref):
    h = jnp.dot(a_ref[...], w_ref[...], preferred_element_type=F32)
    out = _layer_norm(DEEPNORM_ALPHA * x_ref[...] + h, lng_ref[...], lnb_ref[...])
    o_ref[...] = out
    route_ref[...] = _route(out, wr_ref[...], br_ref[...])


def _out_project(a, w_o, x, ln_g, ln_b, wr_t, br):
    t = x.shape[0]
    row_spec = pl.BlockSpec((OPROJ_TM, D_MODEL), lambda i: (i, 0))
    return pl.pallas_call(
        _oproj_kernel,
        out_shape=(jax.ShapeDtypeStruct((t, D_MODEL), F32), jax.ShapeDtypeStruct((ROUTER_ROWS, t), F32)),
        grid=(t // OPROJ_TM,),
        in_specs=[row_spec, _const_spec(w_o.shape), row_spec,
                  _const_spec(ln_g.shape), _const_spec(ln_b.shape),
                  _const_spec(wr_t.shape), _const_spec(br.shape)],
        out_specs=(row_spec, pl.BlockSpec((ROUTER_ROWS, OPROJ_TM), lambda i: (0, i))),
        compiler_params=_params("arbitrary"),
        name="out_project",
    )(a, w_o, x, ln_g, ln_b, wr_t, br)


def kernel(x, ln_g, ln_b, a_w_in, a_v_g, a_v_b, a_w_s, a_b_s, a_w_out, w_k, w_v, b_w_q, b_w_o,
           rel_bias, w_router, b_router, e_w1, e_w3, e_w2):
    batch, seq, _ = x.shape
    t = batch * seq
    n_blocks = seq // MOBA_BLOCK
    xt = x.reshape(t, D_MODEL)
    wr_t = w_router.T.astype(BF16)
    br = b_router.reshape(N_EXPERTS, 1)
    n_tiles = t // MOE_TM + MOE_CLASSES
    sorted_buf = jnp.zeros((n_tiles * MOE_TM * FEATURE_ROWS, V7X_LANES), F32)

    def ln_rows(l, j):
        return ln_g[l, j].reshape(1, D_MODEL), ln_b[l, j].reshape(1, D_MODEL)

    k = vt = kmean = bias = None
    for l in range(DEPTH):
        g0, b0 = ln_rows(l, 0)
        if l < N_A_LAYERS:
            xt, routed = _gmlp_layer(xt, a_w_in[l].astype(BF16),
                                     a_v_g[l].reshape(1, GMLP_HALF), a_v_b[l].reshape(1, GMLP_HALF),
                                     a_w_s[l], a_b_s[l].T, a_w_out[l].astype(BF16), g0, b0, wr_t, br)
        else:
            x3 = xt.reshape(batch, seq, D_MODEL)
            if l == N_A_LAYERS:
                k, vt, kmean = _project_kv(x3, w_k.astype(BF16), w_v.T.astype(BF16))
                kmean = kmean.reshape(batch, n_blocks, D_MODEL)
                bias = _bias_tables(rel_bias)
            j = l - N_A_LAYERS
            qt = _project_t(x3, b_w_q[j].T.astype(BF16), HEAD_DIM ** -0.5 * LOG2_E)
            a = _moba_attention(qt, k, vt, kmean, bias)
            xt, routed = _out_project(a.reshape(t, D_MODEL), b_w_o[j].astype(BF16), xt, g0, b0, wr_t, br)
        g1, b1 = ln_rows(l, 1)
        xt, sorted_buf = _moe_layer(xt, routed, sorted_buf, e_w1, e_w3, e_w2, g1, b1, layer=l)
    return xt.reshape(batch, seq, D_MODEL)
```

```python
import functools
import math

import numpy as np
import jax
import jax.numpy as jnp
from jax import lax
from jax.experimental import pallas as pl
from jax.experimental.pallas import tpu as pltpu

D_MODEL = 1024
DEPTH = 4
N_A_LAYERS = DEPTH // 2
GMLP_HALF = 3 * D_MODEL
GMLP_GROUPS = 8
GMLP_GROUP_WIDTH = GMLP_HALF // GMLP_GROUPS
GMLP_CHUNK = 128
N_HEADS = 16
HEAD_DIM = D_MODEL // N_HEADS
MOBA_BLOCK = 256
MOBA_TOPK = 3
N_BUCKETS = 32
MAX_DISTANCE = 128
N_EXPERTS = 16
N_GROUPS = 4
EXPERTS_PER_GROUP = N_EXPERTS // N_GROUPS
TOP_K = 2
MOE_PAIRS = EXPERTS_PER_GROUP * (EXPERTS_PER_GROUP - 1) // 2
MOE_CLASSES = N_GROUPS * MOE_PAIRS
ROUTER_ROWS = 8
D_EXPERT = D_MODEL // 4
DEEPNORM_ALPHA = (2 * DEPTH) ** 0.25
LN_EPS = 1e-5
NEG_INF = -1e30

V7X_VMEM_LIMIT_BYTES = 56 * 1024 * 1024
V7X_LANES = 128
BF16_SUBLANE_TILE = 16

BF16 = jnp.bfloat16
F32 = jnp.float32

_NT = (((1,), (1,)), ((), ()))


def _params(*semantics):
    return pltpu.CompilerParams(dimension_semantics=semantics,
                                vmem_limit_bytes=V7X_VMEM_LIMIT_BYTES)


def _const_spec(shape):
    zeros = (0,) * len(shape)
    return pl.BlockSpec(shape, lambda *_: zeros, pipeline_mode=pl.Buffered(1))


def _layer_norm(y, g, b):
    mu = jnp.mean(y, axis=-1, keepdims=True)
    d = y - mu
    var = jnp.mean(d * d, axis=-1, keepdims=True)
    return d * lax.rsqrt(var + LN_EPS) * g + b


GMLP_TM = 512
GMLP_GROUPS_PER_STEP = 2


def _gmlp_kernel(x_ref, w_in_ref, vg_ref, vb_ref, ws_ref, bs_ref, w_out_ref,
                 lng_ref, lnb_ref, wr_ref, br_ref, o_ref, route_ref, vn_ref, sv_ref):
    x = x_ref[...]
    xb = x.astype(BF16)
    v = jax.nn.gelu(jnp.dot(xb, w_in_ref[:, GMLP_HALF:], preferred_element_type=F32))
    vn_ref[...] = _layer_norm(v, vg_ref[...], vb_ref[...]).astype(BF16)

    row = lax.broadcasted_iota(jnp.int32, (GMLP_CHUNK, GMLP_CHUNK), 0)
    col = lax.broadcasted_iota(jnp.int32, (GMLP_CHUNK, GMLP_CHUNK), 1)
    causal = row >= col
    width = GMLP_GROUPS_PER_STEP * GMLP_GROUP_WIDTH
    acc = jnp.zeros((GMLP_TM, D_MODEL), F32)
    for step in range(GMLP_GROUPS // GMLP_GROUPS_PER_STEP):
        lo = step * width
        u = jax.nn.gelu(jnp.dot(xb, w_in_ref[:, lo:lo + width], preferred_element_type=F32))
        for gi in range(GMLP_GROUPS_PER_STEP):
            g = step * GMLP_GROUPS_PER_STEP + gi
            w_s = jnp.where(causal, ws_ref[g], 0.0).astype(BF16)
            b_s = bs_ref[:, g:g + 1]
            glo = g * GMLP_GROUP_WIDTH
            for c in range(GMLP_TM // GMLP_CHUNK):
                rows = slice(c * GMLP_CHUNK, (c + 1) * GMLP_CHUNK)
                vc = vn_ref[rows, glo:glo + GMLP_GROUP_WIDTH]
                sv_ref[rows, gi * GMLP_GROUP_WIDTH:(gi + 1) * GMLP_GROUP_WIDTH] = (
                    jnp.dot(w_s, vc, preferred_element_type=F32) + b_s)
        gated = (u * sv_ref[...]).astype(BF16)
        acc = acc + jnp.dot(gated, w_out_ref[lo:lo + width, :], preferred_element_type=F32)
    out = _layer_norm(DEEPNORM_ALPHA * x + acc, lng_ref[...], lnb_ref[...])
    o_ref[...] = out
    route_ref[...] = _route(out, wr_ref[...], br_ref[...])


def _gmlp_layer(x, w_in, v_g, v_b, w_s, b_s_t, w_out, ln_g, ln_b, wr_t, br):
    t = x.shape[0]
    row_spec = pl.BlockSpec((GMLP_TM, D_MODEL), lambda i: (i, 0))
    return pl.pallas_call(
        _gmlp_kernel,
        out_shape=(jax.ShapeDtypeStruct((t, D_MODEL), F32), jax.ShapeDtypeStruct((ROUTER_ROWS, t), F32)),
        grid=(t // GMLP_TM,),
        in_specs=[row_spec,
                  _const_spec(w_in.shape), _const_spec(v_g.shape), _const_spec(v_b.shape),
                  _const_spec(w_s.shape), _const_spec(b_s_t.shape), _const_spec(w_out.shape),
                  _const_spec(ln_g.shape), _const_spec(ln_b.shape),
                  _const_spec(wr_t.shape), _const_spec(br.shape)],
        out_specs=(row_spec, pl.BlockSpec((ROUTER_ROWS, GMLP_TM), lambda i: (0, i))),
        scratch_shapes=[pltpu.VMEM((GMLP_TM, GMLP_HALF), BF16),
                        pltpu.VMEM((GMLP_TM, GMLP_GROUPS_PER_STEP * GMLP_GROUP_WIDTH), F32)],
        compiler_params=_params("arbitrary"),
        name="gmlp_layer",
    )(x, w_in, v_g, v_b, w_s, b_s_t, w_out, ln_g, ln_b, wr_t, br)


def _route(x, wr_t, br):
    logits = lax.dot_general(wr_t, x.astype(BF16), _NT, preferred_element_type=F32) + br
    m = jnp.max(logits, axis=0, keepdims=True)
    ex = jnp.exp(logits - m)
    probs = ex / jnp.sum(ex, axis=0, keepdims=True)
    p = [probs[e:e + 1, :] for e in range(N_EXPERTS)]

    scores = []
    for g in range(N_GROUPS):
        q = p[g * EXPERTS_PER_GROUP:(g + 1) * EXPERTS_PER_GROUP]
        best = None
        for a in range(EXPERTS_PER_GROUP):
            for b in range(a + 1, EXPERTS_PER_GROUP):
                s = q[a] + q[b]
                best = s if best is None else jnp.maximum(best, s)
        scores.append(best)
    best_score = scores[0]
    gsel = jnp.zeros_like(best_score, dtype=jnp.int32)
    for g in range(1, N_GROUPS):
        better = scores[g] > best_score
        best_score = jnp.where(better, scores[g], best_score)
        gsel = jnp.where(better, g, gsel)

    slot_chosen = [None] * EXPERTS_PER_GROUP
    slot_gate = [None] * EXPERTS_PER_GROUP
    for e in range(N_EXPERTS):
        g, k = divmod(e, EXPERTS_PER_GROUP)
        rank = jnp.zeros_like(gsel)
        for e2 in range(g * EXPERTS_PER_GROUP, (g + 1) * EXPERTS_PER_GROUP):
            if e2 == e:
                continue
            ahead = (p[e2] > p[e]) | ((p[e2] == p[e]) & (e2 < e))
            rank = rank + ahead.astype(jnp.int32)
        chosen = (gsel == g) & (rank < TOP_K)
        gate = jnp.where(chosen, p[e] / best_score, 0.0)
        slot_chosen[k] = chosen if slot_chosen[k] is None else slot_chosen[k] | chosen
        slot_gate[k] = gate if slot_gate[k] is None else slot_gate[k] + gate
    c0, c1, c2, c3 = slot_chosen
    lo = jnp.where(c0, 0, jnp.where(c1, 1, 2))
    hi = jnp.where(c3, 3, jnp.where(c2, 2, 1))
    w_lo = jnp.where(c0, slot_gate[0], jnp.where(c1, slot_gate[1], slot_gate[2]))
    w_hi = jnp.where(c3, slot_gate[3], jnp.where(c2, slot_gate[2], slot_gate[1]))
    pair = jnp.right_shift(lo * (7 - lo), 1) + hi - lo - 1
    cls = gsel * MOE_PAIRS + pair
    zero = jnp.zeros_like(w_lo)
    return jnp.concatenate([cls.astype(F32), w_lo, w_hi] + [zero] * (ROUTER_ROWS - 3), axis=0)


MOE_TM = 256
FEATURE_ROWS = D_MODEL // V7X_LANES
DISPATCH_TD = 1024
DMA_UNROLL = 8


def _token_rows(ref, n_tokens):
    return jnp.concatenate(
        [ref[pl.ds(c, n_tokens, stride=FEATURE_ROWS), :] for c in range(FEATURE_ROWS)], axis=1)


def _store_token_rows(ref, value, n_tokens):
    for c in range(FEATURE_ROWS):
        ref[pl.ds(c, n_tokens, stride=FEATURE_ROWS), :] = value[:, c * V7X_LANES:(c + 1) * V7X_LANES]


def _dispatch_kernel(pos_ref, x_ref, gate_ref, buf_in_ref, buf_ref, gate_tbl_ref, stage_ref, sems):
    del buf_in_ref
    i = pl.program_id(0)
    n = pl.num_programs(0)
    slot = i % 2
    stage = stage_ref.at[slot]
    rows = DISPATCH_TD * FEATURE_ROWS

    @pl.when(i == 0)
    def _():
        gate_tbl_ref[...] = jnp.zeros_like(gate_tbl_ref)

    def wait_slot(s):
        pltpu.make_async_copy(stage_ref.at[s], buf_ref.at[pl.ds(0, rows)], sems.at[s]).wait()

    @pl.when(i >= 2)
    def _():
        wait_slot(slot)

    _store_token_rows(stage, x_ref[...], DISPATCH_TD)

    def issue(r, _):
        p = pos_ref[i * DISPATCH_TD + r]
        dst = pl.multiple_of(p * FEATURE_ROWS, FEATURE_ROWS)
        src = pl.multiple_of(r * FEATURE_ROWS, FEATURE_ROWS)
        pltpu.make_async_copy(stage.at[pl.ds(src, FEATURE_ROWS)], buf_ref.at[pl.ds(dst, FEATURE_ROWS)],
                              sems.at[slot]).start()
        gate_tbl_ref[pl.ds(p, 1), :] = gate_ref[pl.ds(r, 1), :]
        return 0

    lax.fori_loop(0, DISPATCH_TD, issue, 0, unroll=DMA_UNROLL)

    @pl.when(i == n - 1)
    def _():
        wait_slot(slot)

        @pl.when(n > 1)
        def _():
            wait_slot(1 - slot)


def _dispatch(pos, x, gates, buf):
    t = x.shape[0]
    n_slots = buf.shape[0] // FEATURE_ROWS
    return pl.pallas_call(
        _dispatch_kernel,
        out_shape=(jax.ShapeDtypeStruct(buf.shape, buf.dtype),
                   jax.ShapeDtypeStruct((n_slots, V7X_LANES), F32)),
        grid_spec=pltpu.PrefetchScalarGridSpec(
            num_scalar_prefetch=1,
            grid=(t // DISPATCH_TD,),
            in_specs=[pl.BlockSpec((DISPATCH_TD, D_MODEL), lambda i, pos: (i, 0)),
                      pl.BlockSpec((DISPATCH_TD, V7X_LANES), lambda i, pos: (i, 0)),
                      pl.BlockSpec(memory_space=pl.ANY)],
            out_specs=(pl.BlockSpec(memory_space=pl.ANY),
                       pl.BlockSpec((n_slots, V7X_LANES), lambda i, pos: (0, 0))),
            scratch_shapes=[pltpu.VMEM((2, DISPATCH_TD * FEATURE_ROWS, V7X_LANES), F32),
                            pltpu.SemaphoreType.DMA((2,))]),
        input_output_aliases={3: 0},
        compiler_params=_params("arbitrary"),
        name="moe_dispatch",
    )(pos, x, gates, buf)


def _expert_kernel(ea_ref, eb_ref, fresh_ref, used_ref, xs_ref, gate_ref,
                   w1a_ref, w1b_ref, w3a_ref, w3b_ref, w2a_ref, w2b_ref, ys_ref,
                   w1_s, w3_s, w2_s):
    del ea_ref, eb_ref
    i = pl.program_id(0)

    @pl.when(fresh_ref[i] == 1)
    def _():
        for e, (w1, w3, w2) in enumerate(((w1a_ref, w3a_ref, w2a_ref), (w1b_ref, w3b_ref, w2b_ref))):
            w1_s[e] = w1[...].astype(BF16)
            w3_s[e] = w3[...].astype(BF16)
            w2_s[e] = w2[...].astype(BF16)

    @pl.when(used_ref[i] == 1)
    def _():
        xb = _token_rows(xs_ref, MOE_TM).astype(BF16)
        gates = gate_ref[...]
        y = None
        for e in range(TOP_K):
            h1 = jnp.dot(xb, w1_s[e], preferred_element_type=F32)
            h3 = jnp.dot(xb, w3_s[e], preferred_element_type=F32)
            h = (jax.nn.silu(h1) * h3 * gates[:, e:e + 1]).astype(BF16)
            ye = jnp.dot(h, w2_s[e], preferred_element_type=F32)
            y = ye if y is None else y + ye
        _store_token_rows(ys_ref, y, MOE_TM)

    @pl.when(used_ref[i] == 0)
    def _():
        ys_ref[...] = jnp.zeros_like(ys_ref)


def _expert_ffn(ea, eb, fresh, used, xs, gates, w1, w3, w2, layer):
    n_tiles = ea.shape[0]

    def w_spec(shape, which):
        return pl.BlockSpec((None, None) + shape,
                            lambda i, ea, eb, fresh, used: (layer, (ea, eb)[which][i], 0, 0))

    up, down = (D_MODEL, D_EXPERT), (D_EXPERT, D_MODEL)
    return pl.pallas_call(
        _expert_kernel,
        out_shape=jax.ShapeDtypeStruct((n_tiles * MOE_TM * FEATURE_ROWS, V7X_LANES), F32),
        grid_spec=pltpu.PrefetchScalarGridSpec(
            num_scalar_prefetch=4,
            grid=(n_tiles,),
            in_specs=[pl.BlockSpec((MOE_TM * FEATURE_ROWS, V7X_LANES), lambda i, *_: (i, 0)),
                      pl.BlockSpec((MOE_TM, V7X_LANES), lambda i, *_: (i, 0)),
                      w_spec(up, 0), w_spec(up, 1), w_spec(up, 0), w_spec(up, 1),
                      w_spec(down, 0), w_spec(down, 1)],
            out_specs=pl.BlockSpec((MOE_TM * FEATURE_ROWS, V7X_LANES), lambda i, *_: (i, 0)),
            scratch_shapes=[pltpu.VMEM((TOP_K,) + up, BF16), pltpu.VMEM((TOP_K,) + up, BF16),
                            pltpu.VMEM((TOP_K,) + down, BF16)]),
        compiler_params=_params("arbitrary"),
        name="moe_experts",
    )(ea, eb, fresh, used, xs, gates, w1, w1, w3, w3, w2, w2)


def _gather_and_norm(pos_ref, ys_ref, x_ref, lng_ref, lnb_ref, stage_ref, sems):
    i = pl.program_id(0)
    n = pl.num_programs(0)
    slot = i % 2
    rows = DISPATCH_TD * FEATURE_ROWS

    def issue_step(step, s):
        def issue(r, _):
            src = pl.multiple_of(pos_ref[step * DISPATCH_TD + r] * FEATURE_ROWS, FEATURE_ROWS)
            dst = pl.multiple_of(r * FEATURE_ROWS, FEATURE_ROWS)
            pltpu.make_async_copy(ys_ref.at[pl.ds(src, FEATURE_ROWS)],
                                  stage_ref.at[s, pl.ds(dst, FEATURE_ROWS)], sems.at[s]).start()
            return 0
        lax.fori_loop(0, DISPATCH_TD, issue, 0, unroll=DMA_UNROLL)

    @pl.when(i == 0)
    def _():
        issue_step(0, 0)

    @pl.when(i + 1 < n)
    def _():
        issue_step(i + 1, 1 - slot)

    pltpu.make_async_copy(ys_ref.at[pl.ds(0, rows)], stage_ref.at[slot], sems.at[slot]).wait()

    f = _token_rows(stage_ref.at[slot], DISPATCH_TD)
    return _layer_norm(DEEPNORM_ALPHA * x_ref[...] + f, lng_ref[...], lnb_ref[...])


def _combine_kernel(pos_ref, ys_ref, x_ref, lng_ref, lnb_ref, o_ref, stage_ref, sems):
    o_ref[...] = _gather_and_norm(pos_ref, ys_ref, x_ref, lng_ref, lnb_ref, stage_ref, sems)


def _combine_q_kernel(pos_ref, ys_ref, x_ref, lng_ref, lnb_ref, wq_ref, o_ref, qt_ref, stage_ref, sems,
                      *, q_scale):
    out = _gather_and_norm(pos_ref, ys_ref, x_ref, lng_ref, lnb_ref, stage_ref, sems)
    o_ref[...] = out
    q = lax.dot_general(wq_ref[...], out.astype(BF16), _NT, preferred_element_type=F32)
    qt_ref[...] = (q * q_scale).astype(qt_ref.dtype)


def _combine(pos, ys, x, ln_g, ln_b, wq_t=None, q_scale=None, batch=None):
    t = x.shape[0]
    row_spec = pl.BlockSpec((DISPATCH_TD, D_MODEL), lambda i, pos: (i, 0))
    vec_spec = pl.BlockSpec((1, D_MODEL), lambda i, pos: (0, 0))
    in_specs = [pl.BlockSpec(memory_space=pl.ANY), row_spec, vec_spec, vec_spec]
    out_shape = jax.ShapeDtypeStruct((t, D_MODEL), F32)
    out_specs = row_spec
    body, args = _combine_kernel, (pos, ys, x, ln_g, ln_b)
    if wq_t is not None:
        steps_per_batch = t // batch // DISPATCH_TD
        body = functools.partial(_combine_q_kernel, q_scale=q_scale)
        args = args + (wq_t,)
        in_specs = in_specs + [pl.BlockSpec(wq_t.shape, lambda i, pos: (0, 0), pipeline_mode=pl.Buffered(1))]
        out_shape = (out_shape, jax.ShapeDtypeStruct((batch, D_MODEL, t // batch), BF16))
        out_specs = (row_spec, pl.BlockSpec((None, D_MODEL, DISPATCH_TD),
                                            lambda i, pos: (i // steps_per_batch, 0, i % steps_per_batch)))
    return pl.pallas_call(
        body,
        out_shape=out_shape,
        grid_spec=pltpu.PrefetchScalarGridSpec(
            num_scalar_prefetch=1,
            grid=(t // DISPATCH_TD,),
            in_specs=in_specs,
            out_specs=out_specs,
            scratch_shapes=[pltpu.VMEM((2, DISPATCH_TD * FEATURE_ROWS, V7X_LANES), F32),
                            pltpu.SemaphoreType.DMA((2,))]),
        compiler_params=_params("arbitrary"),
        name="moe_combine",
    )(*args)


def _pair_table():
    lo, hi = [], []
    for g in range(N_GROUPS):
        for a in range(EXPERTS_PER_GROUP):
            for b in range(a + 1, EXPERTS_PER_GROUP):
                lo.append(g * EXPERTS_PER_GROUP + a)
                hi.append(g * EXPERTS_PER_GROUP + b)
    return np.asarray(lo, np.int32), np.asarray(hi, np.int32)


def _sort_plan(cls, n_tiles):
    onehot = (cls[:, None] == jnp.arange(MOE_CLASSES, dtype=jnp.int32)[None, :]).astype(jnp.int32)
    running = jnp.cumsum(onehot, axis=0)
    counts = running[-1]
    padded = (counts + MOE_TM - 1) // MOE_TM * MOE_TM
    ends = jnp.cumsum(padded)
    rank = jnp.sum(running * onehot, axis=1) - 1
    pos = jnp.sum(onehot * (ends - padded)[None, :], axis=1) + rank
    tile_cls = jnp.searchsorted(ends, jnp.arange(n_tiles, dtype=jnp.int32) * MOE_TM, side="right")
    used = (tile_cls < MOE_CLASSES).astype(jnp.int32)
    tile_cls = jnp.minimum(tile_cls, MOE_CLASSES - 1).astype(jnp.int32)
    fresh = jnp.concatenate([jnp.ones((1,), jnp.int32),
                             (tile_cls[1:] != tile_cls[:-1]).astype(jnp.int32)])
    lo, hi = _pair_table()
    return pos.astype(jnp.int32), jnp.asarray(lo)[tile_cls], jnp.asarray(hi)[tile_cls], fresh, used


def _moe_layer(x, routed, buf, w1, w3, w2, ln_g, ln_b, layer, **next_q):
    n_tiles = buf.shape[0] // (MOE_TM * FEATURE_ROWS)
    pos, ea, eb, fresh, used = _sort_plan(routed[0].astype(jnp.int32), n_tiles)
    gate_rows = jnp.pad(routed[1:1 + TOP_K].T, ((0, 0), (0, V7X_LANES - TOP_K)))
    buf, gates = _dispatch(pos, x, gate_rows, buf)
    ys = _expert_ffn(ea, eb, fresh, used, buf, gates, w1, w3, w2, layer)
    return _combine(pos, ys, x, ln_g, ln_b, **next_q), buf


PROJ_TM = 512


def _kv_kernel(x_ref, wk_ref, wvt_ref, k_ref, vt_ref, kmean_ref):
    xb = x_ref[...].astype(BF16)
    k = jnp.dot(xb, wk_ref[...], preferred_element_type=F32)
    k_ref[...] = k.astype(BF16)
    for j in range(PROJ_TM // MOBA_BLOCK):
        kmean_ref[j] = jnp.mean(k[j * MOBA_BLOCK:(j + 1) * MOBA_BLOCK], axis=0, keepdims=True)
    vt_ref[...] = lax.dot_general(wvt_ref[...], xb, _NT, preferred_element_type=F32).astype(BF16)


def _project_kv(x, w_k, w_v_t):
    b, s, _ = x.shape
    blocks_per_tile = PROJ_TM // MOBA_BLOCK
    tiles = s // PROJ_TM
    return pl.pallas_call(
        _kv_kernel,
        out_shape=(jax.ShapeDtypeStruct((b, s, D_MODEL), BF16),
                   jax.ShapeDtypeStruct((b, D_MODEL, s), BF16),
                   jax.ShapeDtypeStruct((b * s // MOBA_BLOCK, 1, D_MODEL), F32)),
        grid=(b, tiles),
        in_specs=[pl.BlockSpec((None, PROJ_TM, D_MODEL), lambda bi, i: (bi, i, 0)),
                  _const_spec(w_k.shape), _const_spec(w_v_t.shape)],
        out_specs=(pl.BlockSpec((None, PROJ_TM, D_MODEL), lambda bi, i: (bi, i, 0)),
                   pl.BlockSpec((None, D_MODEL, PROJ_TM), lambda bi, i: (bi, 0, i)),
                   pl.BlockSpec((blocks_per_tile, 1, D_MODEL), lambda bi, i: (bi * tiles + i, 0, 0))),
        compiler_params=_params("arbitrary", "arbitrary"),
        name="project_kv",
    )(x, w_k, w_v_t)


def _t5_bucket_np(dist):
    n = np.maximum(dist, 0)
    max_exact = N_BUCKETS // 2
    nf = np.maximum(n, 1).astype(np.float32)
    large = max_exact + (np.log(nf / np.float32(max_exact)) / np.float32(math.log(MAX_DISTANCE / max_exact))
                         * np.float32(N_BUCKETS - max_exact)).astype(np.int32)
    large = np.minimum(large, N_BUCKETS - 1)
    return np.where(n < max_exact, n, large).astype(np.int32)


def _bucket_tables():
    kpos = np.arange(MOBA_BLOCK)[:, None]
    qpos = np.arange(MOBA_BLOCK)[None, :]
    own = np.where(qpos >= kpos, _t5_bucket_np(qpos - kpos), -1)
    prev = _t5_bucket_np(qpos + MOBA_BLOCK - kpos)
    return np.stack([own, prev]).astype(np.int32)


def _bias_kernel(rel_ref, bucket_ref, o_ref):
    h = pl.program_id(0)
    far = rel_ref[N_BUCKETS - 1, h]
    for t in range(2):
        bucket = bucket_ref[t]
        acc = jnp.zeros(bucket.shape, F32)
        for b in range(N_BUCKETS):
            acc = jnp.where(bucket == b, rel_ref[b, h], acc)
        o_ref[t] = jnp.where(bucket < 0, NEG_INF, (acc - far) * LOG2_E)


def _bias_tables(rel_bias):
    buckets = jnp.asarray(_bucket_tables())
    return pl.pallas_call(
        _bias_kernel,
        out_shape=jax.ShapeDtypeStruct((N_HEADS, 2, MOBA_BLOCK, MOBA_BLOCK), F32),
        grid=(N_HEADS,),
        in_specs=[pl.BlockSpec(memory_space=pltpu.SMEM), _const_spec(buckets.shape)],
        out_specs=pl.BlockSpec((None, 2, MOBA_BLOCK, MOBA_BLOCK), lambda h: (h, 0, 0, 0)),
        compiler_params=_params("arbitrary"),
        name="rel_bias_tables",
    )(rel_bias, buckets)


ATTN_HEADS = 8
ATTN_WIDTH = ATTN_HEADS * HEAD_DIM
HEADS_PER_LANE_TILE = V7X_LANES // HEAD_DIM
FAR_TILES_PER_STEP = 3
QK_LOOKAHEAD = 6
LOG2_E = math.log2(math.e)


def _attn_kernel(qt_ref, k_ref, vt_ref, kmean_ref, bias_ref, o_ref,
                 sel_ref, ml_ref, acc_ref, qp_ref, sc_ref):
    n_blocks = kmean_ref.shape[0]
    blk = lax.broadcasted_iota(jnp.int32, (n_blocks, MOBA_BLOCK), 0)
    causal = (lax.broadcasted_iota(jnp.int32, (MOBA_BLOCK, MOBA_BLOCK), 0)
              <= lax.broadcasted_iota(jnp.int32, (MOBA_BLOCK, MOBA_BLOCK), 1))
    ones_rows = jnp.ones((BF16_SUBLANE_TILE, MOBA_BLOCK), BF16)

    def lane_slab(h):
        return slice((h // HEADS_PER_LANE_TILE) * V7X_LANES, (h // HEADS_PER_LANE_TILE + 1) * V7X_LANES)

    def padded_q(h, q0):
        qh = qt_ref[h * HEAD_DIM:(h + 1) * HEAD_DIM, pl.ds(q0, MOBA_BLOCK)]
        pieces = [jnp.zeros_like(qh)] * HEADS_PER_LANE_TILE
        pieces[h % HEADS_PER_LANE_TILE] = qh
        return jnp.concatenate(pieces, axis=0)

    def scores(h, j):
        k0 = pl.multiple_of(j * MOBA_BLOCK, MOBA_BLOCK)
        kj = k_ref[pl.ds(k0, MOBA_BLOCK), lane_slab(h)]
        return jnp.dot(kj, qp_ref[h], preferred_element_type=F32)

    def weights_and_values(h, j, s, bias, sel):
        if bias is not None:
            keep = causal if sel is None else sel
            s = jnp.where(keep, s, NEG_INF) + bias[...]
        m_loc = jnp.max(s, axis=0, keepdims=True)
        if sel is None:
            shift = m_loc
        else:
            shift = jnp.where(sel, m_loc, -NEG_INF)
            m_loc = jnp.where(sel, m_loc, NEG_INF)
        p = jnp.exp2(s - shift).astype(BF16)
        k0 = pl.multiple_of(j * MOBA_BLOCK, MOBA_BLOCK)
        vtj = vt_ref[h * HEAD_DIM:(h + 1) * HEAD_DIM, pl.ds(k0, MOBA_BLOCK)]
        pv = jnp.dot(jnp.concatenate([vtj, ones_rows], axis=0), p, preferred_element_type=F32)
        return m_loc, pv[HEAD_DIM:HEAD_DIM + 1], pv[:HEAD_DIM]

    def run_tiles(tiles, carried_in, prefetch):
        n = len(tiles)
        s = [None] * n
        out = [None] * n
        for t in range(QK_LOOKAHEAD if carried_in else 0, n + QK_LOOKAHEAD):
            if t < n:
                h, j, _, _ = tiles[t]
                s[t] = scores(h, j)
            else:
                h, j = prefetch[t - n]
                sc_ref[t - n] = scores(h, j)
            d = t - QK_LOOKAHEAD
            if d >= 0:
                h, j, bias, sel = tiles[d]
                sd = sc_ref[d] if carried_in and d < QK_LOOKAHEAD else s[d]
                out[d] = weights_and_values(h, j, sd, bias, sel)
                s[d] = None
        return out

    def far_tile_ids(t):
        return [(h, t * FAR_TILES_PER_STEP + u) for h in range(ATTN_HEADS) for u in range(FAR_TILES_PER_STEP)]

    def merge(h, parts, first):
        m_new = parts[0][0]
        for m_loc, _, _ in parts[1:]:
            m_new = jnp.maximum(m_new, m_loc)
        if first:
            l_new = jnp.zeros_like(m_new)
            acc_new = jnp.zeros((HEAD_DIM, MOBA_BLOCK), F32)
        else:
            m_old = ml_ref[h, 0:1, :]
            m_new = jnp.maximum(m_new, m_old)
            alpha = jnp.exp2(m_old - m_new)
            l_new = alpha * ml_ref[h, 1:2, :]
            acc_new = alpha * acc_ref[h]
        for m_loc, l_loc, pv in parts:
            w = jnp.exp2(m_loc - m_new)
            l_new = l_new + w * l_loc
            acc_new = acc_new + w * pv
        ml_ref[h, 0:1, :] = m_new
        ml_ref[h, 1:2, :] = l_new
        acc_ref[h] = acc_new

    def q_block(i, _):
        q0 = pl.multiple_of(i * MOBA_BLOCK, MOBA_BLOCK)
        jp = jnp.maximum(i - 1, 0)
        n_far = jnp.maximum(i - 1, 0)
        for h in range(ATTN_HEADS):
            qp_ref[h] = padded_q(h, q0)

        for h in range(ATTN_HEADS):
            gate = jnp.dot(kmean_ref[:, lane_slab(h)].astype(BF16), qp_ref[h], preferred_element_type=F32)
            gate = jnp.where(blk < i, gate, NEG_INF)
            sel = jnp.zeros(gate.shape, F32)
            for r in range(MOBA_TOPK):
                top = jnp.max(gate, axis=0, keepdims=True)
                first = jnp.min(jnp.where(gate == top, blk, n_blocks), axis=0, keepdims=True)
                hit = blk == first
                sel = jnp.where(hit & (r < i), 1.0, sel)
                gate = jnp.where(hit, -jnp.inf, gate)
            sel_ref[h] = sel

        tiles = []
        for h in range(ATTN_HEADS):
            tiles.append((h, i, bias_ref.at[h, 0], None))
            tiles.append((h, jp, bias_ref.at[h, 1], sel_ref[h, pl.ds(jp, 1), :] > 0.5))
        last_far = jnp.maximum(n_far - 1, 0)

        def prefetch_ids(t):
            return [(h, jnp.minimum(j, last_far)) for h, j in far_tile_ids(t)[:QK_LOOKAHEAD]]

        parts = run_tiles(tiles, carried_in=False, prefetch=prefetch_ids(0))
        for h in range(ATTN_HEADS):
            merge(h, parts[2 * h:2 * h + 2], first=True)

        def far_step(t, _):
            tiles = []
            for h, j in far_tile_ids(t):
                jc = jnp.minimum(j, last_far)
                sel = (sel_ref[h, pl.ds(jc, 1), :] > 0.5) & (j < n_far)
                tiles.append((h, jc, None, sel))
            parts = run_tiles(tiles, carried_in=True, prefetch=prefetch_ids(t + 1))
            for h in range(ATTN_HEADS):
                merge(h, parts[h * FAR_TILES_PER_STEP:(h + 1) * FAR_TILES_PER_STEP], first=False)
            return 0

        lax.fori_loop(0, (n_far + FAR_TILES_PER_STEP - 1) // FAR_TILES_PER_STEP, far_step, 0)

        for slab in range(ATTN_HEADS // HEADS_PER_LANE_TILE):
            heads = range(slab * HEADS_PER_LANE_TILE, (slab + 1) * HEADS_PER_LANE_TILE)
            out_t = jnp.concatenate([acc_ref[h] / ml_ref[h, 1:2, :] for h in heads], axis=0)
            o_ref[pl.ds(q0, MOBA_BLOCK), slab * V7X_LANES:(slab + 1) * V7X_LANES] = out_t.T.astype(o_ref.dtype)
        return 0

    lax.fori_loop(0, n_blocks, q_block, 0)


def _moba_attention(qt, k, vt, kmean, bias):
    b, _, s = qt.shape
    n_blocks = s // MOBA_BLOCK
    single = dict(pipeline_mode=pl.Buffered(1))
    t_spec = pl.BlockSpec((None, ATTN_WIDTH, s), lambda bi, hi: (bi, hi, 0), **single)
    return pl.pallas_call(
        _attn_kernel,
        out_shape=jax.ShapeDtypeStruct((b, s, D_MODEL), BF16),
        grid=(b, N_HEADS // ATTN_HEADS),
        in_specs=[t_spec,
                  pl.BlockSpec((None, s, ATTN_WIDTH), lambda bi, hi: (bi, 0, hi), **single),
                  t_spec,
                  pl.BlockSpec((None, n_blocks, ATTN_WIDTH), lambda bi, hi: (bi, 0, hi)),
                  pl.BlockSpec((ATTN_HEADS, 2, MOBA_BLOCK, MOBA_BLOCK), lambda bi, hi: (hi, 0, 0, 0), **single)],
        out_specs=pl.BlockSpec((None, s, ATTN_WIDTH), lambda bi, hi: (bi, 0, hi)),
        scratch_shapes=[pltpu.VMEM((ATTN_HEADS, n_blocks, MOBA_BLOCK), F32),
                        pltpu.VMEM((ATTN_HEADS, 2, MOBA_BLOCK), F32),
                        pltpu.VMEM((ATTN_HEADS, HEAD_DIM, MOBA_BLOCK), F32),
                        pltpu.VMEM((ATTN_HEADS, V7X_LANES, MOBA_BLOCK), BF16),
                        pltpu.VMEM((QK_LOOKAHEAD, MOBA_BLOCK, MOBA_BLOCK), F32)],
        compiler_params=_params("arbitrary", "arbitrary"),
        name="moba_attention",
    )(qt, k, vt, kmean, bias)


OPROJ_TM = 512


def _oproj_kernel(a_ref, w_ref, x_ref, lng_ref, lnb_ref, wr_ref, br_ref, o_ref, route_ref):
    h = jnp.dot(a_ref[...], w_ref[...], preferred_element_type=F32)
    out = _layer_norm(DEEPNORM_ALPHA * x_ref[...] + h, lng_ref[...], lnb_ref[...])
    o_ref[...] = out
    route_ref[...] = _route(out, wr_ref[...], br_ref[...])


def _out_project(a, w_o, x, ln_g, ln_b, wr_t, br):
    t = x.shape[0]
    row_spec = pl.BlockSpec((OPROJ_TM, D_MODEL), lambda i: (i, 0))
    return pl.pallas_call(
        _oproj_kernel,
        out_shape=(jax.ShapeDtypeStruct((t, D_MODEL), F32), jax.ShapeDtypeStruct((ROUTER_ROWS, t), F32)),
        grid=(t // OPROJ_TM,),
        in_specs=[row_spec, _const_spec(w_o.shape), row_spec,
                  _const_spec(ln_g.shape), _const_spec(ln_b.shape),
                  _const_spec(wr_t.shape), _const_spec(br.shape)],
        out_specs=(row_spec, pl.BlockSpec((ROUTER_ROWS, OPROJ_TM), lambda i: (0, i))),
        compiler_params=_params("arbitrary"),
        name="out_project",
    )(a, w_o, x, ln_g, ln_b, wr_t, br)


def kernel(x, ln_g, ln_b, a_w_in, a_v_g, a_v_b, a_w_s, a_b_s, a_w_out, w_k, w_v, b_w_q, b_w_o,
           rel_bias, w_router, b_router, e_w1, e_w3, e_w2):
    batch, seq, _ = x.shape
    t = batch * seq
    n_blocks = seq // MOBA_BLOCK
    xt = x.reshape(t, D_MODEL)
    wr_t = w_router.T.astype(BF16)
    br = b_router.reshape(N_EXPERTS, 1)
    n_tiles = t // MOE_TM + MOE_CLASSES
    sorted_buf = jnp.zeros((n_tiles * MOE_TM * FEATURE_ROWS, V7X_LANES), F32)

    def ln_rows(l, j):
        return ln_g[l, j].reshape(1, D_MODEL), ln_b[l, j].reshape(1, D_MODEL)

    q_scale = HEAD_DIM ** -0.5 * LOG2_E
    k = vt = kmean = bias = qt = None
    for l in range(DEPTH):
        g0, b0 = ln_rows(l, 0)
        if l < N_A_LAYERS:
            xt, routed = _gmlp_layer(xt, a_w_in[l].astype(BF16),
                                     a_v_g[l].reshape(1, GMLP_HALF), a_v_b[l].reshape(1, GMLP_HALF),
                                     a_w_s[l], a_b_s[l].T, a_w_out[l].astype(BF16), g0, b0, wr_t, br)
        else:
            x3 = xt.reshape(batch, seq, D_MODEL)
            if l == N_A_LAYERS:
                k, vt, kmean = _project_kv(x3, w_k.astype(BF16), w_v.T.astype(BF16))
                kmean = kmean.reshape(batch, n_blocks, D_MODEL)
                bias = _bias_tables(rel_bias)
            j = l - N_A_LAYERS
            a = _moba_attention(qt, k, vt, kmean, bias)
            xt, routed = _out_project(a.reshape(t, D_MODEL), b_w_o[j].astype(BF16), xt, g0, b0, wr_t, br)
        g1, b1 = ln_rows(l, 1)
        if N_A_LAYERS <= l + 1 < DEPTH:
            next_q = dict(wq_t=b_w_q[l + 1 - N_A_LAYERS].T.astype(BF16), q_scale=q_scale, batch=batch)
            (xt, qt), sorted_buf = _moe_layer(xt, routed, sorted_buf, e_w1, e_w3, e_w2, g1, b1, layer=l, **next_q)
        else:
            xt, sorted_buf = _moe_layer(xt, routed, sorted_buf, e_w1, e_w3, e_w2, g1, b1, layer=l)
    return xt.reshape(batch, seq, D_MODEL)
```

```python
import functools
import math

import numpy as np
import jax
import jax.numpy as jnp
from jax import lax
from jax.experimental import pallas as pl
from jax.experimental.pallas import tpu as pltpu

D_MODEL = 1024
DEPTH = 4
N_A_LAYERS = DEPTH // 2
GMLP_HALF = 3 * D_MODEL
GMLP_GROUPS = 8
GMLP_GROUP_WIDTH = GMLP_HALF // GMLP_GROUPS
GMLP_CHUNK = 128
N_HEADS = 16
HEAD_DIM = D_MODEL // N_HEADS
MOBA_BLOCK = 256
MOBA_TOPK = 3
N_BUCKETS = 32
MAX_DISTANCE = 128
N_EXPERTS = 16
N_GROUPS = 4
EXPERTS_PER_GROUP = N_EXPERTS // N_GROUPS
TOP_K = 2
MOE_PAIRS = EXPERTS_PER_GROUP * (EXPERTS_PER_GROUP - 1) // 2
MOE_CLASSES = N_GROUPS * MOE_PAIRS
ROUTER_ROWS = 8
D_EXPERT = D_MODEL // 4
DEEPNORM_ALPHA = (2 * DEPTH) ** 0.25
LN_EPS = 1e-5
NEG_INF = -1e30

V7X_VMEM_LIMIT_BYTES = 56 * 1024 * 1024
V7X_LANES = 128
BF16_SUBLANE_TILE = 16

BF16 = jnp.bfloat16
F32 = jnp.float32

_NT = (((1,), (1,)), ((), ()))


def _params(*semantics):
    return pltpu.CompilerParams(dimension_semantics=semantics,
                                vmem_limit_bytes=V7X_VMEM_LIMIT_BYTES)


def _const_spec(shape):
    zeros = (0,) * len(shape)
    return pl.BlockSpec(shape, lambda *_: zeros, pipeline_mode=pl.Buffered(1))


def _layer_norm(y, g, b):
    mu = jnp.mean(y, axis=-1, keepdims=True)
    d = y - mu
    var = jnp.mean(d * d, axis=-1, keepdims=True)
    return d * lax.rsqrt(var + LN_EPS) * g + b


GMLP_TM = 512
GMLP_GROUPS_PER_STEP = 2


def _gmlp_kernel(x_ref, w_in_ref, vg_ref, vb_ref, ws_ref, bs_ref, w_out_ref,
                 lng_ref, lnb_ref, wr_ref, br_ref, o_ref, route_ref, vn_ref, sv_ref):
    x = x_ref[...]
    xb = x.astype(BF16)
    v = jax.nn.gelu(jnp.dot(xb, w_in_ref[:, GMLP_HALF:], preferred_element_type=F32))
    vn_ref[...] = _layer_norm(v, vg_ref[...], vb_ref[...]).astype(BF16)

    row = lax.broadcasted_iota(jnp.int32, (GMLP_CHUNK, GMLP_CHUNK), 0)
    col = lax.broadcasted_iota(jnp.int32, (GMLP_CHUNK, GMLP_CHUNK), 1)
    causal = row >= col
    width = GMLP_GROUPS_PER_STEP * GMLP_GROUP_WIDTH
    acc = jnp.zeros((GMLP_TM, D_MODEL), F32)
    for step in range(GMLP_GROUPS // GMLP_GROUPS_PER_STEP):
        lo = step * width
        u = jax.nn.gelu(jnp.dot(xb, w_in_ref[:, lo:lo + width], preferred_element_type=F32))
        for gi in range(GMLP_GROUPS_PER_STEP):
            g = step * GMLP_GROUPS_PER_STEP + gi
            w_s = jnp.where(causal, ws_ref[g], 0.0).astype(BF16)
            b_s = bs_ref[:, g:g + 1]
            glo = g * GMLP_GROUP_WIDTH
            for c in range(GMLP_TM // GMLP_CHUNK):
                rows = slice(c * GMLP_CHUNK, (c + 1) * GMLP_CHUNK)
                vc = vn_ref[rows, glo:glo + GMLP_GROUP_WIDTH]
                sv_ref[rows, gi * GMLP_GROUP_WIDTH:(gi + 1) * GMLP_GROUP_WIDTH] = (
                    jnp.dot(w_s, vc, preferred_element_type=F32) + b_s)
        gated = (u * sv_ref[...]).astype(BF16)
        acc = acc + jnp.dot(gated, w_out_ref[lo:lo + width, :], preferred_element_type=F32)
    out = _layer_norm(DEEPNORM_ALPHA * x + acc, lng_ref[...], lnb_ref[...])
    o_ref[...] = out
    route_ref[...] = _route(out, wr_ref[...], br_ref[...])


def _gmlp_layer(x, w_in, v_g, v_b, w_s, b_s_t, w_out, ln_g, ln_b, wr_t, br):
    t = x.shape[0]
    row_spec = pl.BlockSpec((GMLP_TM, D_MODEL), lambda i: (i, 0))
    return pl.pallas_call(
        _gmlp_kernel,
        out_shape=(jax.ShapeDtypeStruct((t, D_MODEL), F32), jax.ShapeDtypeStruct((ROUTER_ROWS, t), F32)),
        grid=(t // GMLP_TM,),
        in_specs=[row_spec,
                  _const_spec(w_in.shape), _const_spec(v_g.shape), _const_spec(v_b.shape),
                  _const_spec(w_s.shape), _const_spec(b_s_t.shape), _const_spec(w_out.shape),
                  _const_spec(ln_g.shape), _const_spec(ln_b.shape),
                  _const_spec(wr_t.shape), _const_spec(br.shape)],
        out_specs=(row_spec, pl.BlockSpec((ROUTER_ROWS, GMLP_TM), lambda i: (0, i))),
        scratch_shapes=[pltpu.VMEM((GMLP_TM, GMLP_HALF), BF16),
                        pltpu.VMEM((GMLP_TM, GMLP_GROUPS_PER_STEP * GMLP_GROUP_WIDTH), F32)],
        compiler_params=_params("arbitrary"),
        name="gmlp_layer",
    )(x, w_in, v_g, v_b, w_s, b_s_t, w_out, ln_g, ln_b, wr_t, br)


def _route(x, wr_t, br):
    logits = lax.dot_general(wr_t, x.astype(BF16), _NT, preferred_element_type=F32) + br
    m = jnp.max(logits, axis=0, keepdims=True)
    ex = jnp.exp(logits - m)
    probs = ex / jnp.sum(ex, axis=0, keepdims=True)
    p = [probs[e:e + 1, :] for e in range(N_EXPERTS)]

    scores = []
    for g in range(N_GROUPS):
        q = p[g * EXPERTS_PER_GROUP:(g + 1) * EXPERTS_PER_GROUP]
        best = None
        for a in range(EXPERTS_PER_GROUP):
            for b in range(a + 1, EXPERTS_PER_GROUP):
                s = q[a] + q[b]
                best = s if best is None else jnp.maximum(best, s)
        scores.append(best)
    best_score = scores[0]
    gsel = jnp.zeros_like(best_score, dtype=jnp.int32)
    for g in range(1, N_GROUPS):
        better = scores[g] > best_score
        best_score = jnp.where(better, scores[g], best_score)
        gsel = jnp.where(better, g, gsel)

    slot_chosen = [None] * EXPERTS_PER_GROUP
    slot_gate = [None] * EXPERTS_PER_GROUP
    for e in range(N_EXPERTS):
        g, k = divmod(e, EXPERTS_PER_GROUP)
        rank = jnp.zeros_like(gsel)
        for e2 in range(g * EXPERTS_PER_GROUP, (g + 1) * EXPERTS_PER_GROUP):
            if e2 == e:
                continue
            ahead = (p[e2] > p[e]) | ((p[e2] == p[e]) & (e2 < e))
            rank = rank + ahead.astype(jnp.int32)
        chosen = (gsel == g) & (rank < TOP_K)
        gate = jnp.where(chosen, p[e] / best_score, 0.0)
        slot_chosen[k] = chosen if slot_chosen[k] is None else slot_chosen[k] | chosen
        slot_gate[k] = gate if slot_gate[k] is None else slot_gate[k] + gate
    c0, c1, c2, c3 = slot_chosen
    lo = jnp.where(c0, 0, jnp.where(c1, 1, 2))
    hi = jnp.where(c3, 3, jnp.where(c2, 2, 1))
    w_lo = jnp.where(c0, slot_gate[0], jnp.where(c1, slot_gate[1], slot_gate[2]))
    w_hi = jnp.where(c3, slot_gate[3], jnp.where(c2, slot_gate[2], slot_gate[1]))
    pair = jnp.right_shift(lo * (7 - lo), 1) + hi - lo - 1
    cls = gsel * MOE_PAIRS + pair
    zero = jnp.zeros_like(w_lo)
    return jnp.concatenate([cls.astype(F32), w_lo, w_hi] + [zero] * (ROUTER_ROWS - 3), axis=0)


MOE_TM = 256
FEATURE_ROWS = D_MODEL // V7X_LANES
DISPATCH_TD = 512
DMA_UNROLL = 8


def _token_rows(ref, n_tokens):
    return jnp.concatenate(
        [ref[pl.ds(c, n_tokens, stride=FEATURE_ROWS), :] for c in range(FEATURE_ROWS)], axis=1)


def _store_token_rows(ref, value, n_tokens):
    for c in range(FEATURE_ROWS):
        ref[pl.ds(c, n_tokens, stride=FEATURE_ROWS), :] = value[:, c * V7X_LANES:(c + 1) * V7X_LANES]


def _dispatch_kernel(pos_ref, x_ref, gate_ref, buf_in_ref, buf_ref, gate_tbl_ref, stage_ref, sems):
    del buf_in_ref
    i = pl.program_id(0)
    n = pl.num_programs(0)
    slot = i % 2
    stage = stage_ref.at[slot]
    rows = DISPATCH_TD * FEATURE_ROWS

    @pl.when(i == 0)
    def _():
        gate_tbl_ref[...] = jnp.zeros_like(gate_tbl_ref)

    def wait_slot(s):
        pltpu.make_async_copy(stage_ref.at[s], buf_ref.at[pl.ds(0, rows)], sems.at[s]).wait()

    @pl.when(i >= 2)
    def _():
        wait_slot(slot)

    _store_token_rows(stage, x_ref[...], DISPATCH_TD)

    def issue(r, _):
        p = pos_ref[i * DISPATCH_TD + r]
        dst = pl.multiple_of(p * FEATURE_ROWS, FEATURE_ROWS)
        src = pl.multiple_of(r * FEATURE_ROWS, FEATURE_ROWS)
        pltpu.make_async_copy(stage.at[pl.ds(src, FEATURE_ROWS)], buf_ref.at[pl.ds(dst, FEATURE_ROWS)],
                              sems.at[slot]).start()
        gate_tbl_ref[pl.ds(p, 1), :] = gate_ref[pl.ds(r, 1), :]
        return 0

    lax.fori_loop(0, DISPATCH_TD, issue, 0, unroll=DMA_UNROLL)

    @pl.when(i == n - 1)
    def _():
        wait_slot(slot)

        @pl.when(n > 1)
        def _():
            wait_slot(1 - slot)


def _dispatch(pos, x, gates, buf):
    t = x.shape[0]
    n_slots = buf.shape[0] // FEATURE_ROWS
    return pl.pallas_call(
        _dispatch_kernel,
        out_shape=(jax.ShapeDtypeStruct(buf.shape, buf.dtype),
                   jax.ShapeDtypeStruct((n_slots, V7X_LANES), F32)),
        grid_spec=pltpu.PrefetchScalarGridSpec(
            num_scalar_prefetch=1,
            grid=(t // DISPATCH_TD,),
            in_specs=[pl.BlockSpec((DISPATCH_TD, D_MODEL), lambda i, pos: (i, 0)),
                      pl.BlockSpec((DISPATCH_TD, V7X_LANES), lambda i, pos: (i, 0)),
                      pl.BlockSpec(memory_space=pl.ANY)],
            out_specs=(pl.BlockSpec(memory_space=pl.ANY),
                       pl.BlockSpec((n_slots, V7X_LANES), lambda i, pos: (0, 0))),
            scratch_shapes=[pltpu.VMEM((2, DISPATCH_TD * FEATURE_ROWS, V7X_LANES), F32),
                            pltpu.SemaphoreType.DMA((2,))]),
        input_output_aliases={3: 0},
        compiler_params=_params("arbitrary"),
        name="moe_dispatch",
    )(pos, x, gates, buf)


def _expert_kernel(ea_ref, eb_ref, fresh_ref, used_ref, xs_ref, gate_ref,
                   w1a_ref, w1b_ref, w3a_ref, w3b_ref, w2a_ref, w2b_ref, ys_ref,
                   w1_s, w3_s, w2_s):
    del ea_ref, eb_ref
    i = pl.program_id(0)

    @pl.when(fresh_ref[i] == 1)
    def _():
        for e, (w1, w3, w2) in enumerate(((w1a_ref, w3a_ref, w2a_ref), (w1b_ref, w3b_ref, w2b_ref))):
            w1_s[e] = w1[...].astype(BF16)
            w3_s[e] = w3[...].astype(BF16)
            w2_s[e] = w2[...].astype(BF16)

    @pl.when(used_ref[i] == 1)
    def _():
        xb = _token_rows(xs_ref, MOE_TM).astype(BF16)
        gates = gate_ref[...]
        y = None
        for e in range(TOP_K):
            h1 = jnp.dot(xb, w1_s[e], preferred_element_type=F32)
            h3 = jnp.dot(xb, w3_s[e], preferred_element_type=F32)
            h = (jax.nn.silu(h1) * h3 * gates[:, e:e + 1]).astype(BF16)
            ye = jnp.dot(h, w2_s[e], preferred_element_type=F32)
            y = ye if y is None else y + ye
        _store_token_rows(ys_ref, y, MOE_TM)

    @pl.when(used_ref[i] == 0)
    def _():
        ys_ref[...] = jnp.zeros_like(ys_ref)


def _expert_ffn(ea, eb, fresh, used, xs, gates, w1, w3, w2, layer):
    n_tiles = ea.shape[0]

    def w_spec(shape, which):
        return pl.BlockSpec((None, None) + shape,
                            lambda i, ea, eb, fresh, used: (layer, (ea, eb)[which][i], 0, 0))

    up, down = (D_MODEL, D_EXPERT), (D_EXPERT, D_MODEL)
    return pl.pallas_call(
        _expert_kernel,
        out_shape=jax.ShapeDtypeStruct((n_tiles * MOE_TM * FEATURE_ROWS, V7X_LANES), F32),
        grid_spec=pltpu.PrefetchScalarGridSpec(
            num_scalar_prefetch=4,
            grid=(n_tiles,),
            in_specs=[pl.BlockSpec((MOE_TM * FEATURE_ROWS, V7X_LANES), lambda i, *_: (i, 0)),
                      pl.BlockSpec((MOE_TM, V7X_LANES), lambda i, *_: (i, 0)),
                      w_spec(up, 0), w_spec(up, 1), w_spec(up, 0), w_spec(up, 1),
                      w_spec(down, 0), w_spec(down, 1)],
            out_specs=pl.BlockSpec((MOE_TM * FEATURE_ROWS, V7X_LANES), lambda i, *_: (i, 0)),
            scratch_shapes=[pltpu.VMEM((TOP_K,) + up, BF16), pltpu.VMEM((TOP_K,) + up, BF16),
                            pltpu.VMEM((TOP_K,) + down, BF16)]),
        compiler_params=_params("arbitrary"),
        name="moe_experts",
    )(ea, eb, fresh, used, xs, gates, w1, w1, w3, w3, w2, w2)


def _gather_and_norm(pos_ref, ys_ref, x_ref, lng_ref, lnb_ref, stage_ref, sems):
    i = pl.program_id(0)
    n = pl.num_programs(0)
    slot = i % 2
    rows = DISPATCH_TD * FEATURE_ROWS

    def issue_step(step, s):
        def issue(r, _):
            src = pl.multiple_of(pos_ref[step * DISPATCH_TD + r] * FEATURE_ROWS, FEATURE_ROWS)
            dst = pl.multiple_of(r * FEATURE_ROWS, FEATURE_ROWS)
            pltpu.make_async_copy(ys_ref.at[pl.ds(src, FEATURE_ROWS)],
                                  stage_ref.at[s, pl.ds(dst, FEATURE_ROWS)], sems.at[s]).start()
            return 0
        lax.fori_loop(0, DISPATCH_TD, issue, 0, unroll=DMA_UNROLL)

    @pl.when(i == 0)
    def _():
        issue_step(0, 0)

    @pl.when(i + 1 < n)
    def _():
        issue_step(i + 1, 1 - slot)

    pltpu.make_async_copy(ys_ref.at[pl.ds(0, rows)], stage_ref.at[slot], sems.at[slot]).wait()

    f = _token_rows(stage_ref.at[slot], DISPATCH_TD)
    return _layer_norm(DEEPNORM_ALPHA * x_ref[...] + f, lng_ref[...], lnb_ref[...])


def _combine_kernel(pos_ref, ys_ref, x_ref, lng_ref, lnb_ref, o_ref, stage_ref, sems):
    o_ref[...] = _gather_and_norm(pos_ref, ys_ref, x_ref, lng_ref, lnb_ref, stage_ref, sems)


def _combine_q_kernel(pos_ref, ys_ref, x_ref, lng_ref, lnb_ref, wq_ref, o_ref, qt_ref, stage_ref, sems,
                      *, q_scale):
    out = _gather_and_norm(pos_ref, ys_ref, x_ref, lng_ref, lnb_ref, stage_ref, sems)
    o_ref[...] = out
    q = lax.dot_general(wq_ref[...], out.astype(BF16), _NT, preferred_element_type=F32)
    qt_ref[...] = (q * q_scale).astype(qt_ref.dtype)


def _combine(pos, ys, x, ln_g, ln_b, wq_t=None, q_scale=None, batch=None):
    t = x.shape[0]
    row_spec = pl.BlockSpec((DISPATCH_TD, D_MODEL), lambda i, pos: (i, 0))
    vec_spec = pl.BlockSpec((1, D_MODEL), lambda i, pos: (0, 0))
    in_specs = [pl.BlockSpec(memory_space=pl.ANY), row_spec, vec_spec, vec_spec]
    out_shape = jax.ShapeDtypeStruct((t, D_MODEL), F32)
    out_specs = row_spec
    body, args = _combine_kernel, (pos, ys, x, ln_g, ln_b)
    if wq_t is not None:
        steps_per_batch = t // batch // DISPATCH_TD
        body = functools.partial(_combine_q_kernel, q_scale=q_scale)
        args = args + (wq_t,)
        in_specs = in_specs + [pl.BlockSpec(wq_t.shape, lambda i, pos: (0, 0), pipeline_mode=pl.Buffered(1))]
        out_shape = (out_shape, jax.ShapeDtypeStruct((batch, D_MODEL, t // batch), BF16))
        out_specs = (row_spec, pl.BlockSpec((None, D_MODEL, DISPATCH_TD),
                                            lambda i, pos: (i // steps_per_batch, 0, i % steps_per_batch)))
    return pl.pallas_call(
        body,
        out_shape=out_shape,
        grid_spec=pltpu.PrefetchScalarGridSpec(
            num_scalar_prefetch=1,
            grid=(t // DISPATCH_TD,),
            in_specs=in_specs,
            out_specs=out_specs,
            scratch_shapes=[pltpu.VMEM((2, DISPATCH_TD * FEATURE_ROWS, V7X_LANES), F32),
                            pltpu.SemaphoreType.DMA((2,))]),
        compiler_params=_params("arbitrary"),
        name="moe_combine",
    )(*args)


def _pair_table():
    lo, hi = [], []
    for g in range(N_GROUPS):
        for a in range(EXPERTS_PER_GROUP):
            for b in range(a + 1, EXPERTS_PER_GROUP):
                lo.append(g * EXPERTS_PER_GROUP + a)
                hi.append(g * EXPERTS_PER_GROUP + b)
    return np.asarray(lo, np.int32), np.asarray(hi, np.int32)


def _sort_plan(cls, n_tiles):
    onehot = (cls[:, None] == jnp.arange(MOE_CLASSES, dtype=jnp.int32)[None, :]).astype(jnp.int32)
    running = jnp.cumsum(onehot, axis=0)
    counts = running[-1]
    padded = (counts + MOE_TM - 1) // MOE_TM * MOE_TM
    ends = jnp.cumsum(padded)
    rank = jnp.sum(running * onehot, axis=1) - 1
    pos = jnp.sum(onehot * (ends - padded)[None, :], axis=1) + rank
    tile_cls = jnp.searchsorted(ends, jnp.arange(n_tiles, dtype=jnp.int32) * MOE_TM, side="right")
    used = (tile_cls < MOE_CLASSES).astype(jnp.int32)
    tile_cls = jnp.minimum(tile_cls, MOE_CLASSES - 1).astype(jnp.int32)
    fresh = jnp.concatenate([jnp.ones((1,), jnp.int32),
                             (tile_cls[1:] != tile_cls[:-1]).astype(jnp.int32)])
    lo, hi = _pair_table()
    return pos.astype(jnp.int32), jnp.asarray(lo)[tile_cls], jnp.asarray(hi)[tile_cls], fresh, used


def _moe_layer(x, routed, buf, w1, w3, w2, ln_g, ln_b, layer, **next_q):
    n_tiles = buf.shape[0] // (MOE_TM * FEATURE_ROWS)
    pos, ea, eb, fresh, used = _sort_plan(routed[0].astype(jnp.int32), n_tiles)
    gate_rows = jnp.pad(routed[1:1 + TOP_K].T, ((0, 0), (0, V7X_LANES - TOP_K)))
    buf, gates = _dispatch(pos, x, gate_rows, buf)
    ys = _expert_ffn(ea, eb, fresh, used, buf, gates, w1, w3, w2, layer)
    return _combine(pos, ys, x, ln_g, ln_b, **next_q), buf


PROJ_TM = 512


def _kv_kernel(x_ref, wk_ref, wvt_ref, k_ref, vt_ref, kmean_ref):
    xb = x_ref[...].astype(BF16)
    k = jnp.dot(xb, wk_ref[...], preferred_element_type=F32)
    k_ref[...] = k.astype(BF16)
    for j in range(PROJ_TM // MOBA_BLOCK):
        kmean_ref[j] = jnp.mean(k[j * MOBA_BLOCK:(j + 1) * MOBA_BLOCK], axis=0, keepdims=True)
    vt_ref[...] = lax.dot_general(wvt_ref[...], xb, _NT, preferred_element_type=F32).astype(BF16)


def _project_kv(x, w_k, w_v_t):
    b, s, _ = x.shape
    blocks_per_tile = PROJ_TM // MOBA_BLOCK
    tiles = s // PROJ_TM
    return pl.pallas_call(
        _kv_kernel,
        out_shape=(jax.ShapeDtypeStruct((b, s, D_MODEL), BF16),
                   jax.ShapeDtypeStruct((b, D_MODEL, s), BF16),
                   jax.ShapeDtypeStruct((b * s // MOBA_BLOCK, 1, D_MODEL), F32)),
        grid=(b, tiles),
        in_specs=[pl.BlockSpec((None, PROJ_TM, D_MODEL), lambda bi, i: (bi, i, 0)),
                  _const_spec(w_k.shape), _const_spec(w_v_t.shape)],
        out_specs=(pl.BlockSpec((None, PROJ_TM, D_MODEL), lambda bi, i: (bi, i, 0)),
                   pl.BlockSpec((None, D_MODEL, PROJ_TM), lambda bi, i: (bi, 0, i)),
                   pl.BlockSpec((blocks_per_tile, 1, D_MODEL), lambda bi, i: (bi * tiles + i, 0, 0))),
        compiler_params=_params("arbitrary", "arbitrary"),
        name="project_kv",
    )(x, w_k, w_v_t)


def _t5_bucket_np(dist):
    n = np.maximum(dist, 0)
    max_exact = N_BUCKETS // 2
    nf = np.maximum(n, 1).astype(np.float32)
    large = max_exact + (np.log(nf / np.float32(max_exact)) / np.float32(math.log(MAX_DISTANCE / max_exact))
                         * np.float32(N_BUCKETS - max_exact)).astype(np.int32)
    large = np.minimum(large, N_BUCKETS - 1)
    return np.where(n < max_exact, n, large).astype(np.int32)


def _bucket_tables():
    kpos = np.arange(MOBA_BLOCK)[:, None]
    qpos = np.arange(MOBA_BLOCK)[None, :]
    own = np.where(qpos >= kpos, _t5_bucket_np(qpos - kpos), -1)
    prev = _t5_bucket_np(qpos + MOBA_BLOCK - kpos)
    return np.stack([own, prev]).astype(np.int32)


def _bias_kernel(rel_ref, bucket_ref, o_ref):
    h = pl.program_id(0)
    far = rel_ref[N_BUCKETS - 1, h]
    for t in range(2):
        bucket = bucket_ref[t]
        acc = jnp.zeros(bucket.shape, F32)
        for b in range(N_BUCKETS):
            acc = jnp.where(bucket == b, rel_ref[b, h], acc)
        o_ref[t] = jnp.where(bucket < 0, NEG_INF, (acc - far) * LOG2_E)


def _bias_tables(rel_bias):
    buckets = jnp.asarray(_bucket_tables())
    return pl.pallas_call(
        _bias_kernel,
        out_shape=jax.ShapeDtypeStruct((N_HEADS, 2, MOBA_BLOCK, MOBA_BLOCK), F32),
        grid=(N_HEADS,),
        in_specs=[pl.BlockSpec(memory_space=pltpu.SMEM), _const_spec(buckets.shape)],
        out_specs=pl.BlockSpec((None, 2, MOBA_BLOCK, MOBA_BLOCK), lambda h: (h, 0, 0, 0)),
        compiler_params=_params("arbitrary"),
        name="rel_bias_tables",
    )(rel_bias, buckets)


ATTN_HEADS = 8
ATTN_WIDTH = ATTN_HEADS * HEAD_DIM
HEADS_PER_LANE_TILE = V7X_LANES // HEAD_DIM
FAR_TILES_PER_STEP = 3
QK_LOOKAHEAD = 6
LOG2_E = math.log2(math.e)


def _attn_kernel(qt_ref, k_ref, vt_ref, kmean_ref, bias_ref, o_ref,
                 sel_ref, ml_ref, acc_ref, qp_ref, sc_ref):
    n_blocks = kmean_ref.shape[0]
    blk = lax.broadcasted_iota(jnp.int32, (n_blocks, MOBA_BLOCK), 0)
    causal = (lax.broadcasted_iota(jnp.int32, (MOBA_BLOCK, MOBA_BLOCK), 0)
              <= lax.broadcasted_iota(jnp.int32, (MOBA_BLOCK, MOBA_BLOCK), 1))
    ones_rows = jnp.ones((BF16_SUBLANE_TILE, MOBA_BLOCK), BF16)

    def lane_slab(h):
        return slice((h // HEADS_PER_LANE_TILE) * V7X_LANES, (h // HEADS_PER_LANE_TILE + 1) * V7X_LANES)

    def padded_q(h, q0):
        qh = qt_ref[h * HEAD_DIM:(h + 1) * HEAD_DIM, pl.ds(q0, MOBA_BLOCK)]
        pieces = [jnp.zeros_like(qh)] * HEADS_PER_LANE_TILE
        pieces[h % HEADS_PER_LANE_TILE] = qh
        return jnp.concatenate(pieces, axis=0)

    def scores(h, j):
        k0 = pl.multiple_of(j * MOBA_BLOCK, MOBA_BLOCK)
        kj = k_ref[pl.ds(k0, MOBA_BLOCK), lane_slab(h)]
        return jnp.dot(kj, qp_ref[h], preferred_element_type=F32)

    def weights_and_values(h, j, s, bias, sel):
        if bias is not None:
            keep = causal if sel is None else sel
            s = jnp.where(keep, s, NEG_INF) + bias[...]
        m_loc = jnp.max(s, axis=0, keepdims=True)
        if sel is None:
            shift = m_loc
        else:
            shift = jnp.where(sel, m_loc, -NEG_INF)
            m_loc = jnp.where(sel, m_loc, NEG_INF)
        p = jnp.exp2(s - shift).astype(BF16)
        k0 = pl.multiple_of(j * MOBA_BLOCK, MOBA_BLOCK)
        vtj = vt_ref[h * HEAD_DIM:(h + 1) * HEAD_DIM, pl.ds(k0, MOBA_BLOCK)]
        pv = jnp.dot(jnp.concatenate([vtj, ones_rows], axis=0), p, preferred_element_type=F32)
        return m_loc, pv[HEAD_DIM:HEAD_DIM + 1], pv[:HEAD_DIM]

    def run_tiles(tiles, carried_in, prefetch):
        n = len(tiles)
        s = [None] * n
        out = [None] * n
        for t in range(QK_LOOKAHEAD if carried_in else 0, n + QK_LOOKAHEAD):
            if t < n:
                h, j, _, _ = tiles[t]
                s[t] = scores(h, j)
            else:
                h, j = prefetch[t - n]
                sc_ref[t - n] = scores(h, j)
            d = t - QK_LOOKAHEAD
            if d >= 0:
                h, j, bias, sel = tiles[d]
                sd = sc_ref[d] if carried_in and d < QK_LOOKAHEAD else s[d]
                out[d] = weights_and_values(h, j, sd, bias, sel)
                s[d] = None
        return out

    def far_tile_ids(t):
        return [(h, t * FAR_TILES_PER_STEP + u) for h in range(ATTN_HEADS) for u in range(FAR_TILES_PER_STEP)]

    def merge(h, parts, first):
        m_new = parts[0][0]
        for m_loc, _, _ in parts[1:]:
            m_new = jnp.maximum(m_new, m_loc)
        if first:
            l_new = jnp.zeros_like(m_new)
            acc_new = jnp.zeros((HEAD_DIM, MOBA_BLOCK), F32)
        else:
            m_old = ml_ref[h, 0:1, :]
            m_new = jnp.maximum(m_new, m_old)
            alpha = jnp.exp2(m_old - m_new)
            l_new = alpha * ml_ref[h, 1:2, :]
            acc_new = alpha * acc_ref[h]
        for m_loc, l_loc, pv in parts:
            w = jnp.exp2(m_loc - m_new)
            l_new = l_new + w * l_loc
            acc_new = acc_new + w * pv
        ml_ref[h, 0:1, :] = m_new
        ml_ref[h, 1:2, :] = l_new
        acc_ref[h] = acc_new

    def q_block(i, _):
        q0 = pl.multiple_of(i * MOBA_BLOCK, MOBA_BLOCK)
        jp = jnp.maximum(i - 1, 0)
        n_far = jnp.maximum(i - 1, 0)
        for h in range(ATTN_HEADS):
            qp_ref[h] = padded_q(h, q0)

        for h in range(ATTN_HEADS):
            gate = jnp.dot(kmean_ref[:, lane_slab(h)].astype(BF16), qp_ref[h], preferred_element_type=F32)
            gate = jnp.where(blk < i, gate, NEG_INF)
            sel = jnp.zeros(gate.shape, F32)
            for r in range(MOBA_TOPK):
                top = jnp.max(gate, axis=0, keepdims=True)
                first = jnp.min(jnp.where(gate == top, blk, n_blocks), axis=0, keepdims=True)
                hit = blk == first
                sel = jnp.where(hit & (r < i), 1.0, sel)
                gate = jnp.where(hit, -jnp.inf, gate)
            sel_ref[h] = sel

        tiles = []
        for h in range(ATTN_HEADS):
            tiles.append((h, i, bias_ref.at[h, 0], None))
            tiles.append((h, jp, bias_ref.at[h, 1], sel_ref[h, pl.ds(jp, 1), :] > 0.5))
        last_far = jnp.maximum(n_far - 1, 0)

        def prefetch_ids(t):
            return [(h, jnp.minimum(j, last_far)) for h, j in far_tile_ids(t)[:QK_LOOKAHEAD]]

        parts = run_tiles(tiles, carried_in=False, prefetch=prefetch_ids(0))
        for h in range(ATTN_HEADS):
            merge(h, parts[2 * h:2 * h + 2], first=True)

        def far_step(t, _):
            tiles = []
            for h, j in far_tile_ids(t):
                jc = jnp.minimum(j, last_far)
                sel = (sel_ref[h, pl.ds(jc, 1), :] > 0.5) & (j < n_far)
                tiles.append((h, jc, None, sel))
            parts = run_tiles(tiles, carried_in=True, prefetch=prefetch_ids(t + 1))
            for h in range(ATTN_HEADS):
                merge(h, parts[h * FAR_TILES_PER_STEP:(h + 1) * FAR_TILES_PER_STEP], first=False)
            return 0

        lax.fori_loop(0, (n_far + FAR_TILES_PER_STEP - 1) // FAR_TILES_PER_STEP, far_step, 0)

        for slab in range(ATTN_HEADS // HEADS_PER_LANE_TILE):
            heads = range(slab * HEADS_PER_LANE_TILE, (slab + 1) * HEADS_PER_LANE_TILE)
            out_t = jnp.concatenate([acc_ref[h] / ml_ref[h, 1:2, :] for h in heads], axis=0)
            o_ref[pl.ds(q0, MOBA_BLOCK), slab * V7X_LANES:(slab + 1) * V7X_LANES] = out_t.T.astype(o_ref.dtype)
        return 0

    lax.fori_loop(0, n_blocks, q_block, 0)


def _moba_attention(qt, k, vt, kmean, bias):
    b, _, s = qt.shape
    n_blocks = s // MOBA_BLOCK
    single = dict(pipeline_mode=pl.Buffered(1))
    t_spec = pl.BlockSpec((None, ATTN_WIDTH, s), lambda bi, hi: (bi, hi, 0), **single)
    return pl.pallas_call(
        _attn_kernel,
        out_shape=jax.ShapeDtypeStruct((b, s, D_MODEL), BF16),
        grid=(b, N_HEADS // ATTN_HEADS),
        in_specs=[t_spec,
                  pl.BlockSpec((None, s, ATTN_WIDTH), lambda bi, hi: (bi, 0, hi), **single),
                  t_spec,
                  pl.BlockSpec((None, n_blocks, ATTN_WIDTH), lambda bi, hi: (bi, 0, hi)),
                  pl.BlockSpec((ATTN_HEADS, 2, MOBA_BLOCK, MOBA_BLOCK), lambda bi, hi: (hi, 0, 0, 0), **single)],
        out_specs=pl.BlockSpec((None, s, ATTN_WIDTH), lambda bi, hi: (bi, 0, hi)),
        scratch_shapes=[pltpu.VMEM((ATTN_HEADS, n_blocks, MOBA_BLOCK), F32),
                        pltpu.VMEM((ATTN_HEADS, 2, MOBA_BLOCK), F32),
                        pltpu.VMEM((ATTN_HEADS, HEAD_DIM, MOBA_BLOCK), F32),
                        pltpu.VMEM((ATTN_HEADS, V7X_LANES, MOBA_BLOCK), BF16),
                        pltpu.VMEM((QK_LOOKAHEAD, MOBA_BLOCK, MOBA_BLOCK), F32)],
        compiler_params=_params("arbitrary", "arbitrary"),
        name="moba_attention",
    )(qt, k, vt, kmean, bias)


OPROJ_TM = 512


def _oproj_kernel(a_ref, w_ref, x_ref, lng_ref, lnb_ref, wr_ref, br_ref, o_ref, route_ref):
    h = jnp.dot(a_ref[...], w_ref[...], preferred_element_type=F32)
    out = _layer_norm(DEEPNORM_ALPHA * x_ref[...] + h, lng_ref[...], lnb_ref[...])
    o_ref[...] = out
    route_ref[...] = _route(out, wr_ref[...], br_ref[...])


def _out_project(a, w_o, x, ln_g, ln_b, wr_t, br):
    t = x.shape[0]
    row_spec = pl.BlockSpec((OPROJ_TM, D_MODEL), lambda i: (i, 0))
    return pl.pallas_call(
        _oproj_kernel,
        out_shape=(jax.ShapeDtypeStruct((t, D_MODEL), F32), jax.ShapeDtypeStruct((ROUTER_ROWS, t), F32)),
        grid=(t // OPROJ_TM,),
        in_specs=[row_spec, _const_spec(w_o.shape), row_spec,
                  _const_spec(ln_g.shape), _const_spec(ln_b.shape),
                  _const_spec(wr_t.shape), _const_spec(br.shape)],
        out_specs=(row_spec, pl.BlockSpec((ROUTER_ROWS, OPROJ_TM), lambda i: (0, i))),
        compiler_params=_params("arbitrary"),
        name="out_project",
    )(a, w_o, x, ln_g, ln_b, wr_t, br)


def kernel(x, ln_g, ln_b, a_w_in, a_v_g, a_v_b, a_w_s, a_b_s, a_w_out, w_k, w_v, b_w_q, b_w_o,
           rel_bias, w_router, b_router, e_w1, e_w3, e_w2):
    batch, seq, _ = x.shape
    t = batch * seq
    n_blocks = seq // MOBA_BLOCK
    xt = x.reshape(t, D_MODEL)
    wr_t = w_router.T.astype(BF16)
    br = b_router.reshape(N_EXPERTS, 1)
    n_tiles = t // MOE_TM + MOE_CLASSES
    sorted_buf = jnp.zeros((n_tiles * MOE_TM * FEATURE_ROWS, V7X_LANES), F32)

    def ln_rows(l, j):
        return ln_g[l, j].reshape(1, D_MODEL), ln_b[l, j].reshape(1, D_MODEL)

    q_scale = HEAD_DIM ** -0.5 * LOG2_E
    k = vt = kmean = bias = qt = None
    for l in range(DEPTH):
        g0, b0 = ln_rows(l, 0)
        if l < N_A_LAYERS:
            xt, routed = _gmlp_layer(xt, a_w_in[l].astype(BF16),
                                     a_v_g[l].reshape(1, GMLP_HALF), a_v_b[l].reshape(1, GMLP_HALF),
                                     a_w_s[l], a_b_s[l].T, a_w_out[l].astype(BF16), g0, b0, wr_t, br)
        else:
            x3 = xt.reshape(batch, seq, D_MODEL)
            if l == N_A_LAYERS:
                k, vt, kmean = _project_kv(x3, w_k.astype(BF16), w_v.T.astype(BF16))
                kmean = kmean.reshape(batch, n_blocks, D_MODEL)
                bias = _bias_tables(rel_bias)
            j = l - N_A_LAYERS
            a = _moba_attention(qt, k, vt, kmean, bias)
            xt, routed = _out_project(a.reshape(t, D_MODEL), b_w_o[j].astype(BF16), xt, g0, b0, wr_t, br)
        g1, b1 = ln_rows(l, 1)
        if N_A_LAYERS <= l + 1 < DEPTH:
            next_q = dict(wq_t=b_w_q[l + 1 - N_A_LAYERS].T.astype(BF16), q_scale=q_scale, batch=batch)
            (xt, qt), sorted_buf = _moe_layer(xt, routed, sorted_buf, e_w1, e_w3, e_w2, g1, b1, layer=l, **next_q)
        else:
            xt, sorted_buf = _moe_layer(xt, routed, sorted_buf, e_w1, e_w3, e_w2, g1, b1, layer=l)
    return xt.reshape(batch, seq, D_MODEL)
```

```python
import functools
import math

import numpy as np
import jax
import jax.numpy as jnp
from jax import lax
from jax.experimental import pallas as pl
from jax.experimental.pallas import tpu as pltpu

D_MODEL = 1024
DEPTH = 4
N_A_LAYERS = DEPTH // 2
GMLP_HALF = 3 * D_MODEL
GMLP_GROUPS = 8
GMLP_GROUP_WIDTH = GMLP_HALF // GMLP_GROUPS
GMLP_CHUNK = 128
N_HEADS = 16
HEAD_DIM = D_MODEL // N_HEADS
MOBA_BLOCK = 256
MOBA_TOPK = 3
N_BUCKETS = 32
MAX_DISTANCE = 128
N_EXPERTS = 16
N_GROUPS = 4
EXPERTS_PER_GROUP = N_EXPERTS // N_GROUPS
TOP_K = 2
MOE_PAIRS = EXPERTS_PER_GROUP * (EXPERTS_PER_GROUP - 1) // 2
MOE_CLASSES = N_GROUPS * MOE_PAIRS
ROUTER_ROWS = 8
D_EXPERT = D_MODEL // 4
DEEPNORM_ALPHA = (2 * DEPTH) ** 0.25
LN_EPS = 1e-5
NEG_INF = -1e30

V7X_VMEM_LIMIT_BYTES = 56 * 1024 * 1024
V7X_LANES = 128
BF16_SUBLANE_TILE = 16

BF16 = jnp.bfloat16
F32 = jnp.float32

_NT = (((1,), (1,)), ((), ()))


def _params(*semantics):
    return pltpu.CompilerParams(dimension_semantics=semantics,
                                vmem_limit_bytes=V7X_VMEM_LIMIT_BYTES)


def _const_spec(shape):
    zeros = (0,) * len(shape)
    return pl.BlockSpec(shape, lambda *_: zeros, pipeline_mode=pl.Buffered(1))


def _layer_norm(y, g, b):
    mu = jnp.mean(y, axis=-1, keepdims=True)
    d = y - mu
    var = jnp.mean(d * d, axis=-1, keepdims=True)
    return d * lax.rsqrt(var + LN_EPS) * g + b


GMLP_TM = 512
GMLP_GROUPS_PER_STEP = 2


def _gmlp_kernel(x_ref, w_in_ref, vg_ref, vb_ref, ws_ref, bs_ref, w_out_ref,
                 lng_ref, lnb_ref, wr_ref, br_ref, o_ref, route_ref, vn_ref, sv_ref):
    x = x_ref[...]
    xb = x.astype(BF16)
    v = jax.nn.gelu(jnp.dot(xb, w_in_ref[:, GMLP_HALF:], preferred_element_type=F32))
    vn_ref[...] = _layer_norm(v, vg_ref[...], vb_ref[...]).astype(BF16)

    row = lax.broadcasted_iota(jnp.int32, (GMLP_CHUNK, GMLP_CHUNK), 0)
    col = lax.broadcasted_iota(jnp.int32, (GMLP_CHUNK, GMLP_CHUNK), 1)
    causal = row >= col
    width = GMLP_GROUPS_PER_STEP * GMLP_GROUP_WIDTH
    acc = jnp.zeros((GMLP_TM, D_MODEL), F32)
    for step in range(GMLP_GROUPS // GMLP_GROUPS_PER_STEP):
        lo = step * width
        u = jax.nn.gelu(jnp.dot(xb, w_in_ref[:, lo:lo + width], preferred_element_type=F32))
        for gi in range(GMLP_GROUPS_PER_STEP):
            g = step * GMLP_GROUPS_PER_STEP + gi
            w_s = jnp.where(causal, ws_ref[g], 0.0).astype(BF16)
            b_s = bs_ref[:, g:g + 1]
            glo = g * GMLP_GROUP_WIDTH
            for c in range(GMLP_TM // GMLP_CHUNK):
                rows = slice(c * GMLP_CHUNK, (c + 1) * GMLP_CHUNK)
                vc = vn_ref[rows, glo:glo + GMLP_GROUP_WIDTH]
                sv_ref[rows, gi * GMLP_GROUP_WIDTH:(gi + 1) * GMLP_GROUP_WIDTH] = (
                    jnp.dot(w_s, vc, preferred_element_type=F32) + b_s)
        gated = (u * sv_ref[...]).astype(BF16)
        acc = acc + jnp.dot(gated, w_out_ref[lo:lo + width, :], preferred_element_type=F32)
    out = _layer_norm(DEEPNORM_ALPHA * x + acc, lng_ref[...], lnb_ref[...])
    o_ref[...] = out
    route_ref[...] = _route(out, wr_ref[...], br_ref[...])


def _gmlp_layer(x, w_in, v_g, v_b, w_s, b_s_t, w_out, ln_g, ln_b, wr_t, br):
    t = x.shape[0]
    row_spec = pl.BlockSpec((GMLP_TM, D_MODEL), lambda i: (i, 0))
    return pl.pallas_call(
        _gmlp_kernel,
        out_shape=(jax.ShapeDtypeStruct((t, D_MODEL), F32), jax.ShapeDtypeStruct((ROUTER_ROWS, t), F32)),
        grid=(t // GMLP_TM,),
        in_specs=[row_spec,
                  _const_spec(w_in.shape), _const_spec(v_g.shape), _const_spec(v_b.shape),
                  _const_spec(w_s.shape), _const_spec(b_s_t.shape), _const_spec(w_out.shape),
                  _const_spec(ln_g.shape), _const_spec(ln_b.shape),
                  _const_spec(wr_t.shape), _const_spec(br.shape)],
        out_specs=(row_spec, pl.BlockSpec((ROUTER_ROWS, GMLP_TM), lambda i: (0, i))),
        scratch_shapes=[pltpu.VMEM((GMLP_TM, GMLP_HALF), BF16),
                        pltpu.VMEM((GMLP_TM, GMLP_GROUPS_PER_STEP * GMLP_GROUP_WIDTH), F32)],
        compiler_params=_params("arbitrary"),
        name="gmlp_layer",
    )(x, w_in, v_g, v_b, w_s, b_s_t, w_out, ln_g, ln_b, wr_t, br)


def _route(x, wr_t, br):
    logits = lax.dot_general(wr_t, x.astype(BF16), _NT, preferred_element_type=F32) + br
    m = jnp.max(logits, axis=0, keepdims=True)
    ex = jnp.exp(logits - m)
    probs = ex / jnp.sum(ex, axis=0, keepdims=True)
    p = [probs[e:e + 1, :] for e in range(N_EXPERTS)]

    scores = []
    for g in range(N_GROUPS):
        q = p[g * EXPERTS_PER_GROUP:(g + 1) * EXPERTS_PER_GROUP]
        best = None
        for a in range(EXPERTS_PER_GROUP):
            for b in range(a + 1, EXPERTS_PER_GROUP):
                s = q[a] + q[b]
                best = s if best is None else jnp.maximum(best, s)
        scores.append(best)
    best_score = scores[0]
    gsel = jnp.zeros_like(best_score, dtype=jnp.int32)
    for g in range(1, N_GROUPS):
        better = scores[g] > best_score
        best_score = jnp.where(better, scores[g], best_score)
        gsel = jnp.where(better, g, gsel)

    slot_chosen = [None] * EXPERTS_PER_GROUP
    slot_gate = [None] * EXPERTS_PER_GROUP
    for e in range(N_EXPERTS):
        g, k = divmod(e, EXPERTS_PER_GROUP)
        rank = jnp.zeros_like(gsel)
        for e2 in range(g * EXPERTS_PER_GROUP, (g + 1) * EXPERTS_PER_GROUP):
            if e2 == e:
                continue
            ahead = (p[e2] > p[e]) | ((p[e2] == p[e]) & (e2 < e))
            rank = rank + ahead.astype(jnp.int32)
        chosen = (gsel == g) & (rank < TOP_K)
        gate = jnp.where(chosen, p[e] / best_score, 0.0)
        slot_chosen[k] = chosen if slot_chosen[k] is None else slot_chosen[k] | chosen
        slot_gate[k] = gate if slot_gate[k] is None else slot_gate[k] + gate
    c0, c1, c2, c3 = slot_chosen
    lo = jnp.where(c0, 0, jnp.where(c1, 1, 2))
    hi = jnp.where(c3, 3, jnp.where(c2, 2, 1))
    w_lo = jnp.where(c0, slot_gate[0], jnp.where(c1, slot_gate[1], slot_gate[2]))
    w_hi = jnp.where(c3, slot_gate[3], jnp.where(c2, slot_gate[2], slot_gate[1]))
    pair = jnp.right_shift(lo * (7 - lo), 1) + hi - lo - 1
    cls = gsel * MOE_PAIRS + pair
    zero = jnp.zeros_like(w_lo)
    return jnp.concatenate([cls.astype(F32), w_lo, w_hi] + [zero] * (ROUTER_ROWS - 3), axis=0)


MOE_TM = 256
FEATURE_ROWS = D_MODEL // V7X_LANES
DISPATCH_TD = 512
DMA_UNROLL = 8


def _token_rows(ref, n_tokens):
    return jnp.concatenate(
        [ref[pl.ds(c, n_tokens, stride=FEATURE_ROWS), :] for c in range(FEATURE_ROWS)], axis=1)


def _store_token_rows(ref, value, n_tokens):
    for c in range(FEATURE_ROWS):
        ref[pl.ds(c, n_tokens, stride=FEATURE_ROWS), :] = value[:, c * V7X_LANES:(c + 1) * V7X_LANES]


def _dispatch_kernel(pos_ref, x_ref, gate_ref, buf_in_ref, buf_ref, gate_tbl_ref, stage_ref, sems):
    del buf_in_ref
    i = pl.program_id(0)
    n = pl.num_programs(0)
    slot = i % 2
    stage = stage_ref.at[slot]
    rows = DISPATCH_TD * FEATURE_ROWS

    @pl.when(i == 0)
    def _():
        gate_tbl_ref[...] = jnp.zeros_like(gate_tbl_ref)

    def wait_slot(s):
        pltpu.make_async_copy(stage_ref.at[s], buf_ref.at[pl.ds(0, rows)], sems.at[s]).wait()

    @pl.when(i >= 2)
    def _():
        wait_slot(slot)

    _store_token_rows(stage, x_ref[...], DISPATCH_TD)

    def issue(r, _):
        p = pos_ref[i * DISPATCH_TD + r]
        dst = pl.multiple_of(p * FEATURE_ROWS, FEATURE_ROWS)
        src = pl.multiple_of(r * FEATURE_ROWS, FEATURE_ROWS)
        pltpu.make_async_copy(stage.at[pl.ds(src, FEATURE_ROWS)], buf_ref.at[pl.ds(dst, FEATURE_ROWS)],
                              sems.at[slot]).start()
        gate_tbl_ref[pl.ds(p, 1), :] = gate_ref[pl.ds(r, 1), :]
        return 0

    lax.fori_loop(0, DISPATCH_TD, issue, 0, unroll=DMA_UNROLL)

    @pl.when(i == n - 1)
    def _():
        wait_slot(slot)

        @pl.when(n > 1)
        def _():
            wait_slot(1 - slot)


def _dispatch(pos, x, gates, buf):
    t = x.shape[0]
    n_slots = buf.shape[0] // FEATURE_ROWS
    return pl.pallas_call(
        _dispatch_kernel,
        out_shape=(jax.ShapeDtypeStruct(buf.shape, buf.dtype),
                   jax.ShapeDtypeStruct((n_slots, V7X_LANES), F32)),
        grid_spec=pltpu.PrefetchScalarGridSpec(
            num_scalar_prefetch=1,
            grid=(t // DISPATCH_TD,),
            in_specs=[pl.BlockSpec((DISPATCH_TD, D_MODEL), lambda i, pos: (i, 0)),
                      pl.BlockSpec((DISPATCH_TD, V7X_LANES), lambda i, pos: (i, 0)),
                      pl.BlockSpec(memory_space=pl.ANY)],
            out_specs=(pl.BlockSpec(memory_space=pl.ANY),
                       pl.BlockSpec((n_slots, V7X_LANES), lambda i, pos: (0, 0))),
            scratch_shapes=[pltpu.VMEM((2, DISPATCH_TD * FEATURE_ROWS, V7X_LANES), F32),
                            pltpu.SemaphoreType.DMA((2,))]),
        input_output_aliases={3: 0},
        compiler_params=_params("arbitrary"),
        name="moe_dispatch",
    )(pos, x, gates, buf)


def _expert_kernel(ea_ref, eb_ref, fresh_ref, used_ref, xs_ref, gate_ref,
                   w1a_ref, w1b_ref, w3a_ref, w3b_ref, w2a_ref, w2b_ref, ys_ref,
                   w1_s, w3_s, w2_s):
    del ea_ref, eb_ref
    i = pl.program_id(0)

    @pl.when(fresh_ref[i] == 1)
    def _():
        for e, (w1, w3, w2) in enumerate(((w1a_ref, w3a_ref, w2a_ref), (w1b_ref, w3b_ref, w2b_ref))):
            w1_s[e] = w1[...].astype(BF16)
            w3_s[e] = w3[...].astype(BF16)
            w2_s[e] = w2[...].astype(BF16)

    @pl.when(used_ref[i] == 1)
    def _():
        xb = _token_rows(xs_ref, MOE_TM).astype(BF16)
        gates = gate_ref[...]
        y = None
        for e in range(TOP_K):
            h1 = jnp.dot(xb, w1_s[e], preferred_element_type=F32)
            h3 = jnp.dot(xb, w3_s[e], preferred_element_type=F32)
            h = (jax.nn.silu(h1) * h3 * gates[:, e:e + 1]).astype(BF16)
            ye = jnp.dot(h, w2_s[e], preferred_element_type=F32)
            y = ye if y is None else y + ye
        _store_token_rows(ys_ref, y, MOE_TM)

    @pl.when(used_ref[i] == 0)
    def _():
        ys_ref[...] = jnp.zeros_like(ys_ref)


def _expert_ffn(ea, eb, fresh, used, xs, gates, w1, w3, w2, layer):
    n_tiles = ea.shape[0]

    def w_spec(shape, which):
        return pl.BlockSpec((None, None) + shape,
                            lambda i, ea, eb, fresh, used: (layer, (ea, eb)[which][i], 0, 0))

    up, down = (D_MODEL, D_EXPERT), (D_EXPERT, D_MODEL)
    return pl.pallas_call(
        _expert_kernel,
        out_shape=jax.ShapeDtypeStruct((n_tiles * MOE_TM * FEATURE_ROWS, V7X_LANES), F32),
        grid_spec=pltpu.PrefetchScalarGridSpec(
            num_scalar_prefetch=4,
            grid=(n_tiles,),
            in_specs=[pl.BlockSpec((MOE_TM * FEATURE_ROWS, V7X_LANES), lambda i, *_: (i, 0)),
                      pl.BlockSpec((MOE_TM, V7X_LANES), lambda i, *_: (i, 0)),
                      w_spec(up, 0), w_spec(up, 1), w_spec(up, 0), w_spec(up, 1),
                      w_spec(down, 0), w_spec(down, 1)],
            out_specs=pl.BlockSpec((MOE_TM * FEATURE_ROWS, V7X_LANES), lambda i, *_: (i, 0)),
            scratch_shapes=[pltpu.VMEM((TOP_K,) + up, BF16), pltpu.VMEM((TOP_K,) + up, BF16),
                            pltpu.VMEM((TOP_K,) + down, BF16)]),
        compiler_params=_params("arbitrary"),
        name="moe_experts",
    )(ea, eb, fresh, used, xs, gates, w1, w1, w3, w3, w2, w2)


def _gather_and_norm(pos_ref, ys_ref, x_ref, lng_ref, lnb_ref, stage_ref, sems):
    i = pl.program_id(0)
    n = pl.num_programs(0)
    slot = i % 2
    rows = DISPATCH_TD * FEATURE_ROWS

    def issue_step(step, s):
        def issue(r, _):
            src = pl.multiple_of(pos_ref[step * DISPATCH_TD + r] * FEATURE_ROWS, FEATURE_ROWS)
            dst = pl.multiple_of(r * FEATURE_ROWS, FEATURE_ROWS)
            pltpu.make_async_copy(ys_ref.at[pl.ds(src, FEATURE_ROWS)],
                                  stage_ref.at[s, pl.ds(dst, FEATURE_ROWS)], sems.at[s]).start()
            return 0
        lax.fori_loop(0, DISPATCH_TD, issue, 0, unroll=DMA_UNROLL)

    @pl.when(i == 0)
    def _():
        issue_step(0, 0)

    @pl.when(i + 1 < n)
    def _():
        issue_step(i + 1, 1 - slot)

    pltpu.make_async_copy(ys_ref.at[pl.ds(0, rows)], stage_ref.at[slot], sems.at[slot]).wait()

    f = _token_rows(stage_ref.at[slot], DISPATCH_TD)
    return _layer_norm(DEEPNORM_ALPHA * x_ref[...] + f, lng_ref[...], lnb_ref[...])


def _combine_kernel(pos_ref, ys_ref, x_ref, lng_ref, lnb_ref, o_ref, stage_ref, sems):
    o_ref[...] = _gather_and_norm(pos_ref, ys_ref, x_ref, lng_ref, lnb_ref, stage_ref, sems)


def _combine_q_kernel(pos_ref, ys_ref, x_ref, lng_ref, lnb_ref, wq_ref, o_ref, qt_ref, stage_ref, sems,
                      *, q_scale):
    out = _gather_and_norm(pos_ref, ys_ref, x_ref, lng_ref, lnb_ref, stage_ref, sems)
    o_ref[...] = out
    q = lax.dot_general(wq_ref[...], out.astype(BF16), _NT, preferred_element_type=F32)
    qt_ref[...] = (q * q_scale).astype(qt_ref.dtype)


def _combine(pos, ys, x, ln_g, ln_b, wq_t=None, q_scale=None, batch=None):
    t = x.shape[0]
    row_spec = pl.BlockSpec((DISPATCH_TD, D_MODEL), lambda i, pos: (i, 0))
    vec_spec = pl.BlockSpec((1, D_MODEL), lambda i, pos: (0, 0))
    in_specs = [pl.BlockSpec(memory_space=pl.ANY), row_spec, vec_spec, vec_spec]
    out_shape = jax.ShapeDtypeStruct((t, D_MODEL), F32)
    out_specs = row_spec
    body, args = _combine_kernel, (pos, ys, x, ln_g, ln_b)
    if wq_t is not None:
        steps_per_batch = t // batch // DISPATCH_TD
        body = functools.partial(_combine_q_kernel, q_scale=q_scale)
        args = args + (wq_t,)
        in_specs = in_specs + [pl.BlockSpec(wq_t.shape, lambda i, pos: (0, 0), pipeline_mode=pl.Buffered(1))]
        out_shape = (out_shape, jax.ShapeDtypeStruct((batch, D_MODEL, t // batch), BF16))
        out_specs = (row_spec, pl.BlockSpec((None, D_MODEL, DISPATCH_TD),
                                            lambda i, pos: (i // steps_per_batch, 0, i % steps_per_batch)))
    return pl.pallas_call(
        body,
        out_shape=out_shape,
        grid_spec=pltpu.PrefetchScalarGridSpec(
            num_scalar_prefetch=1,
            grid=(t // DISPATCH_TD,),
            in_specs=in_specs,
            out_specs=out_specs,
            scratch_shapes=[pltpu.VMEM((2, DISPATCH_TD * FEATURE_ROWS, V7X_LANES), F32),
                            pltpu.SemaphoreType.DMA((2,))]),
        compiler_params=_params("arbitrary"),
        name="moe_combine",
    )(*args)


def _pair_table():
    lo, hi = [], []
    for g in range(N_GROUPS):
        for a in range(EXPERTS_PER_GROUP):
            for b in range(a + 1, EXPERTS_PER_GROUP):
                lo.append(g * EXPERTS_PER_GROUP + a)
                hi.append(g * EXPERTS_PER_GROUP + b)
    return np.asarray(lo, np.int32), np.asarray(hi, np.int32)


def _sort_plan(cls, n_tiles):
    onehot = (cls[:, None] == jnp.arange(MOE_CLASSES, dtype=jnp.int32)[None, :]).astype(jnp.int32)
    running = jnp.cumsum(onehot, axis=0)
    counts = running[-1]
    padded = (counts + MOE_TM - 1) // MOE_TM * MOE_TM
    ends = jnp.cumsum(padded)
    rank = jnp.sum(running * onehot, axis=1) - 1
    pos = jnp.sum(onehot * (ends - padded)[None, :], axis=1) + rank
    tile_cls = jnp.searchsorted(ends, jnp.arange(n_tiles, dtype=jnp.int32) * MOE_TM, side="right")
    used = (tile_cls < MOE_CLASSES).astype(jnp.int32)
    tile_cls = jnp.minimum(tile_cls, MOE_CLASSES - 1).astype(jnp.int32)
    fresh = jnp.concatenate([jnp.ones((1,), jnp.int32),
                             (tile_cls[1:] != tile_cls[:-1]).astype(jnp.int32)])
    lo, hi = _pair_table()
    return pos.astype(jnp.int32), jnp.asarray(lo)[tile_cls], jnp.asarray(hi)[tile_cls], fresh, used


def _moe_layer(x, routed, buf, w1, w3, w2, ln_g, ln_b, layer, **next_q):
    n_tiles = buf.shape[0] // (MOE_TM * FEATURE_ROWS)
    pos, ea, eb, fresh, used = _sort_plan(routed[0].astype(jnp.int32), n_tiles)
    gate_rows = jnp.pad(routed[1:1 + TOP_K].T, ((0, 0), (0, V7X_LANES - TOP_K)))
    buf, gates = _dispatch(pos, x, gate_rows, buf)
    ys = _expert_ffn(ea, eb, fresh, used, buf, gates, w1, w3, w2, layer)
    return _combine(pos, ys, x, ln_g, ln_b, **next_q), buf


PROJ_TM = 512


def _kv_kernel(x_ref, wk_ref, wvt_ref, k_ref, vt_ref, kmean_ref):
    xb = x_ref[...].astype(BF16)
    k = jnp.dot(xb, wk_ref[...], preferred_element_type=F32)
    k_ref[...] = k.astype(BF16)
    for j in range(PROJ_TM // MOBA_BLOCK):
        kmean_ref[j] = jnp.mean(k[j * MOBA_BLOCK:(j + 1) * MOBA_BLOCK], axis=0, keepdims=True)
    vt_ref[...] = lax.dot_general(wvt_ref[...], xb, _NT, preferred_element_type=F32).astype(BF16)


def _project_kv(x, w_k, w_v_t):
    b, s, _ = x.shape
    blocks_per_tile = PROJ_TM // MOBA_BLOCK
    tiles = s // PROJ_TM
    return pl.pallas_call(
        _kv_kernel,
        out_shape=(jax.ShapeDtypeStruct((b, s, D_MODEL), BF16),
                   jax.ShapeDtypeStruct((b, D_MODEL, s), BF16),
                   jax.ShapeDtypeStruct((b * s // MOBA_BLOCK, 1, D_MODEL), F32)),
        grid=(b, tiles),
        in_specs=[pl.BlockSpec((None, PROJ_TM, D_MODEL), lambda bi, i: (bi, i, 0)),
                  _const_spec(w_k.shape), _const_spec(w_v_t.shape)],
        out_specs=(pl.BlockSpec((None, PROJ_TM, D_MODEL), lambda bi, i: (bi, i, 0)),
                   pl.BlockSpec((None, D_MODEL, PROJ_TM), lambda bi, i: (bi, 0, i)),
                   pl.BlockSpec((blocks_per_tile, 1, D_MODEL), lambda bi, i: (bi * tiles + i, 0, 0))),
        compiler_params=_params("arbitrary", "arbitrary"),
        name="project_kv",
    )(x, w_k, w_v_t)


def _t5_bucket_np(dist):
    n = np.maximum(dist, 0)
    max_exact = N_BUCKETS // 2
    nf = np.maximum(n, 1).astype(np.float32)
    large = max_exact + (np.log(nf / np.float32(max_exact)) / np.float32(math.log(MAX_DISTANCE / max_exact))
                         * np.float32(N_BUCKETS - max_exact)).astype(np.int32)
    large = np.minimum(large, N_BUCKETS - 1)
    return np.where(n < max_exact, n, large).astype(np.int32)


def _bucket_tables():
    kpos = np.arange(MOBA_BLOCK)[:, None]
    qpos = np.arange(MOBA_BLOCK)[None, :]
    own = np.where(qpos >= kpos, _t5_bucket_np(qpos - kpos), -1)
    prev = _t5_bucket_np(qpos + MOBA_BLOCK - kpos)
    return np.stack([own, prev]).astype(np.int32)


def _bias_kernel(rel_ref, bucket_ref, o_ref):
    h = pl.program_id(0)
    far = rel_ref[N_BUCKETS - 1, h]
    for t in range(2):
        bucket = bucket_ref[t]
        acc = jnp.zeros(bucket.shape, F32)
        for b in range(N_BUCKETS):
            acc = jnp.where(bucket == b, rel_ref[b, h], acc)
        o_ref[t] = jnp.where(bucket < 0, NEG_INF, (acc - far) * LOG2_E)


def _bias_tables(rel_bias):
    buckets = jnp.asarray(_bucket_tables())
    return pl.pallas_call(
        _bias_kernel,
        out_shape=jax.ShapeDtypeStruct((N_HEADS, 2, MOBA_BLOCK, MOBA_BLOCK), F32),
        grid=(N_HEADS,),
        in_specs=[pl.BlockSpec(memory_space=pltpu.SMEM), _const_spec(buckets.shape)],
        out_specs=pl.BlockSpec((None, 2, MOBA_BLOCK, MOBA_BLOCK), lambda h: (h, 0, 0, 0)),
        compiler_params=_params("arbitrary"),
        name="rel_bias_tables",
    )(rel_bias, buckets)


ATTN_HEADS = 8
ATTN_WIDTH = ATTN_HEADS * HEAD_DIM
HEADS_PER_LANE_TILE = V7X_LANES // HEAD_DIM
FAR_TILES_PER_STEP = 3
QK_LOOKAHEAD = 5
LOG2_E = math.log2(math.e)


def _attn_kernel(qt_ref, k_ref, vt_ref, kmean_ref, bias_ref, o_ref,
                 sel_ref, ml_ref, acc_ref, qp_ref, sc_ref):
    n_blocks = kmean_ref.shape[0]
    blk = lax.broadcasted_iota(jnp.int32, (n_blocks, MOBA_BLOCK), 0)
    causal = (lax.broadcasted_iota(jnp.int32, (MOBA_BLOCK, MOBA_BLOCK), 0)
              <= lax.broadcasted_iota(jnp.int32, (MOBA_BLOCK, MOBA_BLOCK), 1))
    ones_rows = jnp.ones((BF16_SUBLANE_TILE, MOBA_BLOCK), BF16)

    def lane_slab(h):
        return slice((h // HEADS_PER_LANE_TILE) * V7X_LANES, (h // HEADS_PER_LANE_TILE + 1) * V7X_LANES)

    def padded_q(h, q0):
        qh = qt_ref[h * HEAD_DIM:(h + 1) * HEAD_DIM, pl.ds(q0, MOBA_BLOCK)]
        pieces = [jnp.zeros_like(qh)] * HEADS_PER_LANE_TILE
        pieces[h % HEADS_PER_LANE_TILE] = qh
        return jnp.concatenate(pieces, axis=0)

    def scores(h, j):
        k0 = pl.multiple_of(j * MOBA_BLOCK, MOBA_BLOCK)
        kj = k_ref[pl.ds(k0, MOBA_BLOCK), lane_slab(h)]
        return jnp.dot(kj, qp_ref[h], preferred_element_type=F32)

    def weights_and_values(h, j, s, bias, sel):
        if bias is not None:
            keep = causal if sel is None else sel
            s = jnp.where(keep, s, NEG_INF) + bias[...]
        m_loc = jnp.max(s, axis=0, keepdims=True)
        if sel is None:
            shift = m_loc
        else:
            shift = jnp.where(sel, m_loc, -NEG_INF)
            m_loc = jnp.where(sel, m_loc, NEG_INF)
        p = jnp.exp2(s - shift).astype(BF16)
        k0 = pl.multiple_of(j * MOBA_BLOCK, MOBA_BLOCK)
        vtj = vt_ref[h * HEAD_DIM:(h + 1) * HEAD_DIM, pl.ds(k0, MOBA_BLOCK)]
        pv = jnp.dot(jnp.concatenate([vtj, ones_rows], axis=0), p, preferred_element_type=F32)
        return m_loc, pv[HEAD_DIM:HEAD_DIM + 1], pv[:HEAD_DIM]

    def run_tiles(tiles, carried_in, prefetch):
        n = len(tiles)
        s = [None] * n
        out = [None] * n
        for t in range(QK_LOOKAHEAD if carried_in else 0, n + QK_LOOKAHEAD):
            if t < n:
                h, j, _, _ = tiles[t]
                s[t] = scores(h, j)
            else:
                h, j = prefetch[t - n]
                sc_ref[t - n] = scores(h, j)
            d = t - QK_LOOKAHEAD
            if d >= 0:
                h, j, bias, sel = tiles[d]
                sd = sc_ref[d] if carried_in and d < QK_LOOKAHEAD else s[d]
                out[d] = weights_and_values(h, j, sd, bias, sel)
                s[d] = None
        return out

    def far_tile_ids(t):
        return [(h, t * FAR_TILES_PER_STEP + u) for h in range(ATTN_HEADS) for u in range(FAR_TILES_PER_STEP)]

    def merge(h, parts, first):
        m_new = parts[0][0]
        for m_loc, _, _ in parts[1:]:
            m_new = jnp.maximum(m_new, m_loc)
        if first:
            l_new = jnp.zeros_like(m_new)
            acc_new = jnp.zeros((HEAD_DIM, MOBA_BLOCK), F32)
        else:
            m_old = ml_ref[h, 0:1, :]
            m_new = jnp.maximum(m_new, m_old)
            alpha = jnp.exp2(m_old - m_new)
            l_new = alpha * ml_ref[h, 1:2, :]
            acc_new = alpha * acc_ref[h]
        for m_loc, l_loc, pv in parts:
            w = jnp.exp2(m_loc - m_new)
            l_new = l_new + w * l_loc
            acc_new = acc_new + w * pv
        ml_ref[h, 0:1, :] = m_new
        ml_ref[h, 1:2, :] = l_new
        acc_ref[h] = acc_new

    def q_block(i, _):
        q0 = pl.multiple_of(i * MOBA_BLOCK, MOBA_BLOCK)
        jp = jnp.maximum(i - 1, 0)
        n_far = jnp.maximum(i - 1, 0)
        for h in range(ATTN_HEADS):
            qp_ref[h] = padded_q(h, q0)

        for h in range(ATTN_HEADS):
            gate = jnp.dot(kmean_ref[:, lane_slab(h)].astype(BF16), qp_ref[h], preferred_element_type=F32)
            gate = jnp.where(blk < i, gate, NEG_INF)
            sel = jnp.zeros(gate.shape, F32)
            for r in range(MOBA_TOPK):
                top = jnp.max(gate, axis=0, keepdims=True)
                first = jnp.min(jnp.where(gate == top, blk, n_blocks), axis=0, keepdims=True)
                hit = blk == first
                sel = jnp.where(hit & (r < i), 1.0, sel)
                gate = jnp.where(hit, -jnp.inf, gate)
            sel_ref[h] = sel

        tiles = []
        for h in range(ATTN_HEADS):
            tiles.append((h, i, bias_ref.at[h, 0], None))
            tiles.append((h, jp, bias_ref.at[h, 1], sel_ref[h, pl.ds(jp, 1), :] > 0.5))
        last_far = jnp.maximum(n_far - 1, 0)

        def prefetch_ids(t):
            return [(h, jnp.minimum(j, last_far)) for h, j in far_tile_ids(t)[:QK_LOOKAHEAD]]

        parts = run_tiles(tiles, carried_in=False, prefetch=prefetch_ids(0))
        for h in range(ATTN_HEADS):
            merge(h, parts[2 * h:2 * h + 2], first=True)

        def far_step(t, _):
            tiles = []
            for h, j in far_tile_ids(t):
                jc = jnp.minimum(j, last_far)
                sel = (sel_ref[h, pl.ds(jc, 1), :] > 0.5) & (j < n_far)
                tiles.append((h, jc, None, sel))
            parts = run_tiles(tiles, carried_in=True, prefetch=prefetch_ids(t + 1))
            for h in range(ATTN_HEADS):
                merge(h, parts[h * FAR_TILES_PER_STEP:(h + 1) * FAR_TILES_PER_STEP], first=False)
            return 0

        lax.fori_loop(0, (n_far + FAR_TILES_PER_STEP - 1) // FAR_TILES_PER_STEP, far_step, 0)

        for slab in range(ATTN_HEADS // HEADS_PER_LANE_TILE):
            heads = range(slab * HEADS_PER_LANE_TILE, (slab + 1) * HEADS_PER_LANE_TILE)
            out_t = jnp.concatenate([acc_ref[h] / ml_ref[h, 1:2, :] for h in heads], axis=0)
            o_ref[pl.ds(q0, MOBA_BLOCK), slab * V7X_LANES:(slab + 1) * V7X_LANES] = out_t.T.astype(o_ref.dtype)
        return 0

    lax.fori_loop(0, n_blocks, q_block, 0)


def _moba_attention(qt, k, vt, kmean, bias):
    b, _, s = qt.shape
    n_blocks = s // MOBA_BLOCK
    single = dict(pipeline_mode=pl.Buffered(1))
    t_spec = pl.BlockSpec((None, ATTN_WIDTH, s), lambda bi, hi: (bi, hi, 0), **single)
    return pl.pallas_call(
        _attn_kernel,
        out_shape=jax.ShapeDtypeStruct((b, s, D_MODEL), BF16),
        grid=(b, N_HEADS // ATTN_HEADS),
        in_specs=[t_spec,
                  pl.BlockSpec((None, s, ATTN_WIDTH), lambda bi, hi: (bi, 0, hi), **single),
                  t_spec,
                  pl.BlockSpec((None, n_blocks, ATTN_WIDTH), lambda bi, hi: (bi, 0, hi)),
                  pl.BlockSpec((ATTN_HEADS, 2, MOBA_BLOCK, MOBA_BLOCK), lambda bi, hi: (hi, 0, 0, 0), **single)],
        out_specs=pl.BlockSpec((None, s, ATTN_WIDTH), lambda bi, hi: (bi, 0, hi)),
        scratch_shapes=[pltpu.VMEM((ATTN_HEADS, n_blocks, MOBA_BLOCK), F32),
                        pltpu.VMEM((ATTN_HEADS, 2, MOBA_BLOCK), F32),
                        pltpu.VMEM((ATTN_HEADS, HEAD_DIM, MOBA_BLOCK), F32),
                        pltpu.VMEM((ATTN_HEADS, V7X_LANES, MOBA_BLOCK), BF16),
                        pltpu.VMEM((QK_LOOKAHEAD, MOBA_BLOCK, MOBA_BLOCK), F32)],
        compiler_params=_params("arbitrary", "arbitrary"),
        name="moba_attention",
    )(qt, k, vt, kmean, bias)


OPROJ_TM = 512


def _oproj_kernel(a_ref, w_ref, x_ref, lng_ref, lnb_ref, wr_ref, br_ref, o_ref, route_ref):
    h = jnp.dot(a_ref[...], w_ref[...], preferred_element_type=F32)
    out = _layer_norm(DEEPNORM_ALPHA * x_ref[...] + h, lng_ref[...], lnb_ref[...])
    o_ref[...] = out
    route_ref[...] = _route(out, wr_ref[...], br_ref[...])


def _out_project(a, w_o, x, ln_g, ln_b, wr_t, br):
    t = x.shape[0]
    row_spec = pl.BlockSpec((OPROJ_TM, D_MODEL), lambda i: (i, 0))
    return pl.pallas_call(
        _oproj_kernel,
        out_shape=(jax.ShapeDtypeStruct((t, D_MODEL), F32), jax.ShapeDtypeStruct((ROUTER_ROWS, t), F32)),
        grid=(t // OPROJ_TM,),
        in_specs=[row_spec, _const_spec(w_o.shape), row_spec,
                  _const_spec(ln_g.shape), _const_spec(ln_b.shape),
                  _const_spec(wr_t.shape), _const_spec(br.shape)],
        out_specs=(row_spec, pl.BlockSpec((ROUTER_ROWS, OPROJ_TM), lambda i: (0, i))),
        compiler_params=_params("arbitrary"),
        name="out_project",
    )(a, w_o, x, ln_g, ln_b, wr_t, br)


def kernel(x, ln_g, ln_b, a_w_in, a_v_g, a_v_b, a_w_s, a_b_s, a_w_out, w_k, w_v, b_w_q, b_w_o,
           rel_bias, w_router, b_router, e_w1, e_w3, e_w2):
    batch, seq, _ = x.shape
    t = batch * seq
    n_blocks = seq // MOBA_BLOCK
    xt = x.reshape(t, D_MODEL)
    wr_t = w_router.T.astype(BF16)
    br = b_router.reshape(N_EXPERTS, 1)
    n_tiles = t // MOE_TM + MOE_CLASSES
    sorted_buf = jnp.zeros((n_tiles * MOE_TM * FEATURE_ROWS, V7X_LANES), F32)

    def ln_rows(l, j):
        return ln_g[l, j].reshape(1, D_MODEL), ln_b[l, j].reshape(1, D_MODEL)

    q_scale = HEAD_DIM ** -0.5 * LOG2_E
    k = vt = kmean = bias = qt = None
    for l in range(DEPTH):
        g0, b0 = ln_rows(l, 0)
        if l < N_A_LAYERS:
            xt, routed = _gmlp_layer(xt, a_w_in[l].astype(BF16),
                                     a_v_g[l].reshape(1, GMLP_HALF), a_v_b[l].reshape(1, GMLP_HALF),
                                     a_w_s[l], a_b_s[l].T, a_w_out[l].astype(BF16), g0, b0, wr_t, br)
        else:
            x3 = xt.reshape(batch, seq, D_MODEL)
            if l == N_A_LAYERS:
                k, vt, kmean = _project_kv(x3, w_k.astype(BF16), w_v.T.astype(BF16))
                kmean = kmean.reshape(batch, n_blocks, D_MODEL)
                bias = _bias_tables(rel_bias)
            j = l - N_A_LAYERS
            a = _moba_attention(qt, k, vt, kmean, bias)
            xt, routed = _out_project(a.reshape(t, D_MODEL), b_w_o[j].astype(BF16), xt, g0, b0, wr_t, br)
        g1, b1 = ln_rows(l, 1)
        if N_A_LAYERS <= l + 1 < DEPTH:
            next_q = dict(wq_t=b_w_q[l + 1 - N_A_LAYERS].T.astype(BF16), q_scale=q_scale, batch=batch)
            (xt, qt), sorted_buf = _moe_layer(xt, routed, sorted_buf, e_w1, e_w3, e_w2, g1, b1, layer=l, **next_q)
        else:
            xt, sorted_buf = _moe_layer(xt, routed, sorted_buf, e_w1, e_w3, e_w2, g1, b1, layer=l)
    return xt.reshape(batch, seq, D_MODEL)
```

```python
import functools
import math

import numpy as np
import jax
import jax.numpy as jnp
from jax import lax
from jax.experimental import pallas as pl
from jax.experimental.pallas import tpu as pltpu

D_MODEL = 1024
DEPTH = 4
N_A_LAYERS = DEPTH // 2
GMLP_HALF = 3 * D_MODEL
GMLP_GROUPS = 8
GMLP_GROUP_WIDTH = GMLP_HALF // GMLP_GROUPS
GMLP_CHUNK = 128
N_HEADS = 16
HEAD_DIM = D_MODEL // N_HEADS
MOBA_BLOCK = 256
MOBA_TOPK = 3
N_BUCKETS = 32
MAX_DISTANCE = 128
N_EXPERTS = 16
N_GROUPS = 4
EXPERTS_PER_GROUP = N_EXPERTS // N_GROUPS
TOP_K = 2
MOE_PAIRS = EXPERTS_PER_GROUP * (EXPERTS_PER_GROUP - 1) // 2
MOE_CLASSES = N_GROUPS * MOE_PAIRS
ROUTER_ROWS = 8
D_EXPERT = D_MODEL // 4
DEEPNORM_ALPHA = (2 * DEPTH) ** 0.25
LN_EPS = 1e-5
NEG_INF = -1e30

V7X_VMEM_LIMIT_BYTES = 56 * 1024 * 1024
V7X_LANES = 128
BF16_SUBLANE_TILE = 16

BF16 = jnp.bfloat16
F32 = jnp.float32

_NT = (((1,), (1,)), ((), ()))


def _params(*semantics):
    return pltpu.CompilerParams(dimension_semantics=semantics,
                                vmem_limit_bytes=V7X_VMEM_LIMIT_BYTES)


def _const_spec(shape):
    zeros = (0,) * len(shape)
    return pl.BlockSpec(shape, lambda *_: zeros, pipeline_mode=pl.Buffered(1))


def _layer_norm(y, g, b):
    mu = jnp.mean(y, axis=-1, keepdims=True)
    d = y - mu
    var = jnp.mean(d * d, axis=-1, keepdims=True)
    return d * lax.rsqrt(var + LN_EPS) * g + b


GMLP_TM = 512
GMLP_GROUPS_PER_STEP = 2


def _gmlp_kernel(x_ref, w_in_ref, vg_ref, vb_ref, ws_ref, bs_ref, w_out_ref,
                 lng_ref, lnb_ref, wr_ref, br_ref, o_ref, route_ref, vn_ref, sv_ref):
    x = x_ref[...]
    xb = x.astype(BF16)
    v = jax.nn.gelu(jnp.dot(xb, w_in_ref[:, GMLP_HALF:], preferred_element_type=F32))
    vn_ref[...] = _layer_norm(v, vg_ref[...], vb_ref[...]).astype(BF16)

    row = lax.broadcasted_iota(jnp.int32, (GMLP_CHUNK, GMLP_CHUNK), 0)
    col = lax.broadcasted_iota(jnp.int32, (GMLP_CHUNK, GMLP_CHUNK), 1)
    causal = row >= col
    width = GMLP_GROUPS_PER_STEP * GMLP_GROUP_WIDTH
    acc = jnp.zeros((GMLP_TM, D_MODEL), F32)
    for step in range(GMLP_GROUPS // GMLP_GROUPS_PER_STEP):
        lo = step * width
        u = jax.nn.gelu(jnp.dot(xb, w_in_ref[:, lo:lo + width], preferred_element_type=F32))
        for gi in range(GMLP_GROUPS_PER_STEP):
            g = step * GMLP_GROUPS_PER_STEP + gi
            w_s = jnp.where(causal, ws_ref[g], 0.0).astype(BF16)
            b_s = bs_ref[:, g:g + 1]
            glo = g * GMLP_GROUP_WIDTH
            for c in range(GMLP_TM // GMLP_CHUNK):
                rows = slice(c * GMLP_CHUNK, (c + 1) * GMLP_CHUNK)
                vc = vn_ref[rows, glo:glo + GMLP_GROUP_WIDTH]
                sv_ref[rows, gi * GMLP_GROUP_WIDTH:(gi + 1) * GMLP_GROUP_WIDTH] = (
                    jnp.dot(w_s, vc, preferred_element_type=F32) + b_s)
        gated = (u * sv_ref[...]).astype(BF16)
        acc = acc + jnp.dot(gated, w_out_ref[lo:lo + width, :], preferred_element_type=F32)
    out = _layer_norm(DEEPNORM_ALPHA * x + acc, lng_ref[...], lnb_ref[...])
    o_ref[...] = out
    route_ref[...] = _route(out, wr_ref[...], br_ref[...])


def _gmlp_layer(x, w_in, v_g, v_b, w_s, b_s_t, w_out, ln_g, ln_b, wr_t, br):
    t = x.shape[0]
    row_spec = pl.BlockSpec((GMLP_TM, D_MODEL), lambda i: (i, 0))
    return pl.pallas_call(
        _gmlp_kernel,
        out_shape=(jax.ShapeDtypeStruct((t, D_MODEL), F32), jax.ShapeDtypeStruct((ROUTER_ROWS, t), F32)),
        grid=(t // GMLP_TM,),
        in_specs=[row_spec,
                  _const_spec(w_in.shape), _const_spec(v_g.shape), _const_spec(v_b.shape),
                  _const_spec(w_s.shape), _const_spec(b_s_t.shape), _const_spec(w_out.shape),
                  _const_spec(ln_g.shape), _const_spec(ln_b.shape),
                  _const_spec(wr_t.shape), _const_spec(br.shape)],
        out_specs=(row_spec, pl.BlockSpec((ROUTER_ROWS, GMLP_TM), lambda i: (0, i))),
        scratch_shapes=[pltpu.VMEM((GMLP_TM, GMLP_HALF), BF16),
                        pltpu.VMEM((GMLP_TM, GMLP_GROUPS_PER_STEP * GMLP_GROUP_WIDTH), F32)],
        compiler_params=_params("arbitrary"),
        name="gmlp_layer",
    )(x, w_in, v_g, v_b, w_s, b_s_t, w_out, ln_g, ln_b, wr_t, br)


def _route(x, wr_t, br):
    logits = lax.dot_general(wr_t, x.astype(BF16), _NT, preferred_element_type=F32) + br
    m = jnp.max(logits, axis=0, keepdims=True)
    ex = jnp.exp(logits - m)
    probs = ex / jnp.sum(ex, axis=0, keepdims=True)
    p = [probs[e:e + 1, :] for e in range(N_EXPERTS)]

    scores = []
    for g in range(N_GROUPS):
        q = p[g * EXPERTS_PER_GROUP:(g + 1) * EXPERTS_PER_GROUP]
        best = None
        for a in range(EXPERTS_PER_GROUP):
            for b in range(a + 1, EXPERTS_PER_GROUP):
                s = q[a] + q[b]
                best = s if best is None else jnp.maximum(best, s)
        scores.append(best)
    best_score = scores[0]
    gsel = jnp.zeros_like(best_score, dtype=jnp.int32)
    for g in range(1, N_GROUPS):
        better = scores[g] > best_score
        best_score = jnp.where(better, scores[g], best_score)
        gsel = jnp.where(better, g, gsel)

    slot_chosen = [None] * EXPERTS_PER_GROUP
    slot_gate = [None] * EXPERTS_PER_GROUP
    for e in range(N_EXPERTS):
        g, k = divmod(e, EXPERTS_PER_GROUP)
        rank = jnp.zeros_like(gsel)
        for e2 in range(g * EXPERTS_PER_GROUP, (g + 1) * EXPERTS_PER_GROUP):
            if e2 == e:
                continue
            ahead = (p[e2] > p[e]) | ((p[e2] == p[e]) & (e2 < e))
            rank = rank + ahead.astype(jnp.int32)
        chosen = (gsel == g) & (rank < TOP_K)
        gate = jnp.where(chosen, p[e] / best_score, 0.0)
        slot_chosen[k] = chosen if slot_chosen[k] is None else slot_chosen[k] | chosen
        slot_gate[k] = gate if slot_gate[k] is None else slot_gate[k] + gate
    c0, c1, c2, c3 = slot_chosen
    lo = jnp.where(c0, 0, jnp.where(c1, 1, 2))
    hi = jnp.where(c3, 3, jnp.where(c2, 2, 1))
    w_lo = jnp.where(c0, slot_gate[0], jnp.where(c1, slot_gate[1], slot_gate[2]))
    w_hi = jnp.where(c3, slot_gate[3], jnp.where(c2, slot_gate[2], slot_gate[1]))
    pair = jnp.right_shift(lo * (7 - lo), 1) + hi - lo - 1
    cls = gsel * MOE_PAIRS + pair
    zero = jnp.zeros_like(w_lo)
    return jnp.concatenate([cls.astype(F32), w_lo, w_hi] + [zero] * (ROUTER_ROWS - 3), axis=0)


MOE_TM = 256
FEATURE_ROWS = D_MODEL // V7X_LANES
DISPATCH_TD = 512
DMA_UNROLL = 8


def _token_rows(ref, n_tokens):
    return jnp.concatenate(
        [ref[pl.ds(c, n_tokens, stride=FEATURE_ROWS), :] for c in range(FEATURE_ROWS)], axis=1)


def _store_token_rows(ref, value, n_tokens):
    for c in range(FEATURE_ROWS):
        ref[pl.ds(c, n_tokens, stride=FEATURE_ROWS), :] = value[:, c * V7X_LANES:(c + 1) * V7X_LANES]


def _dispatch_kernel(pos_ref, x_ref, gate_ref, buf_in_ref, buf_ref, gate_tbl_ref, stage_ref, sems):
    del buf_in_ref
    i = pl.program_id(0)
    n = pl.num_programs(0)
    slot = i % 2
    stage = stage_ref.at[slot]
    rows = DISPATCH_TD * FEATURE_ROWS

    @pl.when(i == 0)
    def _():
        gate_tbl_ref[...] = jnp.zeros_like(gate_tbl_ref)

    def wait_slot(s):
        pltpu.make_async_copy(stage_ref.at[s], buf_ref.at[pl.ds(0, rows)], sems.at[s]).wait()

    @pl.when(i >= 2)
    def _():
        wait_slot(slot)

    _store_token_rows(stage, x_ref[...], DISPATCH_TD)

    def issue(g, _):
        for u in range(DMA_UNROLL):
            r = g * DMA_UNROLL + u
            p = pos_ref[i * DISPATCH_TD + r]
            dst = pl.multiple_of(p * FEATURE_ROWS, FEATURE_ROWS)
            src = pl.multiple_of(r * FEATURE_ROWS, FEATURE_ROWS)
            pltpu.make_async_copy(stage.at[pl.ds(src, FEATURE_ROWS)], buf_ref.at[pl.ds(dst, FEATURE_ROWS)],
                                  sems.at[slot]).start(priority=u % 2)
            gate_tbl_ref[pl.ds(p, 1), :] = gate_ref[pl.ds(r, 1), :]
        return 0

    lax.fori_loop(0, DISPATCH_TD // DMA_UNROLL, issue, 0)

    @pl.when(i == n - 1)
    def _():
        wait_slot(slot)

        @pl.when(n > 1)
        def _():
            wait_slot(1 - slot)


def _dispatch(pos, x, gates, buf):
    t = x.shape[0]
    n_slots = buf.shape[0] // FEATURE_ROWS
    return pl.pallas_call(
        _dispatch_kernel,
        out_shape=(jax.ShapeDtypeStruct(buf.shape, buf.dtype),
                   jax.ShapeDtypeStruct((n_slots, V7X_LANES), F32)),
        grid_spec=pltpu.PrefetchScalarGridSpec(
            num_scalar_prefetch=1,
            grid=(t // DISPATCH_TD,),
            in_specs=[pl.BlockSpec((DISPATCH_TD, D_MODEL), lambda i, pos: (i, 0)),
                      pl.BlockSpec((DISPATCH_TD, V7X_LANES), lambda i, pos: (i, 0)),
                      pl.BlockSpec(memory_space=pl.ANY)],
            out_specs=(pl.BlockSpec(memory_space=pl.ANY),
                       pl.BlockSpec((n_slots, V7X_LANES), lambda i, pos: (0, 0))),
            scratch_shapes=[pltpu.VMEM((2, DISPATCH_TD * FEATURE_ROWS, V7X_LANES), F32),
                            pltpu.SemaphoreType.DMA((2,))]),
        input_output_aliases={3: 0},
        compiler_params=_params("arbitrary"),
        name="moe_dispatch",
    )(pos, x, gates, buf)


def _expert_kernel(ea_ref, eb_ref, fresh_ref, used_ref, xs_ref, gate_ref,
                   w1a_ref, w1b_ref, w3a_ref, w3b_ref, w2a_ref, w2b_ref, ys_ref,
                   w1_s, w3_s, w2_s):
    del ea_ref, eb_ref
    i = pl.program_id(0)

    @pl.when(fresh_ref[i] == 1)
    def _():
        for e, (w1, w3, w2) in enumerate(((w1a_ref, w3a_ref, w2a_ref), (w1b_ref, w3b_ref, w2b_ref))):
            w1_s[e] = w1[...].astype(BF16)
            w3_s[e] = w3[...].astype(BF16)
            w2_s[e] = w2[...].astype(BF16)

    @pl.when(used_ref[i] == 1)
    def _():
        xb = _token_rows(xs_ref, MOE_TM).astype(BF16)
        gates = gate_ref[...]
        y = None
        for e in range(TOP_K):
            h1 = jnp.dot(xb, w1_s[e], preferred_element_type=F32)
            h3 = jnp.dot(xb, w3_s[e], preferred_element_type=F32)
            h = (jax.nn.silu(h1) * h3 * gates[:, e:e + 1]).astype(BF16)
            ye = jnp.dot(h, w2_s[e], preferred_element_type=F32)
            y = ye if y is None else y + ye
        _store_token_rows(ys_ref, y, MOE_TM)

    @pl.when(used_ref[i] == 0)
    def _():
        ys_ref[...] = jnp.zeros_like(ys_ref)


def _expert_ffn(ea, eb, fresh, used, xs, gates, w1, w3, w2, layer):
    n_tiles = ea.shape[0]

    def w_spec(shape, which):
        return pl.BlockSpec((None, None) + shape,
                            lambda i, ea, eb, fresh, used: (layer, (ea, eb)[which][i], 0, 0))

    up, down = (D_MODEL, D_EXPERT), (D_EXPERT, D_MODEL)
    return pl.pallas_call(
        _expert_kernel,
        out_shape=jax.ShapeDtypeStruct((n_tiles * MOE_TM * FEATURE_ROWS, V7X_LANES), F32),
        grid_spec=pltpu.PrefetchScalarGridSpec(
            num_scalar_prefetch=4,
            grid=(n_tiles,),
            in_specs=[pl.BlockSpec((MOE_TM * FEATURE_ROWS, V7X_LANES), lambda i, *_: (i, 0)),
                      pl.BlockSpec((MOE_TM, V7X_LANES), lambda i, *_: (i, 0)),
                      w_spec(up, 0), w_spec(up, 1), w_spec(up, 0), w_spec(up, 1),
                      w_spec(down, 0), w_spec(down, 1)],
            out_specs=pl.BlockSpec((MOE_TM * FEATURE_ROWS, V7X_LANES), lambda i, *_: (i, 0)),
            scratch_shapes=[pltpu.VMEM((TOP_K,) + up, BF16), pltpu.VMEM((TOP_K,) + up, BF16),
                            pltpu.VMEM((TOP_K,) + down, BF16)]),
        compiler_params=_params("arbitrary"),
        name="moe_experts",
    )(ea, eb, fresh, used, xs, gates, w1, w1, w3, w3, w2, w2)


def _gather_and_norm(pos_ref, ys_ref, x_ref, lng_ref, lnb_ref, stage_ref, sems):
    i = pl.program_id(0)
    n = pl.num_programs(0)
    slot = i % 2
    rows = DISPATCH_TD * FEATURE_ROWS

    def issue_step(step, s):
        def issue(g, _):
            for u in range(DMA_UNROLL):
                r = g * DMA_UNROLL + u
                src = pl.multiple_of(pos_ref[step * DISPATCH_TD + r] * FEATURE_ROWS, FEATURE_ROWS)
                dst = pl.multiple_of(r * FEATURE_ROWS, FEATURE_ROWS)
                pltpu.make_async_copy(ys_ref.at[pl.ds(src, FEATURE_ROWS)],
                                      stage_ref.at[s, pl.ds(dst, FEATURE_ROWS)], sems.at[s]).start(priority=u % 2)
            return 0
        lax.fori_loop(0, DISPATCH_TD // DMA_UNROLL, issue, 0)

    @pl.when(i == 0)
    def _():
        issue_step(0, 0)

    @pl.when(i + 1 < n)
    def _():
        issue_step(i + 1, 1 - slot)

    pltpu.make_async_copy(ys_ref.at[pl.ds(0, rows)], stage_ref.at[slot], sems.at[slot]).wait()

    f = _token_rows(stage_ref.at[slot], DISPATCH_TD)
    return _layer_norm(DEEPNORM_ALPHA * x_ref[...] + f, lng_ref[...], lnb_ref[...])


def _combine_kernel(pos_ref, ys_ref, x_ref, lng_ref, lnb_ref, o_ref, stage_ref, sems):
    o_ref[...] = _gather_and_norm(pos_ref, ys_ref, x_ref, lng_ref, lnb_ref, stage_ref, sems)


def _combine_q_kernel(pos_ref, ys_ref, x_ref, lng_ref, lnb_ref, wq_ref, o_ref, qt_ref, stage_ref, sems,
                      *, q_scale):
    out = _gather_and_norm(pos_ref, ys_ref, x_ref, lng_ref, lnb_ref, stage_ref, sems)
    o_ref[...] = out
    q = lax.dot_general(wq_ref[...], out.astype(BF16), _NT, preferred_element_type=F32)
    qt_ref[...] = (q * q_scale).astype(qt_ref.dtype)


def _combine(pos, ys, x, ln_g, ln_b, wq_t=None, q_scale=None, batch=None):
    t = x.shape[0]
    row_spec = pl.BlockSpec((DISPATCH_TD, D_MODEL), lambda i, pos: (i, 0))
    vec_spec = pl.BlockSpec((1, D_MODEL), lambda i, pos: (0, 0))
    in_specs = [pl.BlockSpec(memory_space=pl.ANY), row_spec, vec_spec, vec_spec]
    out_shape = jax.ShapeDtypeStruct((t, D_MODEL), F32)
    out_specs = row_spec
    body, args = _combine_kernel, (pos, ys, x, ln_g, ln_b)
    if wq_t is not None:
        steps_per_batch = t // batch // DISPATCH_TD
        body = functools.partial(_combine_q_kernel, q_scale=q_scale)
        args = args + (wq_t,)
        in_specs = in_specs + [pl.BlockSpec(wq_t.shape, lambda i, pos: (0, 0), pipeline_mode=pl.Buffered(1))]
        out_shape = (out_shape, jax.ShapeDtypeStruct((batch, D_MODEL, t // batch), BF16))
        out_specs = (row_spec, pl.BlockSpec((None, D_MODEL, DISPATCH_TD),
                                            lambda i, pos: (i // steps_per_batch, 0, i % steps_per_batch)))
    return pl.pallas_call(
        body,
        out_shape=out_shape,
        grid_spec=pltpu.PrefetchScalarGridSpec(
            num_scalar_prefetch=1,
            grid=(t // DISPATCH_TD,),
            in_specs=in_specs,
            out_specs=out_specs,
            scratch_shapes=[pltpu.VMEM((2, DISPATCH_TD * FEATURE_ROWS, V7X_LANES), F32),
                            pltpu.SemaphoreType.DMA((2,))]),
        compiler_params=_params("arbitrary"),
        name="moe_combine",
    )(*args)


def _pair_table():
    lo, hi = [], []
    for g in range(N_GROUPS):
        for a in range(EXPERTS_PER_GROUP):
            for b in range(a + 1, EXPERTS_PER_GROUP):
                lo.append(g * EXPERTS_PER_GROUP + a)
                hi.append(g * EXPERTS_PER_GROUP + b)
    return np.asarray(lo, np.int32), np.asarray(hi, np.int32)


def _sort_plan(cls, n_tiles):
    onehot = (cls[:, None] == jnp.arange(MOE_CLASSES, dtype=jnp.int32)[None, :]).astype(jnp.int32)
    running = jnp.cumsum(onehot, axis=0)
    counts = running[-1]
    padded = (counts + MOE_TM - 1) // MOE_TM * MOE_TM
    ends = jnp.cumsum(padded)
    rank = jnp.sum(running * onehot, axis=1) - 1
    pos = jnp.sum(onehot * (ends - padded)[None, :], axis=1) + rank
    tile_cls = jnp.searchsorted(ends, jnp.arange(n_tiles, dtype=jnp.int32) * MOE_TM, side="right")
    used = (tile_cls < MOE_CLASSES).astype(jnp.int32)
    tile_cls = jnp.minimum(tile_cls, MOE_CLASSES - 1).astype(jnp.int32)
    fresh = jnp.concatenate([jnp.ones((1,), jnp.int32),
                             (tile_cls[1:] != tile_cls[:-1]).astype(jnp.int32)])
    lo, hi = _pair_table()
    return pos.astype(jnp.int32), jnp.asarray(lo)[tile_cls], jnp.asarray(hi)[tile_cls], fresh, used


def _moe_layer(x, routed, buf, w1, w3, w2, ln_g, ln_b, layer, **next_q):
    n_tiles = buf.shape[0] // (MOE_TM * FEATURE_ROWS)
    pos, ea, eb, fresh, used = _sort_plan(routed[0].astype(jnp.int32), n_tiles)
    gate_rows = jnp.pad(routed[1:1 + TOP_K].T, ((0, 0), (0, V7X_LANES - TOP_K)))
    buf, gates = _dispatch(pos, x, gate_rows, buf)
    ys = _expert_ffn(ea, eb, fresh, used, buf, gates, w1, w3, w2, layer)
    return _combine(pos, ys, x, ln_g, ln_b, **next_q), buf


PROJ_TM = 512


def _kv_kernel(x_ref, wk_ref, wvt_ref, k_ref, vt_ref, kmean_ref):
    xb = x_ref[...].astype(BF16)
    k = jnp.dot(xb, wk_ref[...], preferred_element_type=F32)
    k_ref[...] = k.astype(BF16)
    for j in range(PROJ_TM // MOBA_BLOCK):
        kmean_ref[j] = jnp.mean(k[j * MOBA_BLOCK:(j + 1) * MOBA_BLOCK], axis=0, keepdims=True)
    vt_ref[...] = lax.dot_general(wvt_ref[...], xb, _NT, preferred_element_type=F32).astype(BF16)


def _project_kv(x, w_k, w_v_t):
    b, s, _ = x.shape
    blocks_per_tile = PROJ_TM // MOBA_BLOCK
    tiles = s // PROJ_TM
    return pl.pallas_call(
        _kv_kernel,
        out_shape=(jax.ShapeDtypeStruct((b, s, D_MODEL), BF16),
                   jax.ShapeDtypeStruct((b, D_MODEL, s), BF16),
                   jax.ShapeDtypeStruct((b * s // MOBA_BLOCK, 1, D_MODEL), F32)),
        grid=(b, tiles),
        in_specs=[pl.BlockSpec((None, PROJ_TM, D_MODEL), lambda bi, i: (bi, i, 0)),
                  _const_spec(w_k.shape), _const_spec(w_v_t.shape)],
        out_specs=(pl.BlockSpec((None, PROJ_TM, D_MODEL), lambda bi, i: (bi, i, 0)),
                   pl.BlockSpec((None, D_MODEL, PROJ_TM), lambda bi, i: (bi, 0, i)),
                   pl.BlockSpec((blocks_per_tile, 1, D_MODEL), lambda bi, i: (bi * tiles + i, 0, 0))),
        compiler_params=_params("arbitrary", "arbitrary"),
        name="project_kv",
    )(x, w_k, w_v_t)


def _t5_bucket_np(dist):
    n = np.maximum(dist, 0)
    max_exact = N_BUCKETS // 2
    nf = np.maximum(n, 1).astype(np.float32)
    large = max_exact + (np.log(nf / np.float32(max_exact)) / np.float32(math.log(MAX_DISTANCE / max_exact))
                         * np.float32(N_BUCKETS - max_exact)).astype(np.int32)
    large = np.minimum(large, N_BUCKETS - 1)
    return np.where(n < max_exact, n, large).astype(np.int32)


def _bucket_tables():
    kpos = np.arange(MOBA_BLOCK)[:, None]
    qpos = np.arange(MOBA_BLOCK)[None, :]
    own = np.where(qpos >= kpos, _t5_bucket_np(qpos - kpos), -1)
    prev = _t5_bucket_np(qpos + MOBA_BLOCK - kpos)
    return np.stack([own, prev]).astype(np.int32)


def _bias_kernel(rel_ref, bucket_ref, o_ref):
    h = pl.program_id(0)
    far = rel_ref[N_BUCKETS - 1, h]
    for t in range(2):
        bucket = bucket_ref[t]
        acc = jnp.zeros(bucket.shape, F32)
        for b in range(N_BUCKETS):
            acc = jnp.where(bucket == b, rel_ref[b, h], acc)
        o_ref[t] = jnp.where(bucket < 0, NEG_INF, (acc - far) * LOG2_E)


def _bias_tables(rel_bias):
    buckets = jnp.asarray(_bucket_tables())
    return pl.pallas_call(
        _bias_kernel,
        out_shape=jax.ShapeDtypeStruct((N_HEADS, 2, MOBA_BLOCK, MOBA_BLOCK), F32),
        grid=(N_HEADS,),
        in_specs=[pl.BlockSpec(memory_space=pltpu.SMEM), _const_spec(buckets.shape)],
        out_specs=pl.BlockSpec((None, 2, MOBA_BLOCK, MOBA_BLOCK), lambda h: (h, 0, 0, 0)),
        compiler_params=_params("arbitrary"),
        name="rel_bias_tables",
    )(rel_bias, buckets)


ATTN_HEADS = 8
ATTN_WIDTH = ATTN_HEADS * HEAD_DIM
HEADS_PER_LANE_TILE = V7X_LANES // HEAD_DIM
FAR_TILES_PER_STEP = 3
QK_LOOKAHEAD = 5
LOG2_E = math.log2(math.e)


def _attn_kernel(qt_ref, k_ref, vt_ref, kmean_ref, bias_ref, o_ref,
                 sel_ref, ml_ref, acc_ref, qp_ref, sc_ref):
    n_blocks = kmean_ref.shape[0]
    blk = lax.broadcasted_iota(jnp.int32, (n_blocks, MOBA_BLOCK), 0)
    causal = (lax.broadcasted_iota(jnp.int32, (MOBA_BLOCK, MOBA_BLOCK), 0)
              <= lax.broadcasted_iota(jnp.int32, (MOBA_BLOCK, MOBA_BLOCK), 1))
    ones_rows = jnp.ones((BF16_SUBLANE_TILE, MOBA_BLOCK), BF16)

    def lane_slab(h):
        return slice((h // HEADS_PER_LANE_TILE) * V7X_LANES, (h // HEADS_PER_LANE_TILE + 1) * V7X_LANES)

    def padded_q(h, q0):
        qh = qt_ref[h * HEAD_DIM:(h + 1) * HEAD_DIM, pl.ds(q0, MOBA_BLOCK)]
        pieces = [jnp.zeros_like(qh)] * HEADS_PER_LANE_TILE
        pieces[h % HEADS_PER_LANE_TILE] = qh
        return jnp.concatenate(pieces, axis=0)

    def scores(h, j):
        k0 = pl.multiple_of(j * MOBA_BLOCK, MOBA_BLOCK)
        kj = k_ref[pl.ds(k0, MOBA_BLOCK), lane_slab(h)]
        return jnp.dot(kj, qp_ref[h], preferred_element_type=F32)

    def weights_and_values(h, j, s, bias, sel):
        if bias is not None:
            keep = causal if sel is None else sel
            s = jnp.where(keep, s, NEG_INF) + bias[...]
        m_loc = jnp.max(s, axis=0, keepdims=True)
        if sel is None:
            shift = m_loc
        else:
            shift = jnp.where(sel, m_loc, -NEG_INF)
            m_loc = jnp.where(sel, m_loc, NEG_INF)
        p = jnp.exp2(s - shift).astype(BF16)
        k0 = pl.multiple_of(j * MOBA_BLOCK, MOBA_BLOCK)
        vtj = vt_ref[h * HEAD_DIM:(h + 1) * HEAD_DIM, pl.ds(k0, MOBA_BLOCK)]
        pv = jnp.dot(jnp.concatenate([vtj, ones_rows], axis=0), p, preferred_element_type=F32)
        return m_loc, pv[HEAD_DIM:HEAD_DIM + 1], pv[:HEAD_DIM]

    def run_tiles(tiles, carried_in, prefetch):
        n = len(tiles)
        s = [None] * n
        out = [None] * n
        for t in range(QK_LOOKAHEAD if carried_in else 0, n + QK_LOOKAHEAD):
            if t < n:
                h, j, _, _ = tiles[t]
                s[t] = scores(h, j)
            else:
                h, j = prefetch[t - n]
                sc_ref[t - n] = scores(h, j)
            d = t - QK_LOOKAHEAD
            if d >= 0:
                h, j, bias, sel = tiles[d]
                sd = sc_ref[d] if carried_in and d < QK_LOOKAHEAD else s[d]
                out[d] = weights_and_values(h, j, sd, bias, sel)
                s[d] = None
        return out

    def far_tile_ids(t):
        return [(h, t * FAR_TILES_PER_STEP + u) for h in range(ATTN_HEADS) for u in range(FAR_TILES_PER_STEP)]

    def merge(h, parts, first):
        m_new = parts[0][0]
        for m_loc, _, _ in parts[1:]:
            m_new = jnp.maximum(m_new, m_loc)
        if first:
            l_new = jnp.zeros_like(m_new)
            acc_new = jnp.zeros((HEAD_DIM, MOBA_BLOCK), F32)
        else:
            m_old = ml_ref[h, 0:1, :]
            m_new = jnp.maximum(m_new, m_old)
            alpha = jnp.exp2(m_old - m_new)
            l_new = alpha * ml_ref[h, 1:2, :]
            acc_new = alpha * acc_ref[h]
        for m_loc, l_loc, pv in parts:
            w = jnp.exp2(m_loc - m_new)
            l_new = l_new + w * l_loc
            acc_new = acc_new + w * pv
        ml_ref[h, 0:1, :] = m_new
        ml_ref[h, 1:2, :] = l_new
        acc_ref[h] = acc_new

    def q_block(i, _):
        q0 = pl.multiple_of(i * MOBA_BLOCK, MOBA_BLOCK)
        jp = jnp.maximum(i - 1, 0)
        n_far = jnp.maximum(i - 1, 0)
        for h in range(ATTN_HEADS):
            qp_ref[h] = padded_q(h, q0)

        for h in range(ATTN_HEADS):
            gate = jnp.dot(kmean_ref[:, lane_slab(h)].astype(BF16), qp_ref[h], preferred_element_type=F32)
            gate = jnp.where(blk < i, gate, NEG_INF)
            sel = jnp.zeros(gate.shape, F32)
            for r in range(MOBA_TOPK):
                top = jnp.max(gate, axis=0, keepdims=True)
                first = jnp.min(jnp.where(gate == top, blk, n_blocks), axis=0, keepdims=True)
                hit = blk == first
                sel = jnp.where(hit & (r < i), 1.0, sel)
                gate = jnp.where(hit, -jnp.inf, gate)
            sel_ref[h] = sel

        tiles = []
        for h in range(ATTN_HEADS):
            tiles.append((h, i, bias_ref.at[h, 0], None))
            tiles.append((h, jp, bias_ref.at[h, 1], sel_ref[h, pl.ds(jp, 1), :] > 0.5))
        last_far = jnp.maximum(n_far - 1, 0)

        def prefetch_ids(t):
            return [(h, jnp.minimum(j, last_far)) for h, j in far_tile_ids(t)[:QK_LOOKAHEAD]]

        parts = run_tiles(tiles, carried_in=False, prefetch=prefetch_ids(0))
        for h in range(ATTN_HEADS):
            merge(h, parts[2 * h:2 * h + 2], first=True)

        def far_step(t, _):
            tiles = []
            for h, j in far_tile_ids(t):
                jc = jnp.minimum(j, last_far)
                sel = (sel_ref[h, pl.ds(jc, 1), :] > 0.5) & (j < n_far)
                tiles.append((h, jc, None, sel))
            parts = run_tiles(tiles, carried_in=True, prefetch=prefetch_ids(t + 1))
            for h in range(ATTN_HEADS):
                merge(h, parts[h * FAR_TILES_PER_STEP:(h + 1) * FAR_TILES_PER_STEP], first=False)
            return 0

        lax.fori_loop(0, (n_far + FAR_TILES_PER_STEP - 1) // FAR_TILES_PER_STEP, far_step, 0)

        for slab in range(ATTN_HEADS // HEADS_PER_LANE_TILE):
            heads = range(slab * HEADS_PER_LANE_TILE, (slab + 1) * HEADS_PER_LANE_TILE)
            out_t = jnp.concatenate([acc_ref[h] / ml_ref[h, 1:2, :] for h in heads], axis=0)
            o_ref[pl.ds(q0, MOBA_BLOCK), slab * V7X_LANES:(slab + 1) * V7X_LANES] = out_t.T.astype(o_ref.dtype)
        return 0

    lax.fori_loop(0, n_blocks, q_block, 0)


def _moba_attention(qt, k, vt, kmean, bias):
    b, _, s = qt.shape
    n_blocks = s // MOBA_BLOCK
    single = dict(pipeline_mode=pl.Buffered(1))
    t_spec = pl.BlockSpec((None, ATTN_WIDTH, s), lambda bi, hi: (bi, hi, 0), **single)
    return pl.pallas_call(
        _attn_kernel,
        out_shape=jax.ShapeDtypeStruct((b, s, D_MODEL), BF16),
        grid=(b, N_HEADS // ATTN_HEADS),
        in_specs=[t_spec,
                  pl.BlockSpec((None, s, ATTN_WIDTH), lambda bi, hi: (bi, 0, hi), **single),
                  t_spec,
                  pl.BlockSpec((None, n_blocks, ATTN_WIDTH), lambda bi, hi: (bi, 0, hi)),
                  pl.BlockSpec((ATTN_HEADS, 2, MOBA_BLOCK, MOBA_BLOCK), lambda bi, hi: (hi, 0, 0, 0), **single)],
        out_specs=pl.BlockSpec((None, s, ATTN_WIDTH), lambda bi, hi: (bi, 0, hi)),
        scratch_shapes=[pltpu.VMEM((ATTN_HEADS, n_blocks, MOBA_BLOCK), F32),
                        pltpu.VMEM((ATTN_HEADS, 2, MOBA_BLOCK), F32),
                        pltpu.VMEM((ATTN_HEADS, HEAD_DIM, MOBA_BLOCK), F32),
                        pltpu.VMEM((ATTN_HEADS, V7X_LANES, MOBA_BLOCK), BF16),
                        pltpu.VMEM((QK_LOOKAHEAD, MOBA_BLOCK, MOBA_BLOCK), F32)],
        compiler_params=_params("arbitrary", "arbitrary"),
        name="moba_attention",
    )(qt, k, vt, kmean, bias)


OPROJ_TM = 512


def _oproj_kernel(a_ref, w_ref, x_ref, lng_ref, lnb_ref, wr_ref, br_ref, o_ref, route_ref):
    h = jnp.dot(a_ref[...], w_ref[...], preferred_element_type=F32)
    out = _layer_norm(DEEPNORM_ALPHA * x_ref[...] + h, lng_ref[...], lnb_ref[...])
    o_ref[...] = out
    route_ref[...] = _route(out, wr_ref[...], br_ref[...])


def _out_project(a, w_o, x, ln_g, ln_b, wr_t, br):
    t = x.shape[0]
    row_spec = pl.BlockSpec((OPROJ_TM, D_MODEL), lambda i: (i, 0))
    return pl.pallas_call(
        _oproj_kernel,
        out_shape=(jax.ShapeDtypeStruct((t, D_MODEL), F32), jax.ShapeDtypeStruct((ROUTER_ROWS, t), F32)),
        grid=(t // OPROJ_TM,),
        in_specs=[row_spec, _const_spec(w_o.shape), row_spec,
                  _const_spec(ln_g.shape), _const_spec(ln_b.shape),
                  _const_spec(wr_t.shape), _const_spec(br.shape)],
        out_specs=(row_spec, pl.BlockSpec((ROUTER_ROWS, OPROJ_TM), lambda i: (0, i))),
        compiler_params=_params("arbitrary"),
        name="out_project",
    )(a, w_o, x, ln_g, ln_b, wr_t, br)


def kernel(x, ln_g, ln_b, a_w_in, a_v_g, a_v_b, a_w_s, a_b_s, a_w_out, w_k, w_v, b_w_q, b_w_o,
           rel_bias, w_router, b_router, e_w1, e_w3, e_w2):
    batch, seq, _ = x.shape
    t = batch * seq
    n_blocks = seq // MOBA_BLOCK
    xt = x.reshape(t, D_MODEL)
    wr_t = w_router.T.astype(BF16)
    br = b_router.reshape(N_EXPERTS, 1)
    n_tiles = t // MOE_TM + MOE_CLASSES
    sorted_buf = jnp.zeros((n_tiles * MOE_TM * FEATURE_ROWS, V7X_LANES), F32)

    def ln_rows(l, j):
        return ln_g[l, j].reshape(1, D_MODEL), ln_b[l, j].reshape(1, D_MODEL)

    q_scale = HEAD_DIM ** -0.5 * LOG2_E
    k = vt = kmean = bias = qt = None
    for l in range(DEPTH):
        g0, b0 = ln_rows(l, 0)
        if l < N_A_LAYERS:
            xt, routed = _gmlp_layer(xt, a_w_in[l].astype(BF16),
                                     a_v_g[l].reshape(1, GMLP_HALF), a_v_b[l].reshape(1, GMLP_HALF),
                                     a_w_s[l], a_b_s[l].T, a_w_out[l].astype(BF16), g0, b0, wr_t, br)
        else:
            x3 = xt.reshape(batch, seq, D_MODEL)
            if l == N_A_LAYERS:
                k, vt, kmean = _project_kv(x3, w_k.astype(BF16), w_v.T.astype(BF16))
                kmean = kmean.reshape(batch, n_blocks, D_MODEL)
                bias = _bias_tables(rel_bias)
            j = l - N_A_LAYERS
            a = _moba_attention(qt, k, vt, kmean, bias)
            xt, routed = _out_project(a.reshape(t, D_MODEL), b_w_o[j].astype(BF16), xt, g0, b0, wr_t, br)
        g1, b1 = ln_rows(l, 1)
        if N_A_LAYERS <= l + 1 < DEPTH:
            next_q = dict(wq_t=b_w_q[l + 1 - N_A_LAYERS].T.astype(BF16), q_scale=q_scale, batch=batch)
            (xt, qt), sorted_buf = _moe_layer(xt, routed, sorted_buf, e_w1, e_w3, e_w2, g1, b1, layer=l, **next_q)
        else:
            xt, sorted_buf = _moe_layer(xt, routed, sorted_buf, e_w1, e_w3, e_w2, g1, b1, layer=l)
    return xt.reshape(batch, seq, D_MODEL)
```
